```python
import math
import jax
import jax.numpy as jnp
from jax import lax
import numpy as np

D_MODEL = 1024
BATCH = 8
SEQ = 4096
DEPTH = 2

D_MIX = D_MODEL
SSM_WIDTH = D_MIX // 4
POOL_WIDTH = D_MIX // 4
ATT_WIDTH = D_MIX // 2
N_PROJ = SSM_WIDTH + POOL_WIDTH + 3 * ATT_WIDTH
SSM_GROUP = 16
SSM_GROUPS = SSM_WIDTH // SSM_GROUP
SSM_STATE = 64
SSM_DT_MIN = 1e-3
SSM_DT_MAX = 1e-1
POOL_WINDOWS = (2, 4, 8, 16)
POOL_GROUP = POOL_WIDTH // len(POOL_WINDOWS)
ATT_HEAD_DIM = 64
ATT_HEADS = ATT_WIDTH // ATT_HEAD_DIM
MOBA_BLOCK = 256
MOBA_TOPK = 3
MOBA_Q_CHUNK = 32
REL_BUCKETS = 32
REL_MAX_DIST = 128
MOE_GROUPS = 4
MOE_EXPERTS_PER_GROUP = 4
MOE_EXPERTS = MOE_GROUPS * MOE_EXPERTS_PER_GROUP
MOE_TOPK = 2
D_EXPERT = D_MODEL // 4
RMS_EPS = 1e-6
NEG_INF = -1e30

kernel_name = 'hybrid_s5_pool_moba_hmoe'


def rms_norm(x, g):
    xf = x.astype(jnp.float32)
    y = xf * lax.rsqrt(jnp.mean(xf * xf, axis=-1, keepdims=True) + RMS_EPS)
    return (y * g.astype(jnp.float32)).astype(x.dtype)


def t5_bucket(rel):
    n = jnp.maximum(rel, 0)
    max_exact = REL_BUCKETS // 2
    nf = jnp.maximum(n, 1).astype(jnp.float32)
    large = max_exact + (jnp.log(nf / max_exact) / math.log(REL_MAX_DIST / max_exact)
                         * (REL_BUCKETS - max_exact)).astype(jnp.int32)
    large = jnp.minimum(large, REL_BUCKETS - 1)
    return jnp.where(n < max_exact, n, large)


def ssm_mixer(u, a_re, a_im, log_dt, b_re, b_im, c_re, c_im, d, glu_w, glu_b):
    f32 = jnp.float32
    bsz, seq, _ = u.shape
    lam = lax.complex(a_re.astype(f32), a_im.astype(f32))
    dt = jnp.exp(log_dt.astype(f32))[:, None]
    lam_bar = jnp.exp(lam * dt)
    b = lax.complex(b_re.astype(f32), b_im.astype(f32))
    b_bar = ((lam_bar - 1.0) / lam)[..., None] * b
    c = lax.complex(c_re.astype(f32), c_im.astype(f32))
    uf = u.astype(f32)
    ug = uf.reshape(bsz, seq, SSM_GROUPS, SSM_GROUP).astype(jnp.complex64)
    bu = jnp.einsum('blgh,gph->blgp', ug, b_bar)
    a = jnp.broadcast_to(lam_bar, bu.shape)

    def combine(e1, e2):
        a1, b1 = e1
        a2, b2 = e2
        return a1 * a2, a2 * b1 + b2

    _, states = lax.associative_scan(combine, (a, bu), axis=1)
    y = jnp.einsum('blgp,ghp->blgh', states, c).real.reshape(bsz, seq, SSM_WIDTH)
    y = jax.nn.gelu(y + d.astype(f32) * uf)
    y = y * jax.nn.sigmoid(y @ glu_w.astype(f32) + glu_b.astype(f32))
    return y.astype(u.dtype)


def pool_mixer(p, w, b, scale):
    f32 = jnp.float32
    bsz, seq, _ = p.shape
    pf = p.astype(f32)
    cs = lax.cumsum(pf, axis=1)
    t = jnp.arange(seq)
    outs = []
    for gi, win in enumerate(POOL_WINDOWS):
        sl = slice(gi * POOL_GROUP, (gi + 1) * POOL_GROUP)
        csg = cs[..., sl]
        lag = jnp.pad(csg, ((0, 0), (win, 0), (0, 0)))[:, :seq]
        cnt = jnp.minimum(t + 1, win).astype(f32)[None, :, None]
        outs.append((csg - lag) / cnt - pf[..., sl])
    dlt = jnp.stack(outs, axis=2)
    y = jnp.einsum('blgc,gcd->blgd', dlt, w.astype(f32)).reshape(bsz, seq, POOL_WIDTH)
    y = (y + b.astype(f32)) * scale.astype(f32)
    return y.astype(p.dtype)


def moba_attention(q, k, v, rel_bias):
    f32 = jnp.float32
    bsz, seq, nh, dh = q.shape
    nb = -(-seq // MOBA_BLOCK)
    lp = nb * MOBA_BLOCK
    pad = lp - seq

    def to_bhld(t):
        return jnp.pad(t, ((0, 0), (0, pad), (0, 0), (0, 0))).transpose(0, 2, 1, 3)

    qh = to_bhld(q)
    kb = to_bhld(k).reshape(bsz, nh, nb, MOBA_BLOCK, dh)
    vb = to_bhld(v).reshape(bsz, nh, nb, MOBA_BLOCK, dh)
    kmean = jnp.mean(kb.astype(f32), axis=3)
    n_sel = min(MOBA_TOPK, nb - 1)
    bias_t = rel_bias.astype(f32).T
    b_ix = jnp.arange(bsz)[:, None, None, None]
    h_ix = jnp.arange(nh)[None, :, None, None]
    offs = jnp.arange(MOBA_BLOCK)
    scale = dh ** -0.5

    def chunk(ci):
        q0 = ci * MOBA_Q_CHUNK
        qc = lax.dynamic_slice_in_dim(qh, q0, MOBA_Q_CHUNK, axis=2).astype(f32)
        qpos = q0 + jnp.arange(MOBA_Q_CHUNK)
        qblk = q0 // MOBA_BLOCK
        k_own = lax.dynamic_index_in_dim(kb, qblk, axis=2, keepdims=False).astype(f32)
        v_own = lax.dynamic_index_in_dim(vb, qblk, axis=2, keepdims=False).astype(f32)
        rel_own = qpos[:, None] - (qblk * MOBA_BLOCK + offs)[None, :]
        s_own = jnp.einsum('bhqd,bhkd->bhqk', qc, k_own) * scale + bias_t[:, t5_bucket(rel_own)]
        s_own = jnp.where(rel_own >= 0, s_own, NEG_INF)
        if n_sel == 0:
            return jnp.einsum('bhqk,bhkd->bhqd', jax.nn.softmax(s_own, axis=-1), v_own)
        gate = jnp.einsum('bhqd,bhnd->bhqn', qc, kmean)
        gate = jnp.where(jnp.arange(nb) < qblk, gate, NEG_INF)
        _, sel = lax.top_k(gate, n_sel)
        k_sel = kb[b_ix, h_ix, sel].astype(f32)
        v_sel = vb[b_ix, h_ix, sel].astype(f32)
        rel_sel = qpos[:, None, None] - (sel[..., None] * MOBA_BLOCK + offs)
        s_sel = (jnp.einsum('bhqd,bhqnkd->bhqnk', qc, k_sel) * scale
                 + bias_t[h_ix[..., None], t5_bucket(rel_sel)])
        s_sel = jnp.where((sel < qblk)[..., None], s_sel, NEG_INF)
        logits = jnp.concatenate(
            [s_sel.reshape(bsz, nh, MOBA_Q_CHUNK, n_sel * MOBA_BLOCK), s_own], axis=-1)
        probs = jax.nn.softmax(logits, axis=-1)
        p_sel = probs[..., :n_sel * MOBA_BLOCK].reshape(bsz, nh, MOBA_Q_CHUNK, n_sel, MOBA_BLOCK)
        p_own = probs[..., n_sel * MOBA_BLOCK:]
        return (jnp.einsum('bhqnk,bhqnkd->bhqd', p_sel, v_sel)
                + jnp.einsum('bhqk,bhkd->bhqd', p_own, v_own))

    out = lax.map(chunk, jnp.arange(lp // MOBA_Q_CHUNK))
    out = out.transpose(1, 0, 3, 2, 4).reshape(bsz, lp, nh * dh)[:, :seq]
    return out.astype(q.dtype)


def hier_moe(h, group_w, group_b, router_w, router_b, w_gate, w_up, w_down):
    f32 = jnp.float32
    bsz, seq, dm = h.shape
    t = h.reshape(-1, dm)
    g_logits = (t @ group_w + group_b).astype(f32)
    g_prob = jax.nn.softmax(g_logits, axis=-1)
    g_sel = jnp.argmax(g_logits, axis=-1)
    g_wt = jnp.take_along_axis(g_prob, g_sel[:, None], axis=1)[:, 0]
    e_all = jnp.einsum('td,gde->tge', t, router_w) + router_b
    e_logits = jnp.take_along_axis(e_all, g_sel[:, None, None], axis=1)[:, 0].astype(f32)
    top_v, top_i = lax.top_k(e_logits, MOE_TOPK)
    top_w = jax.nn.softmax(top_v, axis=-1)
    within = jnp.sum(jax.nn.one_hot(top_i, MOE_EXPERTS_PER_GROUP, dtype=f32) * top_w[..., None], axis=1)
    comb = ((jax.nn.one_hot(g_sel, MOE_GROUPS, dtype=f32) * g_wt[:, None])[:, :, None]
            * within[:, None, :]).reshape(-1, MOE_EXPERTS)
    hg = jnp.einsum('td,edf->tef', t, w_gate)
    hu = jnp.einsum('td,edf->tef', t, w_up)
    act = jax.nn.silu(hg) * hu * comb[..., None]
    y = jnp.einsum('tef,efd->td', act, w_down)
    return y.reshape(bsz, seq, dm).astype(h.dtype)


def setup_inputs(seed: int = 0) -> dict:
    key = jax.random.key(seed)
    ks = jax.random.split(key, 32)
    f32 = jnp.float32

    def nrm(k, shape, s):
        return jax.random.normal(k, shape, f32) * s

    L, G, P, H = DEPTH, SSM_GROUPS, SSM_STATE, SSM_GROUP
    x = nrm(ks[0], (BATCH, SEQ, D_MODEL), 1.0)
    rel_bias = nrm(ks[1], (REL_BUCKETS, ATT_HEADS), 0.5)
    norm1_g = 1.0 + nrm(ks[2], (L, D_MODEL), 0.02)
    w_in = nrm(ks[3], (L, D_MODEL, N_PROJ), D_MODEL ** -0.5)
    ssm_a_re = -0.5 + nrm(ks[4], (L, G, P), 0.01)
    ssm_a_im = jnp.pi * jnp.arange(P, dtype=f32)[None, None, :] + nrm(ks[5], (L, G, P), 0.01)
    ssm_log_dt = jax.random.uniform(ks[6], (L, G), f32, math.log(SSM_DT_MIN), math.log(SSM_DT_MAX))
    ssm_b_re = nrm(ks[7], (L, G, P, H), (2.0 * H) ** -0.5)
    ssm_b_im = nrm(ks[8], (L, G, P, H), (2.0 * H) ** -0.5)
    ssm_c_re = nrm(ks[9], (L, G, H, P), (2.0 * P) ** -0.5 * 4.0)
    ssm_c_im = nrm(ks[10], (L, G, H, P), (2.0 * P) ** -0.5 * 4.0)
    ssm_d = nrm(ks[11], (L, SSM_WIDTH), 1.0)
    ssm_glu_w = nrm(ks[12], (L, SSM_WIDTH, SSM_WIDTH), SSM_WIDTH ** -0.5)
    ssm_glu_b = nrm(ks[13], (L, SSM_WIDTH), 0.02)
    pool_w = nrm(ks[14], (L, len(POOL_WINDOWS), POOL_GROUP, POOL_GROUP), POOL_GROUP ** -0.5)
    pool_b = nrm(ks[15], (L, POOL_WIDTH), 0.02)
    pool_scale = 1.0 + nrm(ks[16], (L, POOL_WIDTH), 0.02)
    w_out = nrm(ks[17], (L, D_MIX, D_MODEL), D_MIX ** -0.5)
    norm2_g = 1.0 + nrm(ks[18], (L, D_MODEL), 0.02)
    moe_group_w = nrm(ks[19], (L, D_MODEL, MOE_GROUPS), D_MODEL ** -0.5)
    moe_group_b = nrm(ks[20], (L, MOE_GROUPS), 0.01)
    moe_router_w = nrm(ks[21], (L, MOE_GROUPS, D_MODEL, MOE_EXPERTS_PER_GROUP), D_MODEL ** -0.5)
    moe_router_b = nrm(ks[22], (L, MOE_GROUPS, MOE_EXPERTS_PER_GROUP), 0.01)
    moe_w_gate = nrm(ks[23], (L, MOE_EXPERTS, D_MODEL, D_EXPERT), D_MODEL ** -0.5)
    moe_w_up = nrm(ks[24], (L, MOE_EXPERTS, D_MODEL, D_EXPERT), D_MODEL ** -0.5)
    moe_w_down = nrm(ks[25], (L, MOE_EXPERTS, D_EXPERT, D_MODEL), D_EXPERT ** -0.5)
    final_norm_g = 1.0 + nrm(ks[26], (D_MODEL,), 0.02)
    return {'x': x, 'rel_bias': rel_bias, 'norm1_g': norm1_g, 'w_in': w_in,
            'ssm_a_re': ssm_a_re, 'ssm_a_im': ssm_a_im, 'ssm_log_dt': ssm_log_dt,
            'ssm_b_re': ssm_b_re, 'ssm_b_im': ssm_b_im, 'ssm_c_re': ssm_c_re, 'ssm_c_im': ssm_c_im,
            'ssm_d': ssm_d, 'ssm_glu_w': ssm_glu_w, 'ssm_glu_b': ssm_glu_b,
            'pool_w': pool_w, 'pool_b': pool_b, 'pool_scale': pool_scale, 'w_out': w_out,
            'norm2_g': norm2_g, 'moe_group_w': moe_group_w, 'moe_group_b': moe_group_b,
            'moe_router_w': moe_router_w, 'moe_router_b': moe_router_b,
            'moe_w_gate': moe_w_gate, 'moe_w_up': moe_w_up, 'moe_w_down': moe_w_down,
            'final_norm_g': final_norm_g}


def reference(x, rel_bias, norm1_g, w_in, ssm_a_re, ssm_a_im, ssm_log_dt, ssm_b_re, ssm_b_im,
              ssm_c_re, ssm_c_im, ssm_d, ssm_glu_w, ssm_glu_b, pool_w, pool_b, pool_scale,
              w_out, norm2_g, moe_group_w, moe_group_b, moe_router_w, moe_router_b,
              moe_w_gate, moe_w_up, moe_w_down, final_norm_g):
    bsz, seq, _ = x.shape
    s0 = SSM_WIDTH
    s1 = s0 + POOL_WIDTH
    s2 = s1 + ATT_WIDTH
    s3 = s2 + ATT_WIDTH
    for l in range(DEPTH):
        h = rms_norm(x, norm1_g[l])
        proj = h @ w_in[l]
        y_ssm = ssm_mixer(proj[..., :s0], ssm_a_re[l], ssm_a_im[l], ssm_log_dt[l],
                          ssm_b_re[l], ssm_b_im[l], ssm_c_re[l], ssm_c_im[l],
                          ssm_d[l], ssm_glu_w[l], ssm_glu_b[l])
        y_pool = pool_mixer(proj[..., s0:s1], pool_w[l], pool_b[l], pool_scale[l])
        att_shape = (bsz, seq, ATT_HEADS, ATT_HEAD_DIM)
        y_att = moba_attention(proj[..., s1:s2].reshape(att_shape),
                               proj[..., s2:s3].reshape(att_shape),
                               proj[..., s3:].reshape(att_shape), rel_bias)
        x = x + jnp.concatenate([y_ssm, y_pool, y_att], axis=-1) @ w_out[l]
        h = rms_norm(x, norm2_g[l])
        x = x + hier_moe(h, moe_group_w[l], moe_group_b[l], moe_router_w[l], moe_router_b[l],
                         moe_w_gate[l], moe_w_up[l], moe_w_down[l])
    return rms_norm(x, final_norm_g)
```

```python
import math
from functools import partial

import jax
import jax.numpy as jnp
from jax import lax
from jax.experimental import pallas as pl
from jax.experimental.pallas import tpu as pltpu

F32 = jnp.float32
BF16 = jnp.bfloat16

D_MODEL = 1024
SSM_WIDTH = 256
POOL_WIDTH = 256
ATT_WIDTH = 512
SSM_GROUP = 16
SSM_GROUPS = SSM_WIDTH // SSM_GROUP
SSM_STATE = 64
SSM_STATES = SSM_GROUPS * SSM_STATE
POOL_WINDOWS = (2, 4, 8, 16)
POOL_GROUP = POOL_WIDTH // len(POOL_WINDOWS)
POOL_HALO = 16
HEAD_DIM = 64
HEADS = ATT_WIDTH // HEAD_DIM
HEAD_PAIR = 2 * HEAD_DIM
MOBA_BLOCK = 256
MOBA_TOPK = 3
REL_BUCKETS = 32
REL_MAX_EXACT = REL_BUCKETS // 2
REL_MAX_DIST = 128
MOE_GROUPS = 4
MOE_PER_GROUP = 4
MOE_EXPERTS = MOE_GROUPS * MOE_PER_GROUP
D_EXPERT = D_MODEL // 4
ROUTER_LANES = 128
RMS_EPS = 1e-6
NEG_INF = -1e30

PROJ_ROWS = 512
MOE_ROWS = 1024
POOL_ROWS = 512
SSM_STEPS = 128
VMEM_LIMIT = 48 * 1024 * 1024

_NT = (((1,), (1,)), ((), ()))

_BUCKET_START = list(range(REL_MAX_EXACT)) + [
    math.ceil(REL_MAX_EXACT * (REL_MAX_DIST / REL_MAX_EXACT) ** (k / (REL_BUCKETS - REL_MAX_EXACT)))
    for k in range(REL_BUCKETS - REL_MAX_EXACT)
]


def _rms_norm(x, g):
    return x * lax.rsqrt(jnp.mean(x * x, axis=-1, keepdims=True) + RMS_EPS) * g


def _inproj_kernel(x_ref, g_ref, w_ref, wvt_ref, u_ref, p_ref, q_ref, k_ref, vt_ref):
    h = _rms_norm(x_ref[...], g_ref[...]).astype(BF16)
    pr = jnp.dot(h, w_ref[...], preferred_element_type=F32)
    s0, s1, s2 = SSM_WIDTH, SSM_WIDTH + POOL_WIDTH, SSM_WIDTH + POOL_WIDTH + ATT_WIDTH
    u_ref[...] = pr[:, :s0]
    p_ref[...] = pr[:, s0:s1]
    q_ref[...] = (pr[:, s1:s2] * HEAD_DIM ** -0.5).astype(BF16)
    k_ref[...] = pr[:, s2:].astype(BF16)
    vt_ref[0] = lax.dot_general(wvt_ref[...], h, _NT, preferred_element_type=F32).astype(BF16)


def _in_projection(x2d, g, w_main, w_vt, bsz, seq):
    rows = x2d.shape[0]
    tiles_per_seq = seq // PROJ_ROWS
    n_main = w_main.shape[1]
    const = lambda i: (0, 0)
    row_blk = lambda width: pl.BlockSpec((PROJ_ROWS, width), lambda i: (i, 0))
    return pl.pallas_call(
        _inproj_kernel,
        grid=(rows // PROJ_ROWS,),
        in_specs=[row_blk(D_MODEL), pl.BlockSpec((1, D_MODEL), const),
                  pl.BlockSpec((D_MODEL, n_main), const), pl.BlockSpec((ATT_WIDTH, D_MODEL), const)],
        out_specs=[row_blk(SSM_WIDTH), row_blk(POOL_WIDTH), row_blk(ATT_WIDTH), row_blk(ATT_WIDTH),
                   pl.BlockSpec((1, ATT_WIDTH, PROJ_ROWS),
                                lambda i: (i // tiles_per_seq, 0, i % tiles_per_seq))],
        out_shape=[jax.ShapeDtypeStruct((rows, SSM_WIDTH), F32),
                   jax.ShapeDtypeStruct((rows, POOL_WIDTH), F32),
                   jax.ShapeDtypeStruct((rows, ATT_WIDTH), BF16),
                   jax.ShapeDtypeStruct((rows, ATT_WIDTH), BF16),
                   jax.ShapeDtypeStruct((bsz, ATT_WIDTH, seq), BF16)],
        compiler_params=pltpu.CompilerParams(dimension_semantics=("parallel",),
                                             vmem_limit_bytes=VMEM_LIMIT),
        name="in_projection",
    )(x2d, g.reshape(1, D_MODEL), w_main, w_vt)


def _ssm_kernel(u_ref, bm_ref, cm_ref, ar_ref, ai_ref, d_ref, gw_ref, gb_ref, o_ref, bu_ref, st_ref):
    bsz = st_ref.shape[0]
    ns = SSM_STATES

    @pl.when(pl.program_id(0) == 0)
    def _init():
        st_ref[...] = jnp.zeros_like(st_ref)

    u = u_ref[...]
    bu_ref[...] = jnp.dot(u.astype(BF16), bm_ref[...], preferred_element_type=F32)
    ar = jnp.broadcast_to(ar_ref[...], (bsz, ns))
    ai = jnp.broadcast_to(ai_ref[...], (bsz, ns))

    def step(t, carry):
        sr, si = carry
        r0 = pl.multiple_of(t * bsz, bsz)
        nr = ar * sr - ai * si + bu_ref[pl.ds(r0, bsz), 0:ns]
        ni = ar * si + ai * sr + bu_ref[pl.ds(r0, bsz), ns:2 * ns]
        bu_ref[pl.ds(r0, bsz), 0:ns] = nr
        bu_ref[pl.ds(r0, bsz), ns:2 * ns] = ni
        return nr, ni

    sr, si = lax.fori_loop(0, SSM_STEPS, step, (st_ref[:, 0:ns], st_ref[:, ns:2 * ns]), unroll=4)
    st_ref[:, 0:ns] = sr
    st_ref[:, ns:2 * ns] = si

    y = jnp.dot(bu_ref[...].astype(BF16), cm_ref[...], preferred_element_type=F32)
    y = y + d_ref[...] * u
    y = y * (0.5 * (1.0 + jnp.tanh(math.sqrt(2.0 / math.pi) * (y + 0.044715 * (y * y * y)))))
    z = jnp.dot(y.astype(BF16), gw_ref[...], preferred_element_type=F32) + gb_ref[...]
    o_ref[...] = (y * (1.0 / (1.0 + jnp.exp(-z)))).astype(o_ref.dtype)


def _ssm_params(a_re, a_im, log_dt, b_re, b_im, c_re, c_im):
    lam = lax.complex(a_re.astype(F32), a_im.astype(F32))
    dt = jnp.exp(log_dt.astype(F32))[:, None]
    lam_bar = jnp.exp(lam * dt)
    b = lax.complex(b_re.astype(F32), b_im.astype(F32))
    b_bar = ((lam_bar - 1.0) / lam)[..., None] * b
    eye = jnp.eye(SSM_GROUPS, dtype=F32)

    def bdiag_in(m):
        return jnp.einsum('gph,gk->ghkp', m, eye).reshape(SSM_WIDTH, SSM_STATES)

    def bdiag_out(m):
        return jnp.einsum('ghp,gk->gpkh', m, eye).reshape(SSM_STATES, SSM_WIDTH)

    bm = jnp.concatenate([bdiag_in(jnp.real(b_bar)), bdiag_in(jnp.imag(b_bar))], axis=1)
    cm = jnp.concatenate([bdiag_out(c_re.astype(F32)), -bdiag_out(c_im.astype(F32))], axis=0)
    return (bm.astype(BF16), cm.astype(BF16),
            jnp.real(lam_bar).reshape(1, SSM_STATES), jnp.imag(lam_bar).reshape(1, SSM_STATES))


def _ssm_mixer(u_tb, params, d, glu_w, glu_b, bsz, seq):
    bm, cm, ar, ai = params
    rows = SSM_STEPS * bsz
    const = lambda c: (0, 0)
    return pl.pallas_call(
        _ssm_kernel,
        grid=(seq // SSM_STEPS,),
        in_specs=[
            pl.BlockSpec((rows, SSM_WIDTH), lambda c: (c, 0)),
            pl.BlockSpec((SSM_WIDTH, 2 * SSM_STATES), const),
            pl.BlockSpec((2 * SSM_STATES, SSM_WIDTH), const),
            pl.BlockSpec((1, SSM_STATES), const),
            pl.BlockSpec((1, SSM_STATES), const),
            pl.BlockSpec((1, SSM_WIDTH), const),
            pl.BlockSpec((SSM_WIDTH, SSM_WIDTH), const),
            pl.BlockSpec((1, SSM_WIDTH), const),
        ],
        out_specs=pl.BlockSpec((rows, SSM_WIDTH), lambda c: (c, 0)),
        out_shape=jax.ShapeDtypeStruct((seq * bsz, SSM_WIDTH), BF16),
        scratch_shapes=[pltpu.VMEM((rows, 2 * SSM_STATES), F32), pltpu.VMEM((bsz, 2 * SSM_STATES), F32)],
        compiler_params=pltpu.CompilerParams(dimension_semantics=("arbitrary",),
                                             vmem_limit_bytes=VMEM_LIMIT),
        name="ssm_mixer",
    )(u_tb, bm, cm, ar, ai, d.reshape(1, SSM_WIDTH), glu_w.astype(BF16), glu_b.reshape(1, SSM_WIDTH))


def _pool_kernel(halo_ref, p_ref, w_ref, b_ref, s_ref, o_ref):
    i = pl.program_id(1)
    halo = jnp.where(i == 0, 0.0, halo_ref[0])
    p = p_ref[0]
    ext = jnp.concatenate([halo, p], axis=0)
    s2 = ext + pltpu.roll(ext, 1, 0)
    s4 = s2 + pltpu.roll(s2, 2, 0)
    s8 = s4 + pltpu.roll(s4, 4, 0)
    s16 = s8 + pltpu.roll(s8, 8, 0)
    sums = (s2, s4, s8, s16)
    lane = lax.broadcasted_iota(jnp.int32, p.shape, 1)
    t1 = (lax.broadcasted_iota(jnp.int32, p.shape, 0) + i * POOL_ROWS + 1).astype(F32)
    mean = None
    for gi, win in enumerate(POOL_WINDOWS):
        m = sums[gi][POOL_HALO:] / jnp.minimum(t1, float(win))
        mean = m if mean is None else jnp.where(lane >= gi * POOL_GROUP, m, mean)
    dlt = (mean - p).astype(BF16)
    y = jnp.dot(dlt, w_ref[...], preferred_element_type=F32)
    o_ref[0] = ((y + b_ref[...]) * s_ref[...]).astype(o_ref.dtype)


def _pool_mixer(p, w, b, scale):
    bsz, seq, _ = p.shape
    eye = jnp.eye(len(POOL_WINDOWS), dtype=F32)
    w_bd = jnp.einsum('gcd,gk->gckd', w.astype(F32), eye).reshape(POOL_WIDTH, POOL_WIDTH).astype(BF16)
    halo_blocks = POOL_ROWS // POOL_HALO
    const = lambda bi, i: (0, 0)
    return pl.pallas_call(
        _pool_kernel,
        grid=(bsz, seq // POOL_ROWS),
        in_specs=[
            pl.BlockSpec((1, POOL_HALO, POOL_WIDTH), lambda bi, i: (bi, jnp.maximum(i * halo_blocks - 1, 0), 0)),
            pl.BlockSpec((1, POOL_ROWS, POOL_WIDTH), lambda bi, i: (bi, i, 0)),
            pl.BlockSpec((POOL_WIDTH, POOL_WIDTH), const),
            pl.BlockSpec((1, POOL_WIDTH), const),
            pl.BlockSpec((1, POOL_WIDTH), const),
        ],
        out_specs=pl.BlockSpec((1, POOL_ROWS, POOL_WIDTH), lambda bi, i: (bi, i, 0)),
        out_shape=jax.ShapeDtypeStruct((bsz, seq, POOL_WIDTH), BF16),
        compiler_params=pltpu.CompilerParams(dimension_semantics=("parallel", "parallel")),
        name="pool_mixer",
    )(p, p, w_bd, b.reshape(1, POOL_WIDTH), scale.reshape(1, POOL_WIDTH))


def _bias_table_kernel(bias_ref, tab_ref):
    h = pl.program_id(0)
    kk = lax.broadcasted_iota(jnp.int32, (MOBA_BLOCK, MOBA_BLOCK), 0)
    qq = lax.broadcasted_iota(jnp.int32, (MOBA_BLOCK, MOBA_BLOCK), 1)
    for which in range(2):
        rel = qq - kk + which * MOBA_BLOCK
        val = jnp.full((MOBA_BLOCK, MOBA_BLOCK), bias_ref[0, h], F32)
        for b in range(1, REL_BUCKETS):
            val = jnp.where(rel >= _BUCKET_START[b], bias_ref[b, h], val)
        tab_ref[0, which] = jnp.where(rel >= 0, val, NEG_INF)


def _bias_tables(rel_bias):
    return pl.pallas_call(
        _bias_table_kernel,
        grid=(HEADS,),
        in_specs=[pl.BlockSpec(memory_space=pltpu.SMEM)],
        out_specs=pl.BlockSpec((1, 2, MOBA_BLOCK, MOBA_BLOCK), lambda h: (h, 0, 0, 0)),
        out_shape=jax.ShapeDtypeStruct((HEADS, 2, MOBA_BLOCK, MOBA_BLOCK), F32),
        name="bias_tables",
    )(rel_bias.astype(F32))


def _attn_kernel(bias_ref, q_ref, k_ref, vt_ref, tab_ref, o_ref, km_ref, va_ref, neg_ref, acc_ref, m_ref):
    seq = k_ref.shape[1]
    nb = seq // MOBA_BLOCK
    pr = pl.program_id(1)
    qi = pl.program_id(2)
    lane = lax.broadcasted_iota(jnp.int32, (1, HEAD_PAIR), 1)
    head_lanes = [lane < HEAD_DIM, lane >= HEAD_DIM]

    @pl.when(qi == 0)
    def _prep():
        kf = k_ref[0].astype(F32).reshape(nb, MOBA_BLOCK, HEAD_PAIR)
        km = jnp.sum(kf, axis=1) * (1.0 / MOBA_BLOCK)
        km_hi = km.astype(BF16).astype(F32)
        km_lo = km - km_hi
        for hh in range(2):
            km_ref[hh, 0:nb, :] = jnp.where(head_lanes[hh], km_hi, 0.0).astype(BF16)
            km_ref[hh, nb:2 * nb, :] = jnp.where(head_lanes[hh], km_lo, 0.0).astype(BF16)
        ones = jnp.ones((HEAD_DIM, seq), BF16)
        va_ref[0, 0:HEAD_DIM, :] = vt_ref[0, 0:HEAD_DIM, :]
        va_ref[0, HEAD_DIM:, :] = ones
        va_ref[1, 0:HEAD_DIM, :] = ones
        va_ref[1, HEAD_DIM:, :] = vt_ref[0, HEAD_DIM:, :]

    q = q_ref[0].astype(F32)
    qh = [jnp.where(head_lanes[hh], q, 0.0).astype(BF16) for hh in range(2)]
    blk_id = lax.broadcasted_iota(jnp.int32, (nb, MOBA_BLOCK), 0)

    for hh in range(2):
        g2 = lax.dot_general(km_ref[hh], qh[hh], _NT, preferred_element_type=F32)
        g = jnp.where(blk_id < qi, g2[0:nb] + g2[nb:2 * nb], NEG_INF)
        cnt = jnp.zeros((nb, MOBA_BLOCK), jnp.int32)
        for m in range(nb):
            gm = g[m:m + 1, :]
            cnt = cnt + ((gm > g) | ((gm == g) & (blk_id > m))).astype(jnp.int32)
        neg_ref[hh] = jnp.where((cnt < MOBA_TOPK) & (blk_id < qi), 0.0, NEG_INF)

    def scores(hh, j):
        kj = k_ref[0, pl.ds(pl.multiple_of(j * MOBA_BLOCK, MOBA_BLOCK), MOBA_BLOCK), :]
        return lax.dot_general(kj, qh[hh], _NT, preferred_element_type=F32)

    def pv(hh, j, p):
        vj = va_ref[hh, :, pl.ds(pl.multiple_of(j * MOBA_BLOCK, MOBA_BLOCK), MOBA_BLOCK)]
        return jnp.dot(vj, p.astype(BF16), preferred_element_type=F32)

    def update(hh, j, s, m):
        m_new = jnp.maximum(m, jnp.max(s, axis=0, keepdims=True))
        acc_ref[hh] = acc_ref[hh] * jnp.exp(m - m_new) + pv(hh, j, jnp.exp(s - m_new))
        return m_new

    for hh in range(2):
        s = scores(hh, qi) + tab_ref[hh, 0]
        m = jnp.max(s, axis=0, keepdims=True)
        acc_ref[hh] = pv(hh, qi, jnp.exp(s - m))
        m_ref[hh] = m

    @pl.when(qi >= 1)
    def _prev():
        j = qi - 1
        for hh in range(2):
            s = scores(hh, j) + (tab_ref[hh, 1] + neg_ref[hh, pl.ds(j, 1), :])
            m_ref[hh] = update(hh, j, s, m_ref[hh])

    def far_body(j, carry):
        out = []
        for hh in range(2):
            far_bias = bias_ref[REL_BUCKETS - 1, 2 * pr + hh]
            s = scores(hh, j) + (neg_ref[hh, pl.ds(j, 1), :] + far_bias)
            out.append(update(hh, j, s, carry[hh]))
        return tuple(out)

    lax.fori_loop(0, jnp.maximum(qi - 1, 0), far_body, (m_ref[0], m_ref[1]))

    oa = acc_ref[0]
    ob = acc_ref[1]
    out_t = jnp.concatenate([oa[0:HEAD_DIM] / oa[HEAD_DIM:HEAD_DIM + 1],
                             ob[HEAD_DIM:] / ob[0:1]], axis=0)
    o_ref[0] = out_t.T.astype(o_ref.dtype)


def _moba_attention(q, k, vt, tabs, rel_bias):
    bsz, seq, _ = q.shape
    assert seq % MOBA_BLOCK == 0
    nb = seq // MOBA_BLOCK
    return pl.pallas_call(
        _attn_kernel,
        grid=(bsz, HEADS // 2, nb),
        in_specs=[
            pl.BlockSpec(memory_space=pltpu.SMEM),
            pl.BlockSpec((1, MOBA_BLOCK, HEAD_PAIR), lambda b, p, i: (b, i, p)),
            pl.BlockSpec((1, seq, HEAD_PAIR), lambda b, p, i: (b, 0, p)),
            pl.BlockSpec((1, HEAD_PAIR, seq), lambda b, p, i: (b, p, 0)),
            pl.BlockSpec((2, 2, MOBA_BLOCK, MOBA_BLOCK), lambda b, p, i: (p, 0, 0, 0)),
        ],
        out_specs=pl.BlockSpec((1, MOBA_BLOCK, HEAD_PAIR), lambda b, p, i: (b, i, p)),
        out_shape=jax.ShapeDtypeStruct((bsz, seq, ATT_WIDTH), BF16),
        scratch_shapes=[
            pltpu.VMEM((2, 2 * nb, HEAD_PAIR), BF16),
            pltpu.VMEM((2, HEAD_PAIR, seq), BF16),
            pltpu.VMEM((2, nb, MOBA_BLOCK), F32),
            pltpu.VMEM((2, HEAD_PAIR, MOBA_BLOCK), F32),
            pltpu.VMEM((2, 1, MOBA_BLOCK), F32),
        ],
        compiler_params=pltpu.CompilerParams(
            dimension_semantics=("parallel", "parallel", "arbitrary"), vmem_limit_bytes=VMEM_LIMIT),
        name="moba_attention",
    )(rel_bias.astype(F32), q, k, vt, tabs)


def _outproj_kernel(x_ref, ys_ref, yp_ref, ya_ref, w_ref, o_ref):
    s0, s1 = SSM_WIDTH, SSM_WIDTH + POOL_WIDTH
    acc = jnp.dot(ys_ref[...], w_ref[0:s0, :], preferred_element_type=F32)
    acc = acc + jnp.dot(yp_ref[...], w_ref[s0:s1, :], preferred_element_type=F32)
    acc = acc + jnp.dot(ya_ref[...], w_ref[s1:, :], preferred_element_type=F32)
    o_ref[...] = x_ref[...] + acc


def _out_projection(x2d, y_ssm, y_pool, y_att, w_out):
    rows = x2d.shape[0]
    row_blk = lambda width: pl.BlockSpec((PROJ_ROWS, width), lambda i: (i, 0))
    return pl.pallas_call(
        _outproj_kernel,
        grid=(rows // PROJ_ROWS,),
        in_specs=[row_blk(D_MODEL), row_blk(SSM_WIDTH), row_blk(POOL_WIDTH), row_blk(ATT_WIDTH),
                  pl.BlockSpec((D_MODEL, D_MODEL), lambda i: (0, 0))],
        out_specs=row_blk(D_MODEL),
        out_shape=jax.ShapeDtypeStruct((rows, D_MODEL), F32),
        compiler_params=pltpu.CompilerParams(dimension_semantics=("parallel",),
                                             vmem_limit_bytes=VMEM_LIMIT),
        name="out_projection",
    )(x2d, y_ssm, y_pool, y_att, w_out)


def _routing_weights(logits):
    col = lambda i: logits[:, i:i + 1]
    gl = [col(g) for g in range(MOE_GROUPS)]
    gmax = _max_of(gl)
    sel, taken = [], None
    for g in range(MOE_GROUPS):
        hit = gl[g] == gmax
        if taken is not None:
            hit = hit & jnp.logical_not(taken)
        taken = hit if taken is None else (taken | hit)
        sel.append(hit)
    denom = None
    for g in range(MOE_GROUPS):
        e = jnp.exp(gl[g] - gmax)
        denom = e if denom is None else denom + e
    g_wt = 1.0 / denom
    el = []
    for e in range(MOE_PER_GROUP):
        v = col(MOE_GROUPS + (MOE_GROUPS - 1) * MOE_PER_GROUP + e)
        for g in range(MOE_GROUPS - 2, -1, -1):
            v = jnp.where(sel[g], col(MOE_GROUPS + g * MOE_PER_GROUP + e), v)
        el.append(v)
    rank = []
    for e in range(MOE_PER_GROUP):
        r = jnp.zeros_like(el[e], dtype=jnp.int32)
        for m in range(MOE_PER_GROUP):
            if m == e:
                continue
            beats = (el[m] >= el[e]) if m < e else (el[m] > el[e])
            r = r + beats.astype(jnp.int32)
        rank.append(r)
    v1 = _max_of(el)
    v2 = None
    for e in range(MOE_PER_GROUP):
        c = jnp.where(rank[e] == 1, el[e], 0.0)
        v2 = c if v2 is None else v2 + c
    e2 = jnp.exp(v2 - v1)
    w1 = 1.0 / (1.0 + e2)
    w2 = e2 / (1.0 + e2)
    lane = lax.broadcasted_iota(jnp.int32, logits.shape, 1)
    comb = jnp.zeros(logits.shape, F32)
    for g in range(MOE_GROUPS):
        for e in range(MOE_PER_GROUP):
            within = jnp.where(rank[e] == 0, w1, jnp.where(rank[e] == 1, w2, 0.0))
            c = jnp.where(sel[g], g_wt * within, 0.0)
            comb = jnp.where(lane == g * MOE_PER_GROUP + e, c, comb)
    return comb


def _max_of(cols):
    out = cols[0]
    for c in cols[1:]:
        out = jnp.maximum(out, c)
    return out


def _moe_kernel(x_ref, g_ref, wr_ref, br_ref, wgu_ref, wd_ref, fg_ref, o_ref, h_ref, comb_ref, *, final_norm):
    e = pl.program_id(1)

    @pl.when(e == 0)
    def _route():
        x = x_ref[...]
        h = _rms_norm(x, g_ref[...])
        h_ref[...] = h.astype(BF16)
        logits = jnp.dot(h, wr_ref[...], preferred_element_type=F32, precision=lax.Precision.HIGHEST)
        comb_ref[...] = _routing_weights(logits + br_ref[...])
        o_ref[...] = x

    h = h_ref[...]
    gu = jnp.dot(h, wgu_ref[0], preferred_element_type=F32)
    hg, hu = gu[:, :D_EXPERT], gu[:, D_EXPERT:]
    lane = lax.broadcasted_iota(jnp.int32, comb_ref.shape, 1)
    c = jnp.sum(jnp.where(lane == e, comb_ref[...], 0.0), axis=1, keepdims=True)
    act = (hg * (1.0 / (1.0 + jnp.exp(-hg)))) * hu * c
    o_ref[...] += jnp.dot(act.astype(BF16), wd_ref[0], preferred_element_type=F32)

    if final_norm:
        @pl.when(e == MOE_EXPERTS - 1)
        def _final():
            o_ref[...] = _rms_norm(o_ref[...], fg_ref[...])


def _hier_moe(x2d, norm_g, w_router, b_router, w_gu, w_down, final_g, final_norm):
    rows = x2d.shape[0]
    const = lambda i, e: (0, 0)
    return pl.pallas_call(
        partial(_moe_kernel, final_norm=final_norm),
        grid=(rows // MOE_ROWS, MOE_EXPERTS),
        in_specs=[
            pl.BlockSpec((MOE_ROWS, D_MODEL), lambda i, e: (i, 0)),
            pl.BlockSpec((1, D_MODEL), const),
            pl.BlockSpec((D_MODEL, ROUTER_LANES), const),
            pl.BlockSpec((1, ROUTER_LANES), const),
            pl.BlockSpec((1, D_MODEL, 2 * D_EXPERT), lambda i, e: (e, 0, 0)),
            pl.BlockSpec((1, D_EXPERT, D_MODEL), lambda i, e: (e, 0, 0)),
            pl.BlockSpec((1, D_MODEL), const),
        ],
        out_specs=pl.BlockSpec((MOE_ROWS, D_MODEL), lambda i, e: (i, 0)),
        out_shape=jax.ShapeDtypeStruct((rows, D_MODEL), F32),
        scratch_shapes=[pltpu.VMEM((MOE_ROWS, D_MODEL), BF16), pltpu.VMEM((MOE_ROWS, ROUTER_LANES), F32)],
        compiler_params=pltpu.CompilerParams(dimension_semantics=("parallel", "arbitrary"),
                                             vmem_limit_bytes=VMEM_LIMIT),
        name="hier_moe",
    )(x2d, norm_g.reshape(1, D_MODEL), w_router, b_router, w_gu, w_down, final_g.reshape(1, D_MODEL))


def _router_params(group_w, group_b, router_w, router_b):
    w = jnp.concatenate([group_w.astype(F32)] + [router_w[g].astype(F32) for g in range(MOE_GROUPS)], axis=1)
    b = jnp.concatenate([group_b.astype(F32), router_b.astype(F32).reshape(-1)])
    pad = ROUTER_LANES - w.shape[1]
    return jnp.pad(w, ((0, 0), (0, pad))), jnp.pad(b, (0, pad)).reshape(1, ROUTER_LANES)


def kernel(x, rel_bias, norm1_g, w_in, ssm_a_re, ssm_a_im, ssm_log_dt, ssm_b_re, ssm_b_im, ssm_c_re, ssm_c_im,
           ssm_d, ssm_glu_w, ssm_glu_b, pool_w, pool_b, pool_scale, w_out, norm2_g, moe_group_w, moe_group_b,
           moe_router_w, moe_router_b, moe_w_gate, moe_w_up, moe_w_down, final_norm_g):
    bsz, seq, dm = x.shape
    depth = w_in.shape[0]
    n_main = SSM_WIDTH + POOL_WIDTH + 2 * ATT_WIDTH
    x2d = x.astype(F32).reshape(bsz * seq, dm)
    tabs = _bias_tables(rel_bias)
    for l in range(depth):
        w_main = w_in[l, :, :n_main].astype(BF16)
        w_vt = w_in[l, :, n_main:].T.astype(BF16)
        u, p, q, k, vt = _in_projection(x2d, norm1_g[l], w_main, w_vt, bsz, seq)

        u_tb = u.reshape(bsz, seq, SSM_WIDTH).transpose(1, 0, 2).reshape(seq * bsz, SSM_WIDTH)
        params = _ssm_params(ssm_a_re[l], ssm_a_im[l], ssm_log_dt[l], ssm_b_re[l], ssm_b_im[l],
                             ssm_c_re[l], ssm_c_im[l])
        y_ssm = _ssm_mixer(u_tb, params, ssm_d[l], ssm_glu_w[l], ssm_glu_b[l], bsz, seq)
        y_ssm = y_ssm.reshape(seq, bsz, SSM_WIDTH).transpose(1, 0, 2).reshape(bsz * seq, SSM_WIDTH)

        y_pool = _pool_mixer(p.reshape(bsz, seq, POOL_WIDTH), pool_w[l], pool_b[l], pool_scale[l])
        y_att = _moba_attention(q.reshape(bsz, seq, ATT_WIDTH), k.reshape(bsz, seq, ATT_WIDTH), vt, tabs, rel_bias)

        x2d = _out_projection(x2d, y_ssm, y_pool.reshape(bsz * seq, POOL_WIDTH),
                              y_att.reshape(bsz * seq, ATT_WIDTH), w_out[l].astype(BF16))

        w_router, b_router = _router_params(moe_group_w[l], moe_group_b[l], moe_router_w[l], moe_router_b[l])
        w_gu = jnp.concatenate([moe_w_gate[l], moe_w_up[l]], axis=-1).astype(BF16)
        x2d = _hier_moe(x2d, norm2_g[l], w_router, b_router, w_gu, moe_w_down[l].astype(BF16),
                        final_norm_g, final_norm=(l == depth - 1))
    return x2d.reshape(bsz, seq, dm).astype(x.dtype)
```

```python
import math
from functools import partial

import jax
import jax.numpy as jnp
from jax import lax
from jax.experimental import pallas as pl
from jax.experimental.pallas import tpu as pltpu

F32 = jnp.float32
BF16 = jnp.bfloat16

D_MODEL = 1024
SSM_WIDTH = 256
POOL_WIDTH = 256
ATT_WIDTH = 512
SSM_GROUP = 16
SSM_GROUPS = SSM_WIDTH // SSM_GROUP
SSM_STATE = 64
SSM_STATES = SSM_GROUPS * SSM_STATE
POOL_WINDOWS = (2, 4, 8, 16)
POOL_GROUP = POOL_WIDTH // len(POOL_WINDOWS)
POOL_HALO = 16
HEAD_DIM = 64
HEADS = ATT_WIDTH // HEAD_DIM
HEAD_PAIR = 2 * HEAD_DIM
MOBA_BLOCK = 256
MOBA_TOPK = 3
FAR_GROUP = 4
REL_BUCKETS = 32
REL_MAX_EXACT = REL_BUCKETS // 2
REL_MAX_DIST = 128
MOE_GROUPS = 4
MOE_PER_GROUP = 4
MOE_EXPERTS = MOE_GROUPS * MOE_PER_GROUP
D_EXPERT = D_MODEL // 4
ROUTER_LANES = 128
RMS_EPS = 1e-6
NEG_INF = -1e30

PROJ_ROWS = 512
MOE_ROWS = 1024
POOL_ROWS = 512
SSM_STEPS = 128
VMEM_LIMIT = 48 * 1024 * 1024

_NT = (((1,), (1,)), ((), ()))

_BUCKET_START = list(range(REL_MAX_EXACT)) + [
    math.ceil(REL_MAX_EXACT * (REL_MAX_DIST / REL_MAX_EXACT) ** (k / (REL_BUCKETS - REL_MAX_EXACT)))
    for k in range(REL_BUCKETS - REL_MAX_EXACT)
]


def _rms_norm(x, g):
    return x * lax.rsqrt(jnp.mean(x * x, axis=-1, keepdims=True) + RMS_EPS) * g


def _inproj_kernel(x_ref, g_ref, w_ref, wvt_ref, u_ref, p_ref, q_ref, k_ref, vt_ref):
    h = _rms_norm(x_ref[...], g_ref[...]).astype(BF16)
    pr = jnp.dot(h, w_ref[...], preferred_element_type=F32)
    s0, s1, s2 = SSM_WIDTH, SSM_WIDTH + POOL_WIDTH, SSM_WIDTH + POOL_WIDTH + ATT_WIDTH
    u_ref[...] = pr[:, :s0]
    p_ref[...] = pr[:, s0:s1]
    q_ref[...] = (pr[:, s1:s2] * HEAD_DIM ** -0.5).astype(BF16)
    k_ref[...] = pr[:, s2:].astype(BF16)
    vt_ref[0] = lax.dot_general(wvt_ref[...], h, _NT, preferred_element_type=F32).astype(BF16)


def _in_projection(x2d, g, w_main, w_vt, bsz, seq):
    rows = x2d.shape[0]
    tiles_per_seq = seq // PROJ_ROWS
    n_main = w_main.shape[1]
    const = lambda i: (0, 0)
    row_blk = lambda width: pl.BlockSpec((PROJ_ROWS, width), lambda i: (i, 0))
    return pl.pallas_call(
        _inproj_kernel,
        grid=(rows // PROJ_ROWS,),
        in_specs=[row_blk(D_MODEL), pl.BlockSpec((1, D_MODEL), const),
                  pl.BlockSpec((D_MODEL, n_main), const), pl.BlockSpec((ATT_WIDTH, D_MODEL), const)],
        out_specs=[row_blk(SSM_WIDTH), row_blk(POOL_WIDTH), row_blk(ATT_WIDTH), row_blk(ATT_WIDTH),
                   pl.BlockSpec((1, ATT_WIDTH, PROJ_ROWS),
                                lambda i: (i // tiles_per_seq, 0, i % tiles_per_seq))],
        out_shape=[jax.ShapeDtypeStruct((rows, SSM_WIDTH), F32),
                   jax.ShapeDtypeStruct((rows, POOL_WIDTH), F32),
                   jax.ShapeDtypeStruct((rows, ATT_WIDTH), BF16),
                   jax.ShapeDtypeStruct((rows, ATT_WIDTH), BF16),
                   jax.ShapeDtypeStruct((bsz, ATT_WIDTH, seq), BF16)],
        compiler_params=pltpu.CompilerParams(dimension_semantics=("parallel",),
                                             vmem_limit_bytes=VMEM_LIMIT),
        name="in_projection",
    )(x2d, g.reshape(1, D_MODEL), w_main, w_vt)


def _ssm_kernel(u_ref, bm_ref, cm_ref, ar_ref, ai_ref, d_ref, gw_ref, gb_ref, o_ref, bu_ref, st_ref):
    bsz = st_ref.shape[0]
    ns = SSM_STATES

    @pl.when(pl.program_id(0) == 0)
    def _init():
        st_ref[...] = jnp.zeros_like(st_ref)

    u = u_ref[...]
    bu_ref[...] = jnp.dot(u.astype(BF16), bm_ref[...], preferred_element_type=F32)
    ar = jnp.broadcast_to(ar_ref[...], (bsz, ns))
    ai = jnp.broadcast_to(ai_ref[...], (bsz, ns))

    def step(t, carry):
        sr, si = carry
        r0 = pl.multiple_of(t * bsz, bsz)
        nr = ar * sr - ai * si + bu_ref[pl.ds(r0, bsz), 0:ns]
        ni = ar * si + ai * sr + bu_ref[pl.ds(r0, bsz), ns:2 * ns]
        bu_ref[pl.ds(r0, bsz), 0:ns] = nr
        bu_ref[pl.ds(r0, bsz), ns:2 * ns] = ni
        return nr, ni

    sr, si = lax.fori_loop(0, SSM_STEPS, step, (st_ref[:, 0:ns], st_ref[:, ns:2 * ns]), unroll=4)
    st_ref[:, 0:ns] = sr
    st_ref[:, ns:2 * ns] = si

    y = jnp.dot(bu_ref[...].astype(BF16), cm_ref[...], preferred_element_type=F32)
    y = y + d_ref[...] * u
    y = y * (0.5 * (1.0 + jnp.tanh(math.sqrt(2.0 / math.pi) * (y + 0.044715 * (y * y * y)))))
    z = jnp.dot(y.astype(BF16), gw_ref[...], preferred_element_type=F32) + gb_ref[...]
    o_ref[...] = (y * (1.0 / (1.0 + jnp.exp(-z)))).astype(o_ref.dtype)


def _ssm_params(a_re, a_im, log_dt, b_re, b_im, c_re, c_im):
    lam = lax.complex(a_re.astype(F32), a_im.astype(F32))
    dt = jnp.exp(log_dt.astype(F32))[:, None]
    lam_bar = jnp.exp(lam * dt)
    b = lax.complex(b_re.astype(F32), b_im.astype(F32))
    b_bar = ((lam_bar - 1.0) / lam)[..., None] * b
    eye = jnp.eye(SSM_GROUPS, dtype=F32)

    def bdiag_in(m):
        return jnp.einsum('gph,gk->ghkp', m, eye).reshape(SSM_WIDTH, SSM_STATES)

    def bdiag_out(m):
        return jnp.einsum('ghp,gk->gpkh', m, eye).reshape(SSM_STATES, SSM_WIDTH)

    bm = jnp.concatenate([bdiag_in(jnp.real(b_bar)), bdiag_in(jnp.imag(b_bar))], axis=1)
    cm = jnp.concatenate([bdiag_out(c_re.astype(F32)), -bdiag_out(c_im.astype(F32))], axis=0)
    return (bm.astype(BF16), cm.astype(BF16),
            jnp.real(lam_bar).reshape(1, SSM_STATES), jnp.imag(lam_bar).reshape(1, SSM_STATES))


def _ssm_mixer(u_tb, params, d, glu_w, glu_b, bsz, seq):
    bm, cm, ar, ai = params
    rows = SSM_STEPS * bsz
    const = lambda c: (0, 0)
    return pl.pallas_call(
        _ssm_kernel,
        grid=(seq // SSM_STEPS,),
        in_specs=[
            pl.BlockSpec((rows, SSM_WIDTH), lambda c: (c, 0)),
            pl.BlockSpec((SSM_WIDTH, 2 * SSM_STATES), const),
            pl.BlockSpec((2 * SSM_STATES, SSM_WIDTH), const),
            pl.BlockSpec((1, SSM_STATES), const),
            pl.BlockSpec((1, SSM_STATES), const),
            pl.BlockSpec((1, SSM_WIDTH), const),
            pl.BlockSpec((SSM_WIDTH, SSM_WIDTH), const),
            pl.BlockSpec((1, SSM_WIDTH), const),
        ],
        out_specs=pl.BlockSpec((rows, SSM_WIDTH), lambda c: (c, 0)),
        out_shape=jax.ShapeDtypeStruct((seq * bsz, SSM_WIDTH), BF16),
        scratch_shapes=[pltpu.VMEM((rows, 2 * SSM_STATES), F32), pltpu.VMEM((bsz, 2 * SSM_STATES), F32)],
        compiler_params=pltpu.CompilerParams(dimension_semantics=("arbitrary",),
                                             vmem_limit_bytes=VMEM_LIMIT),
        name="ssm_mixer",
    )(u_tb, bm, cm, ar, ai, d.reshape(1, SSM_WIDTH), glu_w.astype(BF16), glu_b.reshape(1, SSM_WIDTH))


def _pool_kernel(halo_ref, p_ref, w_ref, b_ref, s_ref, o_ref):
    i = pl.program_id(1)
    halo = jnp.where(i == 0, 0.0, halo_ref[0])
    p = p_ref[0]
    ext = jnp.concatenate([halo, p], axis=0)
    s2 = ext + pltpu.roll(ext, 1, 0)
    s4 = s2 + pltpu.roll(s2, 2, 0)
    s8 = s4 + pltpu.roll(s4, 4, 0)
    s16 = s8 + pltpu.roll(s8, 8, 0)
    sums = (s2, s4, s8, s16)
    lane = lax.broadcasted_iota(jnp.int32, p.shape, 1)
    t1 = (lax.broadcasted_iota(jnp.int32, p.shape, 0) + i * POOL_ROWS + 1).astype(F32)
    mean = None
    for gi, win in enumerate(POOL_WINDOWS):
        m = sums[gi][POOL_HALO:] / jnp.minimum(t1, float(win))
        mean = m if mean is None else jnp.where(lane >= gi * POOL_GROUP, m, mean)
    dlt = (mean - p).astype(BF16)
    y = jnp.dot(dlt, w_ref[...], preferred_element_type=F32)
    o_ref[0] = ((y + b_ref[...]) * s_ref[...]).astype(o_ref.dtype)


def _pool_mixer(p, w, b, scale):
    bsz, seq, _ = p.shape
    eye = jnp.eye(len(POOL_WINDOWS), dtype=F32)
    w_bd = jnp.einsum('gcd,gk->gckd', w.astype(F32), eye).reshape(POOL_WIDTH, POOL_WIDTH).astype(BF16)
    halo_blocks = POOL_ROWS // POOL_HALO
    const = lambda bi, i: (0, 0)
    return pl.pallas_call(
        _pool_kernel,
        grid=(bsz, seq // POOL_ROWS),
        in_specs=[
            pl.BlockSpec((1, POOL_HALO, POOL_WIDTH), lambda bi, i: (bi, jnp.maximum(i * halo_blocks - 1, 0), 0)),
            pl.BlockSpec((1, POOL_ROWS, POOL_WIDTH), lambda bi, i: (bi, i, 0)),
            pl.BlockSpec((POOL_WIDTH, POOL_WIDTH), const),
            pl.BlockSpec((1, POOL_WIDTH), const),
            pl.BlockSpec((1, POOL_WIDTH), const),
        ],
        out_specs=pl.BlockSpec((1, POOL_ROWS, POOL_WIDTH), lambda bi, i: (bi, i, 0)),
        out_shape=jax.ShapeDtypeStruct((bsz, seq, POOL_WIDTH), BF16),
        compiler_params=pltpu.CompilerParams(dimension_semantics=("parallel", "parallel")),
        name="pool_mixer",
    )(p, p, w_bd, b.reshape(1, POOL_WIDTH), scale.reshape(1, POOL_WIDTH))


def _bias_table_kernel(bias_ref, tab_ref):
    h = pl.program_id(0)
    kk = lax.broadcasted_iota(jnp.int32, (MOBA_BLOCK, MOBA_BLOCK), 0)
    qq = lax.broadcasted_iota(jnp.int32, (MOBA_BLOCK, MOBA_BLOCK), 1)
    for which in range(2):
        rel = qq - kk + which * MOBA_BLOCK
        val = jnp.full((MOBA_BLOCK, MOBA_BLOCK), bias_ref[0, h], F32)
        for b in range(1, REL_BUCKETS):
            val = jnp.where(rel >= _BUCKET_START[b], bias_ref[b, h], val)
        tab_ref[0, which] = jnp.where(rel >= 0, val, NEG_INF)


def _bias_tables(rel_bias):
    return pl.pallas_call(
        _bias_table_kernel,
        grid=(HEADS,),
        in_specs=[pl.BlockSpec(memory_space=pltpu.SMEM)],
        out_specs=pl.BlockSpec((1, 2, MOBA_BLOCK, MOBA_BLOCK), lambda h: (h, 0, 0, 0)),
        out_shape=jax.ShapeDtypeStruct((HEADS, 2, MOBA_BLOCK, MOBA_BLOCK), F32),
        name="bias_tables",
    )(rel_bias.astype(F32))


def _attn_kernel(bias_ref, q_ref, k_ref, vt_ref, tab_ref, o_ref, km_ref, va_ref, neg_ref, acc_ref, m_ref, s_ref):
    seq = k_ref.shape[1]
    nb = seq // MOBA_BLOCK
    pr = pl.program_id(1)
    qi = pl.program_id(2)
    lane = lax.broadcasted_iota(jnp.int32, (1, HEAD_PAIR), 1)
    head_lanes = [lane < HEAD_DIM, lane >= HEAD_DIM]

    @pl.when(qi == 0)
    def _prep():
        kf = k_ref[0].astype(F32).reshape(nb, MOBA_BLOCK, HEAD_PAIR)
        km = jnp.sum(kf, axis=1) * (1.0 / MOBA_BLOCK)
        km_hi = km.astype(BF16).astype(F32)
        km_lo = km - km_hi
        for hh in range(2):
            km_ref[hh, 0:nb, :] = jnp.where(head_lanes[hh], km_hi, 0.0).astype(BF16)
            km_ref[hh, nb:2 * nb, :] = jnp.where(head_lanes[hh], km_lo, 0.0).astype(BF16)
        ones = jnp.ones((HEAD_DIM, seq), BF16)
        va_ref[0, 0:HEAD_DIM, :] = vt_ref[0, 0:HEAD_DIM, :]
        va_ref[0, HEAD_DIM:, :] = ones
        va_ref[1, 0:HEAD_DIM, :] = ones
        va_ref[1, HEAD_DIM:, :] = vt_ref[0, HEAD_DIM:, :]

    q = q_ref[0].astype(F32)
    qh = [jnp.where(head_lanes[hh], q, 0.0).astype(BF16) for hh in range(2)]
    blk_id = lax.broadcasted_iota(jnp.int32, (nb, MOBA_BLOCK), 0)

    for hh in range(2):
        g2 = lax.dot_general(km_ref[hh], qh[hh], _NT, preferred_element_type=F32)
        g = jnp.where(blk_id < qi, g2[0:nb] + g2[nb:2 * nb], NEG_INF)
        cnt = jnp.zeros((nb, MOBA_BLOCK), jnp.int32)
        for m in range(nb):
            gm = g[m:m + 1, :]
            cnt = cnt + ((gm > g) | ((gm == g) & (blk_id > m))).astype(jnp.int32)
        neg_ref[hh] = jnp.where((cnt < MOBA_TOPK) & (blk_id < qi), 0.0, NEG_INF)

    def key_rows(j0, nblk):
        return pl.ds(pl.multiple_of(j0 * MOBA_BLOCK, MOBA_BLOCK), nblk * MOBA_BLOCK)

    def attend(j0, nblk, adds, m_prev):
        units = [(hh, b) for b in range(nblk) for hh in range(2)]
        scores = {}
        for hh, b in units:
            scores[hh, b] = lax.dot_general(k_ref[0, key_rows(j0 + b, 1), :], qh[hh], _NT,
                                            preferred_element_type=F32)
        mbs, pvs = {}, {}
        for i, (hh, b) in enumerate(units):
            s, add = scores[hh, b], adds[hh][b]
            if add.shape[0] == 1:
                smax = jnp.max(s, axis=0, keepdims=True)
                mbs[hh, b] = smax + add
                p = jnp.exp(s - smax)
            else:
                s_ref[i] = s
                s = s_ref[i] + add
                mbs[hh, b] = jnp.max(s, axis=0, keepdims=True)
                p = jnp.exp(s - mbs[hh, b])
            pvs[hh, b] = jnp.dot(va_ref[hh, :, key_rows(j0 + b, 1)], p.astype(BF16),
                                 preferred_element_type=F32)
        out = []
        for hh in range(2):
            m_new = None if m_prev is None else m_prev[hh]
            for b in range(nblk):
                m_new = mbs[hh, b] if m_new is None else jnp.maximum(m_new, mbs[hh, b])
            acc = None if m_prev is None else acc_ref[hh] * jnp.exp(m_prev[hh] - m_new)
            for b in range(nblk):
                term = pvs[hh, b] * jnp.exp(mbs[hh, b] - m_new)
                acc = term if acc is None else acc + term
            acc_ref[hh] = acc
            out.append(m_new)
        return tuple(out)

    def store_max(ms):
        m_ref[0] = ms[0]
        m_ref[1] = ms[1]

    @pl.when(qi == 0)
    def _first_block():
        store_max(attend(0, 1, [[tab_ref[hh, 0]] for hh in range(2)], None))

    @pl.when(qi > 0)
    def _near_blocks():
        adds = [[tab_ref[hh, 1] + neg_ref[hh, pl.ds(qi - 1, 1), :], tab_ref[hh, 0]] for hh in range(2)]
        store_max(attend(qi - 1, 2, adds, None))

    def far_blocks(j0, nblk, ms):
        adds = [[neg_ref[hh, pl.ds(j0 + b, 1), :] + bias_ref[REL_BUCKETS - 1, 2 * pr + hh]
                 for b in range(nblk)] for hh in range(2)]
        return attend(j0, nblk, adds, ms)

    def far_tail(nblk, j0):
        store_max(far_blocks(j0, nblk, (m_ref[0], m_ref[1])))

    n_far = jnp.maximum(qi - 1, 0)
    ms = lax.fori_loop(0, n_far // FAR_GROUP, lambda i, ms: far_blocks(i * FAR_GROUP, FAR_GROUP, ms),
                       (m_ref[0], m_ref[1]))
    store_max(ms)
    nblk, j0 = FAR_GROUP // 2, (n_far // FAR_GROUP) * FAR_GROUP
    while nblk >= 1:
        pl.when((n_far & nblk) != 0)(partial(far_tail, nblk, j0))
        j0 = j0 + (n_far & nblk)
        nblk //= 2

    oa = acc_ref[0]
    ob = acc_ref[1]
    out_t = jnp.concatenate([oa[0:HEAD_DIM] / oa[HEAD_DIM:HEAD_DIM + 1],
                             ob[HEAD_DIM:] / ob[0:1]], axis=0)
    o_ref[0] = out_t.T.astype(o_ref.dtype)


def _moba_attention(q, k, vt, tabs, rel_bias):
    bsz, seq, _ = q.shape
    assert seq % MOBA_BLOCK == 0
    nb = seq // MOBA_BLOCK
    return pl.pallas_call(
        _attn_kernel,
        grid=(bsz, HEADS // 2, nb),
        in_specs=[
            pl.BlockSpec(memory_space=pltpu.SMEM),
            pl.BlockSpec((1, MOBA_BLOCK, HEAD_PAIR), lambda b, p, i: (b, i, p)),
            pl.BlockSpec((1, seq, HEAD_PAIR), lambda b, p, i: (b, 0, p)),
            pl.BlockSpec((1, HEAD_PAIR, seq), lambda b, p, i: (b, p, 0)),
            pl.BlockSpec((2, 2, MOBA_BLOCK, MOBA_BLOCK), lambda b, p, i: (p, 0, 0, 0)),
        ],
        out_specs=pl.BlockSpec((1, MOBA_BLOCK, HEAD_PAIR), lambda b, p, i: (b, i, p)),
        out_shape=jax.ShapeDtypeStruct((bsz, seq, ATT_WIDTH), BF16),
        scratch_shapes=[
            pltpu.VMEM((2, 2 * nb, HEAD_PAIR), BF16),
            pltpu.VMEM((2, HEAD_PAIR, seq), BF16),
            pltpu.VMEM((2, nb, MOBA_BLOCK), F32),
            pltpu.VMEM((2, HEAD_PAIR, MOBA_BLOCK), F32),
            pltpu.VMEM((2, 1, MOBA_BLOCK), F32),
            pltpu.VMEM((4, MOBA_BLOCK, MOBA_BLOCK), F32),
        ],
        compiler_params=pltpu.CompilerParams(
            dimension_semantics=("parallel", "parallel", "arbitrary"), vmem_limit_bytes=VMEM_LIMIT),
        name="moba_attention",
    )(rel_bias.astype(F32), q, k, vt, tabs)


def _outproj_kernel(x_ref, ys_ref, yp_ref, ya_ref, w_ref, o_ref):
    s0, s1 = SSM_WIDTH, SSM_WIDTH + POOL_WIDTH
    acc = jnp.dot(ys_ref[...], w_ref[0:s0, :], preferred_element_type=F32)
    acc = acc + jnp.dot(yp_ref[...], w_ref[s0:s1, :], preferred_element_type=F32)
    acc = acc + jnp.dot(ya_ref[...], w_ref[s1:, :], preferred_element_type=F32)
    o_ref[...] = x_ref[...] + acc


def _out_projection(x2d, y_ssm, y_pool, y_att, w_out):
    rows = x2d.shape[0]
    row_blk = lambda width: pl.BlockSpec((PROJ_ROWS, width), lambda i: (i, 0))
    return pl.pallas_call(
        _outproj_kernel,
        grid=(rows // PROJ_ROWS,),
        in_specs=[row_blk(D_MODEL), row_blk(SSM_WIDTH), row_blk(POOL_WIDTH), row_blk(ATT_WIDTH),
                  pl.BlockSpec((D_MODEL, D_MODEL), lambda i: (0, 0))],
        out_specs=row_blk(D_MODEL),
        out_shape=jax.ShapeDtypeStruct((rows, D_MODEL), F32),
        compiler_params=pltpu.CompilerParams(dimension_semantics=("parallel",),
                                             vmem_limit_bytes=VMEM_LIMIT),
        name="out_projection",
    )(x2d, y_ssm, y_pool, y_att, w_out)


def _routing_weights(logits):
    col = lambda i: logits[:, i:i + 1]
    gl = [col(g) for g in range(MOE_GROUPS)]
    gmax = _max_of(gl)
    sel, taken = [], None
    for g in range(MOE_GROUPS):
        hit = gl[g] == gmax
        if taken is not None:
            hit = hit & jnp.logical_not(taken)
        taken = hit if taken is None else (taken | hit)
        sel.append(hit)
    denom = None
    for g in range(MOE_GROUPS):
        e = jnp.exp(gl[g] - gmax)
        denom = e if denom is None else denom + e
    g_wt = 1.0 / denom
    el = []
    for e in range(MOE_PER_GROUP):
        v = col(MOE_GROUPS + (MOE_GROUPS - 1) * MOE_PER_GROUP + e)
        for g in range(MOE_GROUPS - 2, -1, -1):
            v = jnp.where(sel[g], col(MOE_GROUPS + g * MOE_PER_GROUP + e), v)
        el.append(v)
    rank = []
    for e in range(MOE_PER_GROUP):
        r = jnp.zeros_like(el[e], dtype=jnp.int32)
        for m in range(MOE_PER_GROUP):
            if m == e:
                continue
            beats = (el[m] >= el[e]) if m < e else (el[m] > el[e])
            r = r + beats.astype(jnp.int32)
        rank.append(r)
    v1 = _max_of(el)
    v2 = None
    for e in range(MOE_PER_GROUP):
        c = jnp.where(rank[e] == 1, el[e], 0.0)
        v2 = c if v2 is None else v2 + c
    e2 = jnp.exp(v2 - v1)
    w1 = 1.0 / (1.0 + e2)
    w2 = e2 / (1.0 + e2)
    lane = lax.broadcasted_iota(jnp.int32, logits.shape, 1)
    comb = jnp.zeros(logits.shape, F32)
    for g in range(MOE_GROUPS):
        for e in range(MOE_PER_GROUP):
            within = jnp.where(rank[e] == 0, w1, jnp.where(rank[e] == 1, w2, 0.0))
            c = jnp.where(sel[g], g_wt * within, 0.0)
            comb = jnp.where(lane == g * MOE_PER_GROUP + e, c, comb)
    return comb


def _max_of(cols):
    out = cols[0]
    for c in cols[1:]:
        out = jnp.maximum(out, c)
    return out


def _moe_kernel(x_ref, g_ref, wr_ref, br_ref, wgu_ref, wd_ref, fg_ref, o_ref, h_ref, comb_ref, *, final_norm):
    e = pl.program_id(1)

    @pl.when(e == 0)
    def _route():
        x = x_ref[...]
        h = _rms_norm(x, g_ref[...])
        h_ref[...] = h.astype(BF16)
        logits = jnp.dot(h, wr_ref[...], preferred_element_type=F32, precision=lax.Precision.HIGHEST)
        comb_ref[...] = _routing_weights(logits + br_ref[...])
        o_ref[...] = x

    h = h_ref[...]
    gu = jnp.dot(h, wgu_ref[0], preferred_element_type=F32)
    hg, hu = gu[:, :D_EXPERT], gu[:, D_EXPERT:]
    lane = lax.broadcasted_iota(jnp.int32, comb_ref.shape, 1)
    c = jnp.sum(jnp.where(lane == e, comb_ref[...], 0.0), axis=1, keepdims=True)
    act = (hg * (1.0 / (1.0 + jnp.exp(-hg)))) * hu * c
    o_ref[...] += jnp.dot(act.astype(BF16), wd_ref[0], preferred_element_type=F32)

    if final_norm:
        @pl.when(e == MOE_EXPERTS - 1)
        def _final():
            o_ref[...] = _rms_norm(o_ref[...], fg_ref[...])


def _hier_moe(x2d, norm_g, w_router, b_router, w_gu, w_down, final_g, final_norm):
    rows = x2d.shape[0]
    const = lambda i, e: (0, 0)
    return pl.pallas_call(
        partial(_moe_kernel, final_norm=final_norm),
        grid=(rows // MOE_ROWS, MOE_EXPERTS),
        in_specs=[
            pl.BlockSpec((MOE_ROWS, D_MODEL), lambda i, e: (i, 0)),
            pl.BlockSpec((1, D_MODEL), const),
            pl.BlockSpec((D_MODEL, ROUTER_LANES), const),
            pl.BlockSpec((1, ROUTER_LANES), const),
            pl.BlockSpec((1, D_MODEL, 2 * D_EXPERT), lambda i, e: (e, 0, 0)),
            pl.BlockSpec((1, D_EXPERT, D_MODEL), lambda i, e: (e, 0, 0)),
            pl.BlockSpec((1, D_MODEL), const),
        ],
        out_specs=pl.BlockSpec((MOE_ROWS, D_MODEL), lambda i, e: (i, 0)),
        out_shape=jax.ShapeDtypeStruct((rows, D_MODEL), F32),
        scratch_shapes=[pltpu.VMEM((MOE_ROWS, D_MODEL), BF16), pltpu.VMEM((MOE_ROWS, ROUTER_LANES), F32)],
        compiler_params=pltpu.CompilerParams(dimension_semantics=("parallel", "arbitrary"),
                                             vmem_limit_bytes=VMEM_LIMIT),
        name="hier_moe",
    )(x2d, norm_g.reshape(1, D_MODEL), w_router, b_router, w_gu, w_down, final_g.reshape(1, D_MODEL))


def _router_params(group_w, group_b, router_w, router_b):
    w = jnp.concatenate([group_w.astype(F32)] + [router_w[g].astype(F32) for g in range(MOE_GROUPS)], axis=1)
    b = jnp.concatenate([group_b.astype(F32), router_b.astype(F32).reshape(-1)])
    pad = ROUTER_LANES - w.shape[1]
    return jnp.pad(w, ((0, 0), (0, pad))), jnp.pad(b, (0, pad)).reshape(1, ROUTER_LANES)


def kernel(x, rel_bias, norm1_g, w_in, ssm_a_re, ssm_a_im, ssm_log_dt, ssm_b_re, ssm_b_im, ssm_c_re, ssm_c_im,
           ssm_d, ssm_glu_w, ssm_glu_b, pool_w, pool_b, pool_scale, w_out, norm2_g, moe_group_w, moe_group_b,
           moe_router_w, moe_router_b, moe_w_gate, moe_w_up, moe_w_down, final_norm_g):
    bsz, seq, dm = x.shape
    depth = w_in.shape[0]
    n_main = SSM_WIDTH + POOL_WIDTH + 2 * ATT_WIDTH
    x2d = x.astype(F32).reshape(bsz * seq, dm)
    tabs = _bias_tables(rel_bias)
    for l in range(depth):
        w_main = w_in[l, :, :n_main].astype(BF16)
        w_vt = w_in[l, :, n_main:].T.astype(BF16)
        u, p, q, k, vt = _in_projection(x2d, norm1_g[l], w_main, w_vt, bsz, seq)

        u_tb = u.reshape(bsz, seq, SSM_WIDTH).transpose(1, 0, 2).reshape(seq * bsz, SSM_WIDTH)
        params = _ssm_params(ssm_a_re[l], ssm_a_im[l], ssm_log_dt[l], ssm_b_re[l], ssm_b_im[l],
                             ssm_c_re[l], ssm_c_im[l])
        y_ssm = _ssm_mixer(u_tb, params, ssm_d[l], ssm_glu_w[l], ssm_glu_b[l], bsz, seq)
        y_ssm = y_ssm.reshape(seq, bsz, SSM_WIDTH).transpose(1, 0, 2).reshape(bsz * seq, SSM_WIDTH)

        y_pool = _pool_mixer(p.reshape(bsz, seq, POOL_WIDTH), pool_w[l], pool_b[l], pool_scale[l])
        y_att = _moba_attention(q.reshape(bsz, seq, ATT_WIDTH), k.reshape(bsz, seq, ATT_WIDTH), vt, tabs, rel_bias)

        x2d = _out_projection(x2d, y_ssm, y_pool.reshape(bsz * seq, POOL_WIDTH),
                              y_att.reshape(bsz * seq, ATT_WIDTH), w_out[l].astype(BF16))

        w_router, b_router = _router_params(moe_group_w[l], moe_group_b[l], moe_router_w[l], moe_router_b[l])
        w_gu = jnp.concatenate([moe_w_gate[l], moe_w_up[l]], axis=-1).astype(BF16)
        x2d = _hier_moe(x2d, norm2_g[l], w_router, b_router, w_gu, moe_w_down[l].astype(BF16),
                        final_norm_g, final_norm=(l == depth - 1))
    return x2d.reshape(bsz, seq, dm).astype(x.dtype)
```

```python
import math
from functools import partial

import jax
import jax.numpy as jnp
from jax import lax
from jax.experimental import pallas as pl
from jax.experimental.pallas import tpu as pltpu

F32 = jnp.float32
BF16 = jnp.bfloat16

D_MODEL = 1024
SSM_WIDTH = 256
POOL_WIDTH = 256
ATT_WIDTH = 512
SSM_GROUP = 16
SSM_GROUPS = SSM_WIDTH // SSM_GROUP
SSM_STATE = 64
SSM_STATES = SSM_GROUPS * SSM_STATE
POOL_WINDOWS = (2, 4, 8, 16)
POOL_GROUP = POOL_WIDTH // len(POOL_WINDOWS)
POOL_HALO = 16
HEAD_DIM = 64
HEADS = ATT_WIDTH // HEAD_DIM
HEAD_PAIR = 2 * HEAD_DIM
MOBA_BLOCK = 256
MOBA_TOPK = 3
FAR_GROUP = 4
REL_BUCKETS = 32
REL_MAX_EXACT = REL_BUCKETS // 2
REL_MAX_DIST = 128
MOE_GROUPS = 4
MOE_PER_GROUP = 4
MOE_EXPERTS = MOE_GROUPS * MOE_PER_GROUP
D_EXPERT = D_MODEL // 4
ROUTER_ROWS = 32
ROUTER_LANES = 128
RMS_EPS = 1e-6
NEG_INF = -1e30

PROJ_ROWS = 512
MOE_ROWS = 1024
POOL_ROWS = 512
SSM_STEPS = 128
VMEM_LIMIT = 48 * 1024 * 1024

_NT = (((1,), (1,)), ((), ()))

_BUCKET_START = list(range(REL_MAX_EXACT)) + [
    math.ceil(REL_MAX_EXACT * (REL_MAX_DIST / REL_MAX_EXACT) ** (k / (REL_BUCKETS - REL_MAX_EXACT)))
    for k in range(REL_BUCKETS - REL_MAX_EXACT)
]


def _rms_norm(x, g):
    return x * lax.rsqrt(jnp.mean(x * x, axis=-1, keepdims=True) + RMS_EPS) * g


def _inproj_kernel(x_ref, g_ref, w_ref, wvt_ref, u_ref, p_ref, q_ref, k_ref, vt_ref):
    h = _rms_norm(x_ref[...], g_ref[...]).astype(BF16)
    pr = jnp.dot(h, w_ref[...], preferred_element_type=F32)
    s0, s1, s2 = SSM_WIDTH, SSM_WIDTH + POOL_WIDTH, SSM_WIDTH + POOL_WIDTH + ATT_WIDTH
    u_ref[...] = pr[:, :s0]
    p_ref[...] = pr[:, s0:s1]
    q_ref[...] = (pr[:, s1:s2] * HEAD_DIM ** -0.5).astype(BF16)
    k_ref[...] = pr[:, s2:].astype(BF16)
    vt_ref[0] = lax.dot_general(wvt_ref[...], h, _NT, preferred_element_type=F32).astype(BF16)


def _in_projection(x2d, g, w_main, w_vt, bsz, seq):
    rows = x2d.shape[0]
    tiles_per_seq = seq // PROJ_ROWS
    n_main = w_main.shape[1]
    const = lambda i: (0, 0)
    row_blk = lambda width: pl.BlockSpec((PROJ_ROWS, width), lambda i: (i, 0))
    return pl.pallas_call(
        _inproj_kernel,
        grid=(rows // PROJ_ROWS,),
        in_specs=[row_blk(D_MODEL), pl.BlockSpec((1, D_MODEL), const),
                  pl.BlockSpec((D_MODEL, n_main), const), pl.BlockSpec((ATT_WIDTH, D_MODEL), const)],
        out_specs=[row_blk(SSM_WIDTH), row_blk(POOL_WIDTH), row_blk(ATT_WIDTH), row_blk(ATT_WIDTH),
                   pl.BlockSpec((1, ATT_WIDTH, PROJ_ROWS),
                                lambda i: (i // tiles_per_seq, 0, i % tiles_per_seq))],
        out_shape=[jax.ShapeDtypeStruct((rows, SSM_WIDTH), F32),
                   jax.ShapeDtypeStruct((rows, POOL_WIDTH), F32),
                   jax.ShapeDtypeStruct((rows, ATT_WIDTH), BF16),
                   jax.ShapeDtypeStruct((rows, ATT_WIDTH), BF16),
                   jax.ShapeDtypeStruct((bsz, ATT_WIDTH, seq), BF16)],
        compiler_params=pltpu.CompilerParams(dimension_semantics=("parallel",),
                                             vmem_limit_bytes=VMEM_LIMIT),
        name="in_projection",
    )(x2d, g.reshape(1, D_MODEL), w_main, w_vt)


def _ssm_kernel(u_ref, bm_ref, cm_ref, ar_ref, ai_ref, d_ref, gw_ref, gb_ref, o_ref, bu_ref, st_ref):
    bsz = st_ref.shape[0]
    ns = SSM_STATES

    @pl.when(pl.program_id(0) == 0)
    def _init():
        st_ref[...] = jnp.zeros_like(st_ref)

    u = u_ref[...]
    bu_ref[...] = jnp.dot(u.astype(BF16), bm_ref[...], preferred_element_type=F32)
    ar = jnp.broadcast_to(ar_ref[...], (bsz, ns))
    ai = jnp.broadcast_to(ai_ref[...], (bsz, ns))

    def step(t, carry):
        sr, si = carry
        r0 = pl.multiple_of(t * bsz, bsz)
        nr = ar * sr - ai * si + bu_ref[pl.ds(r0, bsz), 0:ns]
        ni = ar * si + ai * sr + bu_ref[pl.ds(r0, bsz), ns:2 * ns]
        bu_ref[pl.ds(r0, bsz), 0:ns] = nr
        bu_ref[pl.ds(r0, bsz), ns:2 * ns] = ni
        return nr, ni

    sr, si = lax.fori_loop(0, SSM_STEPS, step, (st_ref[:, 0:ns], st_ref[:, ns:2 * ns]), unroll=4)
    st_ref[:, 0:ns] = sr
    st_ref[:, ns:2 * ns] = si

    y = jnp.dot(bu_ref[...].astype(BF16), cm_ref[...], preferred_element_type=F32)
    y = y + d_ref[...] * u
    y = y * (0.5 * (1.0 + jnp.tanh(math.sqrt(2.0 / math.pi) * (y + 0.044715 * (y * y * y)))))
    z = jnp.dot(y.astype(BF16), gw_ref[...], preferred_element_type=F32) + gb_ref[...]
    o_ref[...] = (y * (1.0 / (1.0 + jnp.exp(-z)))).astype(o_ref.dtype)


def _ssm_params(a_re, a_im, log_dt, b_re, b_im, c_re, c_im):
    lam = lax.complex(a_re.astype(F32), a_im.astype(F32))
    dt = jnp.exp(log_dt.astype(F32))[:, None]
    lam_bar = jnp.exp(lam * dt)
    b = lax.complex(b_re.astype(F32), b_im.astype(F32))
    b_bar = ((lam_bar - 1.0) / lam)[..., None] * b
    eye = jnp.eye(SSM_GROUPS, dtype=F32)

    def bdiag_in(m):
        return jnp.einsum('gph,gk->ghkp', m, eye).reshape(SSM_WIDTH, SSM_STATES)

    def bdiag_out(m):
        return jnp.einsum('ghp,gk->gpkh', m, eye).reshape(SSM_STATES, SSM_WIDTH)

    bm = jnp.concatenate([bdiag_in(jnp.real(b_bar)), bdiag_in(jnp.imag(b_bar))], axis=1)
    cm = jnp.concatenate([bdiag_out(c_re.astype(F32)), -bdiag_out(c_im.astype(F32))], axis=0)
    return (bm.astype(BF16), cm.astype(BF16),
            jnp.real(lam_bar).reshape(1, SSM_STATES), jnp.imag(lam_bar).reshape(1, SSM_STATES))


def _ssm_mixer(u_tb, params, d, glu_w, glu_b, bsz, seq):
    bm, cm, ar, ai = params
    rows = SSM_STEPS * bsz
    const = lambda c: (0, 0)
    return pl.pallas_call(
        _ssm_kernel,
        grid=(seq // SSM_STEPS,),
        in_specs=[
            pl.BlockSpec((rows, SSM_WIDTH), lambda c: (c, 0)),
            pl.BlockSpec((SSM_WIDTH, 2 * SSM_STATES), const),
            pl.BlockSpec((2 * SSM_STATES, SSM_WIDTH), const),
            pl.BlockSpec((1, SSM_STATES), const),
            pl.BlockSpec((1, SSM_STATES), const),
            pl.BlockSpec((1, SSM_WIDTH), const),
            pl.BlockSpec((SSM_WIDTH, SSM_WIDTH), const),
            pl.BlockSpec((1, SSM_WIDTH), const),
        ],
        out_specs=pl.BlockSpec((rows, SSM_WIDTH), lambda c: (c, 0)),
        out_shape=jax.ShapeDtypeStruct((seq * bsz, SSM_WIDTH), BF16),
        scratch_shapes=[pltpu.VMEM((rows, 2 * SSM_STATES), F32), pltpu.VMEM((bsz, 2 * SSM_STATES), F32)],
        compiler_params=pltpu.CompilerParams(dimension_semantics=("arbitrary",),
                                             vmem_limit_bytes=VMEM_LIMIT),
        name="ssm_mixer",
    )(u_tb, bm, cm, ar, ai, d.reshape(1, SSM_WIDTH), glu_w.astype(BF16), glu_b.reshape(1, SSM_WIDTH))


def _pool_kernel(halo_ref, p_ref, w_ref, b_ref, s_ref, o_ref):
    i = pl.program_id(1)
    halo = jnp.where(i == 0, 0.0, halo_ref[0])
    p = p_ref[0]
    ext = jnp.concatenate([halo, p], axis=0)
    s2 = ext + pltpu.roll(ext, 1, 0)
    s4 = s2 + pltpu.roll(s2, 2, 0)
    s8 = s4 + pltpu.roll(s4, 4, 0)
    s16 = s8 + pltpu.roll(s8, 8, 0)
    sums = (s2, s4, s8, s16)
    lane = lax.broadcasted_iota(jnp.int32, p.shape, 1)
    t1 = (lax.broadcasted_iota(jnp.int32, p.shape, 0) + i * POOL_ROWS + 1).astype(F32)
    mean = None
    for gi, win in enumerate(POOL_WINDOWS):
        m = sums[gi][POOL_HALO:] / jnp.minimum(t1, float(win))
        mean = m if mean is None else jnp.where(lane >= gi * POOL_GROUP, m, mean)
    dlt = (mean - p).astype(BF16)
    y = jnp.dot(dlt, w_ref[...], preferred_element_type=F32)
    o_ref[0] = ((y + b_ref[...]) * s_ref[...]).astype(o_ref.dtype)


def _pool_mixer(p, w, b, scale):
    bsz, seq, _ = p.shape
    eye = jnp.eye(len(POOL_WINDOWS), dtype=F32)
    w_bd = jnp.einsum('gcd,gk->gckd', w.astype(F32), eye).reshape(POOL_WIDTH, POOL_WIDTH).astype(BF16)
    halo_blocks = POOL_ROWS // POOL_HALO
    const = lambda bi, i: (0, 0)
    return pl.pallas_call(
        _pool_kernel,
        grid=(bsz, seq // POOL_ROWS),
        in_specs=[
            pl.BlockSpec((1, POOL_HALO, POOL_WIDTH), lambda bi, i: (bi, jnp.maximum(i * halo_blocks - 1, 0), 0)),
            pl.BlockSpec((1, POOL_ROWS, POOL_WIDTH), lambda bi, i: (bi, i, 0)),
            pl.BlockSpec((POOL_WIDTH, POOL_WIDTH), const),
            pl.BlockSpec((1, POOL_WIDTH), const),
            pl.BlockSpec((1, POOL_WIDTH), const),
        ],
        out_specs=pl.BlockSpec((1, POOL_ROWS, POOL_WIDTH), lambda bi, i: (bi, i, 0)),
        out_shape=jax.ShapeDtypeStruct((bsz, seq, POOL_WIDTH), BF16),
        compiler_params=pltpu.CompilerParams(dimension_semantics=("parallel", "parallel")),
        name="pool_mixer",
    )(p, p, w_bd, b.reshape(1, POOL_WIDTH), scale.reshape(1, POOL_WIDTH))


def _bias_table_kernel(bias_ref, tab_ref):
    h = pl.program_id(0)
    kk = lax.broadcasted_iota(jnp.int32, (MOBA_BLOCK, MOBA_BLOCK), 0)
    qq = lax.broadcasted_iota(jnp.int32, (MOBA_BLOCK, MOBA_BLOCK), 1)
    for which in range(2):
        rel = qq - kk + which * MOBA_BLOCK
        val = jnp.full((MOBA_BLOCK, MOBA_BLOCK), bias_ref[0, h], F32)
        for b in range(1, REL_BUCKETS):
            val = jnp.where(rel >= _BUCKET_START[b], bias_ref[b, h], val)
        tab_ref[0, which] = jnp.where(rel >= 0, val, NEG_INF)


def _bias_tables(rel_bias):
    return pl.pallas_call(
        _bias_table_kernel,
        grid=(HEADS,),
        in_specs=[pl.BlockSpec(memory_space=pltpu.SMEM)],
        out_specs=pl.BlockSpec((1, 2, MOBA_BLOCK, MOBA_BLOCK), lambda h: (h, 0, 0, 0)),
        out_shape=jax.ShapeDtypeStruct((HEADS, 2, MOBA_BLOCK, MOBA_BLOCK), F32),
        name="bias_tables",
    )(rel_bias.astype(F32))


def _attn_kernel(bias_ref, q_ref, k_ref, vt_ref, tab_ref, o_ref, km_ref, va_ref, neg_ref, acc_ref, m_ref, s_ref):
    seq = k_ref.shape[1]
    nb = seq // MOBA_BLOCK
    pr = pl.program_id(1)
    qi = pl.program_id(2)
    lane = lax.broadcasted_iota(jnp.int32, (1, HEAD_PAIR), 1)
    head_lanes = [lane < HEAD_DIM, lane >= HEAD_DIM]

    @pl.when(qi == 0)
    def _prep():
        kf = k_ref[0].astype(F32).reshape(nb, MOBA_BLOCK, HEAD_PAIR)
        km = jnp.sum(kf, axis=1) * (1.0 / MOBA_BLOCK)
        km_hi = km.astype(BF16).astype(F32)
        km_lo = km - km_hi
        for hh in range(2):
            km_ref[hh, 0:nb, :] = jnp.where(head_lanes[hh], km_hi, 0.0).astype(BF16)
            km_ref[hh, nb:2 * nb, :] = jnp.where(head_lanes[hh], km_lo, 0.0).astype(BF16)
        ones = jnp.ones((HEAD_DIM, seq), BF16)
        va_ref[0, 0:HEAD_DIM, :] = vt_ref[0, 0:HEAD_DIM, :]
        va_ref[0, HEAD_DIM:, :] = ones
        va_ref[1, 0:HEAD_DIM, :] = ones
        va_ref[1, HEAD_DIM:, :] = vt_ref[0, HEAD_DIM:, :]

    q = q_ref[0].astype(F32)
    qh = [jnp.where(head_lanes[hh], q, 0.0).astype(BF16) for hh in range(2)]
    blk_id = lax.broadcasted_iota(jnp.int32, (nb, MOBA_BLOCK), 0)

    for hh in range(2):
        g2 = lax.dot_general(km_ref[hh], qh[hh], _NT, preferred_element_type=F32)
        g = jnp.where(blk_id < qi, g2[0:nb] + g2[nb:2 * nb], NEG_INF)
        cnt = jnp.zeros((nb, MOBA_BLOCK), jnp.int32)
        for m in range(nb):
            gm = g[m:m + 1, :]
            cnt = cnt + ((gm > g) | ((gm == g) & (blk_id > m))).astype(jnp.int32)
        neg_ref[hh] = jnp.where((cnt < MOBA_TOPK) & (blk_id < qi), 0.0, NEG_INF)

    def key_rows(j):
        return pl.ds(pl.multiple_of(j * MOBA_BLOCK, MOBA_BLOCK), MOBA_BLOCK)

    def attend(items, m_prev):
        units = [(hh, i) for i in range(len(items)) for hh in range(2)]
        scores = {}
        for hh, i in units:
            scores[hh, i] = lax.dot_general(k_ref[0, key_rows(items[i][0]), :], qh[hh], _NT,
                                            preferred_element_type=F32)
        mbs, pvs, staged = {}, {}, 0
        for hh, i in units:
            s, add = scores[hh, i], items[i][1][hh]
            if add.shape[0] == 1:
                smax = jnp.max(s, axis=0, keepdims=True)
                mbs[hh, i] = smax + add
                p = jnp.exp(s - smax)
            else:
                s_ref[staged] = s
                s = s_ref[staged] + add
                staged += 1
                mbs[hh, i] = jnp.max(s, axis=0, keepdims=True)
                p = jnp.exp(s - mbs[hh, i])
            pvs[hh, i] = jnp.dot(va_ref[hh, :, key_rows(items[i][0])], p.astype(BF16),
                                 preferred_element_type=F32)
        out = []
        for hh in range(2):
            m_new = None if m_prev is None else m_prev[hh]
            for i in range(len(items)):
                m_new = mbs[hh, i] if m_new is None else jnp.maximum(m_new, mbs[hh, i])
            acc = None if m_prev is None else acc_ref[hh] * jnp.exp(m_prev[hh] - m_new)
            for i in range(len(items)):
                term = pvs[hh, i] * jnp.exp(mbs[hh, i] - m_new)
                acc = term if acc is None else acc + term
            acc_ref[hh] = acc
            out.append(m_new)
        return tuple(out)

    def store_max(ms):
        m_ref[0] = ms[0]
        m_ref[1] = ms[1]

    def own_item():
        return (qi, [tab_ref[hh, 0] for hh in range(2)])

    def prev_item():
        return (qi - 1, [tab_ref[hh, 1] + neg_ref[hh, pl.ds(qi - 1, 1), :] for hh in range(2)])

    def far_item(j):
        return (j, [neg_ref[hh, pl.ds(j, 1), :] + bias_ref[REL_BUCKETS - 1, 2 * pr + hh] for hh in range(2)])

    n_far = jnp.maximum(qi - 1, 0)
    n_rem = n_far % FAR_GROUP

    @pl.when(qi == 0)
    def _first_block():
        store_max(attend([own_item()], None))

    for r in range(FAR_GROUP):
        @pl.when((qi > 0) & (n_rem == r))
        def _near_blocks(r=r):
            store_max(attend([own_item(), prev_item()] + [far_item(j) for j in range(r)], None))

    def far_group(i, ms):
        j0 = n_rem + i * FAR_GROUP
        return attend([far_item(j0 + b) for b in range(FAR_GROUP)], ms)

    lax.fori_loop(0, n_far // FAR_GROUP, far_group, (m_ref[0], m_ref[1]))

    oa = acc_ref[0]
    ob = acc_ref[1]
    out_t = jnp.concatenate([oa[0:HEAD_DIM] / oa[HEAD_DIM:HEAD_DIM + 1],
                             ob[HEAD_DIM:] / ob[0:1]], axis=0)
    o_ref[0] = out_t.T.astype(o_ref.dtype)


def _moba_attention(q, k, vt, tabs, rel_bias):
    bsz, seq, _ = q.shape
    assert seq % MOBA_BLOCK == 0
    nb = seq // MOBA_BLOCK
    return pl.pallas_call(
        _attn_kernel,
        grid=(bsz, HEADS // 2, nb),
        in_specs=[
            pl.BlockSpec(memory_space=pltpu.SMEM),
            pl.BlockSpec((1, MOBA_BLOCK, HEAD_PAIR), lambda b, p, i: (b, i, p)),
            pl.BlockSpec((1, seq, HEAD_PAIR), lambda b, p, i: (b, 0, p)),
            pl.BlockSpec((1, HEAD_PAIR, seq), lambda b, p, i: (b, p, 0)),
            pl.BlockSpec((2, 2, MOBA_BLOCK, MOBA_BLOCK), lambda b, p, i: (p, 0, 0, 0)),
        ],
        out_specs=pl.BlockSpec((1, MOBA_BLOCK, HEAD_PAIR), lambda b, p, i: (b, i, p)),
        out_shape=jax.ShapeDtypeStruct((bsz, seq, ATT_WIDTH), BF16),
        scratch_shapes=[
            pltpu.VMEM((2, 2 * nb, HEAD_PAIR), BF16),
            pltpu.VMEM((2, HEAD_PAIR, seq), BF16),
            pltpu.VMEM((2, nb, MOBA_BLOCK), F32),
            pltpu.VMEM((2, HEAD_PAIR, MOBA_BLOCK), F32),
            pltpu.VMEM((2, 1, MOBA_BLOCK), F32),
            pltpu.VMEM((4, MOBA_BLOCK, MOBA_BLOCK), F32),
        ],
        compiler_params=pltpu.CompilerParams(
            dimension_semantics=("parallel", "parallel", "arbitrary"), vmem_limit_bytes=VMEM_LIMIT),
        name="moba_attention",
    )(rel_bias.astype(F32), q, k, vt, tabs)


def _outproj_kernel(x_ref, ys_ref, yp_ref, ya_ref, w_ref, o_ref):
    s0, s1 = SSM_WIDTH, SSM_WIDTH + POOL_WIDTH
    acc = jnp.dot(ys_ref[...], w_ref[0:s0, :], preferred_element_type=F32)
    acc = acc + jnp.dot(yp_ref[...], w_ref[s0:s1, :], preferred_element_type=F32)
    acc = acc + jnp.dot(ya_ref[...], w_ref[s1:, :], preferred_element_type=F32)
    o_ref[...] = x_ref[...] + acc


def _out_projection(x2d, y_ssm, y_pool, y_att, w_out):
    rows = x2d.shape[0]
    row_blk = lambda width: pl.BlockSpec((PROJ_ROWS, width), lambda i: (i, 0))
    return pl.pallas_call(
        _outproj_kernel,
        grid=(rows // PROJ_ROWS,),
        in_specs=[row_blk(D_MODEL), row_blk(SSM_WIDTH), row_blk(POOL_WIDTH), row_blk(ATT_WIDTH),
                  pl.BlockSpec((D_MODEL, D_MODEL), lambda i: (0, 0))],
        out_specs=row_blk(D_MODEL),
        out_shape=jax.ShapeDtypeStruct((rows, D_MODEL), F32),
        compiler_params=pltpu.CompilerParams(dimension_semantics=("parallel",),
                                             vmem_limit_bytes=VMEM_LIMIT),
        name="out_projection",
    )(x2d, y_ssm, y_pool, y_att, w_out)


def _routing_weights(logits_t):
    col = lambda i: logits_t[i:i + 1, :]
    gl = [col(g) for g in range(MOE_GROUPS)]
    gmax = _max_of(gl)
    sel, taken = [], None
    for g in range(MOE_GROUPS):
        hit = gl[g] == gmax
        if taken is not None:
            hit = hit & jnp.logical_not(taken)
        taken = hit if taken is None else (taken | hit)
        sel.append(hit)
    denom = None
    for g in range(MOE_GROUPS):
        e = jnp.exp(gl[g] - gmax)
        denom = e if denom is None else denom + e
    g_wt = 1.0 / denom
    el = []
    for e in range(MOE_PER_GROUP):
        v = col(MOE_GROUPS + (MOE_GROUPS - 1) * MOE_PER_GROUP + e)
        for g in range(MOE_GROUPS - 2, -1, -1):
            v = jnp.where(sel[g], col(MOE_GROUPS + g * MOE_PER_GROUP + e), v)
        el.append(v)
    rank = []
    for e in range(MOE_PER_GROUP):
        r = jnp.zeros_like(el[e], dtype=jnp.int32)
        for m in range(MOE_PER_GROUP):
            if m == e:
                continue
            beats = (el[m] >= el[e]) if m < e else (el[m] > el[e])
            r = r + beats.astype(jnp.int32)
        rank.append(r)
    v1 = _max_of(el)
    v2 = None
    for e in range(MOE_PER_GROUP):
        c = jnp.where(rank[e] == 1, el[e], 0.0)
        v2 = c if v2 is None else v2 + c
    e2 = jnp.exp(v2 - v1)
    w1 = 1.0 / (1.0 + e2)
    w2 = e2 / (1.0 + e2)
    tokens = logits_t.shape[1]
    row_id = lax.broadcasted_iota(jnp.int32, (MOE_EXPERTS, tokens), 0)
    comb_t = jnp.zeros((MOE_EXPERTS, tokens), F32)
    for g in range(MOE_GROUPS):
        for e in range(MOE_PER_GROUP):
            within = jnp.where(rank[e] == 0, w1, jnp.where(rank[e] == 1, w2, 0.0))
            c = jnp.where(sel[g], g_wt * within, 0.0)
            comb_t = jnp.where(row_id == g * MOE_PER_GROUP + e, c, comb_t)
    pad = jnp.zeros((ROUTER_LANES - MOE_EXPERTS, tokens), F32)
    return jnp.concatenate([comb_t, pad], axis=0).T


def _max_of(cols):
    out = cols[0]
    for c in cols[1:]:
        out = jnp.maximum(out, c)
    return out


def _moe_kernel(x_ref, g_ref, wr_ref, br_ref, wgu_ref, wd_ref, fg_ref, o_ref, h_ref, comb_ref, *, final_norm):
    e = pl.program_id(1)

    @pl.when(e == 0)
    def _route():
        x = x_ref[...]
        h = _rms_norm(x, g_ref[...]).astype(BF16)
        h_ref[...] = h
        logits_t = lax.dot_general(wr_ref[...], h, _NT, preferred_element_type=F32)
        comb_ref[...] = _routing_weights(logits_t + br_ref[...])
        o_ref[...] = x

    h = h_ref[...]
    gu = jnp.dot(h, wgu_ref[0], preferred_element_type=F32)
    hg, hu = gu[:, :D_EXPERT], gu[:, D_EXPERT:]
    lane = lax.broadcasted_iota(jnp.int32, comb_ref.shape, 1)
    c = jnp.sum(jnp.where(lane == e, comb_ref[...], 0.0), axis=1, keepdims=True)
    act = (hg * (1.0 / (1.0 + jnp.exp(-hg)))) * hu * c
    o_ref[...] += jnp.dot(act.astype(BF16), wd_ref[0], preferred_element_type=F32)

    if final_norm:
        @pl.when(e == MOE_EXPERTS - 1)
        def _final():
            o_ref[...] = _rms_norm(o_ref[...], fg_ref[...])


def _hier_moe(x2d, norm_g, w_router, b_router, w_gu, w_down, final_g, final_norm):
    rows = x2d.shape[0]
    const = lambda i, e: (0, 0)
    return pl.pallas_call(
        partial(_moe_kernel, final_norm=final_norm),
        grid=(rows // MOE_ROWS, MOE_EXPERTS),
        in_specs=[
            pl.BlockSpec((MOE_ROWS, D_MODEL), lambda i, e: (i, 0)),
            pl.BlockSpec((1, D_MODEL), const),
            pl.BlockSpec((ROUTER_ROWS, D_MODEL), const),
            pl.BlockSpec((ROUTER_ROWS, 1), const),
            pl.BlockSpec((1, D_MODEL, 2 * D_EXPERT), lambda i, e: (e, 0, 0)),
            pl.BlockSpec((1, D_EXPERT, D_MODEL), lambda i, e: (e, 0, 0)),
            pl.BlockSpec((1, D_MODEL), const),
        ],
        out_specs=pl.BlockSpec((MOE_ROWS, D_MODEL), lambda i, e: (i, 0)),
        out_shape=jax.ShapeDtypeStruct((rows, D_MODEL), F32),
        scratch_shapes=[pltpu.VMEM((MOE_ROWS, D_MODEL), BF16), pltpu.VMEM((MOE_ROWS, ROUTER_LANES), F32)],
        compiler_params=pltpu.CompilerParams(dimension_semantics=("parallel", "arbitrary"),
                                             vmem_limit_bytes=VMEM_LIMIT),
        name="hier_moe",
    )(x2d, norm_g.reshape(1, D_MODEL), w_router, b_router, w_gu, w_down, final_g.reshape(1, D_MODEL))


def _router_params(group_w, group_b, router_w, router_b):
    w = jnp.concatenate([group_w.astype(F32)] + [router_w[g].astype(F32) for g in range(MOE_GROUPS)], axis=1)
    b = jnp.concatenate([group_b.astype(F32), router_b.astype(F32).reshape(-1)])
    pad = ROUTER_ROWS - w.shape[1]
    return jnp.pad(w.T, ((0, pad), (0, 0))).astype(BF16), jnp.pad(b, (0, pad)).reshape(ROUTER_ROWS, 1)


def kernel(x, rel_bias, norm1_g, w_in, ssm_a_re, ssm_a_im, ssm_log_dt, ssm_b_re, ssm_b_im, ssm_c_re, ssm_c_im,
           ssm_d, ssm_glu_w, ssm_glu_b, pool_w, pool_b, pool_scale, w_out, norm2_g, moe_group_w, moe_group_b,
           moe_router_w, moe_router_b, moe_w_gate, moe_w_up, moe_w_down, final_norm_g):
    bsz, seq, dm = x.shape
    depth = w_in.shape[0]
    n_main = SSM_WIDTH + POOL_WIDTH + 2 * ATT_WIDTH
    x2d = x.astype(F32).reshape(bsz * seq, dm)
    tabs = _bias_tables(rel_bias)
    for l in range(depth):
        w_main = w_in[l, :, :n_main].astype(BF16)
        w_vt = w_in[l, :, n_main:].T.astype(BF16)
        u, p, q, k, vt = _in_projection(x2d, norm1_g[l], w_main, w_vt, bsz, seq)

        u_tb = u.reshape(bsz, seq, SSM_WIDTH).transpose(1, 0, 2).reshape(seq * bsz, SSM_WIDTH)
        params = _ssm_params(ssm_a_re[l], ssm_a_im[l], ssm_log_dt[l], ssm_b_re[l], ssm_b_im[l],
                             ssm_c_re[l], ssm_c_im[l])
        y_ssm = _ssm_mixer(u_tb, params, ssm_d[l], ssm_glu_w[l], ssm_glu_b[l], bsz, seq)
        y_ssm = y_ssm.reshape(seq, bsz, SSM_WIDTH).transpose(1, 0, 2).reshape(bsz * seq, SSM_WIDTH)

        y_pool = _pool_mixer(p.reshape(bsz, seq, POOL_WIDTH), pool_w[l], pool_b[l], pool_scale[l])
        y_att = _moba_attention(q.reshape(bsz, seq, ATT_WIDTH), k.reshape(bsz, seq, ATT_WIDTH), vt, tabs, rel_bias)

        x2d = _out_projection(x2d, y_ssm, y_pool.reshape(bsz * seq, POOL_WIDTH),
                              y_att.reshape(bsz * seq, ATT_WIDTH), w_out[l].astype(BF16))

        w_router, b_router = _router_params(moe_group_w[l], moe_group_b[l], moe_router_w[l], moe_router_b[l])
        w_gu = jnp.concatenate([moe_w_gate[l], moe_w_up[l]], axis=-1).astype(BF16)
        x2d = _hier_moe(x2d, norm2_g[l], w_router, b_router, w_gu, moe_w_down[l].astype(BF16),
                        final_norm_g, final_norm=(l == depth - 1))
    return x2d.reshape(bsz, seq, dm).astype(x.dtype)
```

```python
import math
from functools import partial

import jax
import jax.numpy as jnp
from jax import lax
from jax.experimental import pallas as pl
from jax.experimental.pallas import tpu as pltpu

F32 = jnp.float32
BF16 = jnp.bfloat16

D_MODEL = 1024
SSM_WIDTH = 256
POOL_WIDTH = 256
ATT_WIDTH = 512
SSM_GROUP = 16
SSM_GROUPS = SSM_WIDTH // SSM_GROUP
SSM_STATE = 64
SSM_STATES = SSM_GROUPS * SSM_STATE
POOL_WINDOWS = (2, 4, 8, 16)
POOL_GROUP = POOL_WIDTH // len(POOL_WINDOWS)
POOL_HALO = 16
HEAD_DIM = 64
HEADS = ATT_WIDTH // HEAD_DIM
HEAD_PAIR = 2 * HEAD_DIM
ONES_ROWS = 16
MOBA_BLOCK = 256
MOBA_TOPK = 3
FAR_GROUP = 4
REL_BUCKETS = 32
REL_MAX_EXACT = REL_BUCKETS // 2
REL_MAX_DIST = 128
MOE_GROUPS = 4
MOE_PER_GROUP = 4
MOE_EXPERTS = MOE_GROUPS * MOE_PER_GROUP
D_EXPERT = D_MODEL // 4
ROUTER_ROWS = 32
ROUTER_LANES = 128
RMS_EPS = 1e-6
NEG_INF = -1e30
LOG2E = math.log2(math.e)

PROJ_ROWS = 512
MOE_ROWS = 1024
POOL_ROWS = 512
SSM_STEPS = 128
VMEM_LIMIT = 48 * 1024 * 1024

_NT = (((1,), (1,)), ((), ()))

_BUCKET_START = list(range(REL_MAX_EXACT)) + [
    math.ceil(REL_MAX_EXACT * (REL_MAX_DIST / REL_MAX_EXACT) ** (k / (REL_BUCKETS - REL_MAX_EXACT)))
    for k in range(REL_BUCKETS - REL_MAX_EXACT)
]


def _rms_norm(x, g):
    return x * lax.rsqrt(jnp.mean(x * x, axis=-1, keepdims=True) + RMS_EPS) * g


def _inproj_kernel(x_ref, g_ref, w_ref, wvt_ref, u_ref, p_ref, q_ref, k_ref, vt_ref):
    h = _rms_norm(x_ref[...], g_ref[...]).astype(BF16)
    pr = jnp.dot(h, w_ref[...], preferred_element_type=F32)
    s0, s1, s2 = SSM_WIDTH, SSM_WIDTH + POOL_WIDTH, SSM_WIDTH + POOL_WIDTH + ATT_WIDTH
    u_ref[...] = pr[:, :s0]
    p_ref[...] = pr[:, s0:s1]
    q_ref[...] = (pr[:, s1:s2] * (HEAD_DIM ** -0.5 * LOG2E)).astype(BF16)
    k_ref[...] = pr[:, s2:].astype(BF16)
    vt_ref[0] = lax.dot_general(wvt_ref[...], h, _NT, preferred_element_type=F32).astype(BF16)


def _in_projection(x2d, g, w_main, w_vt, bsz, seq):
    rows = x2d.shape[0]
    tiles_per_seq = seq // PROJ_ROWS
    n_main = w_main.shape[1]
    const = lambda i: (0, 0)
    row_blk = lambda width: pl.BlockSpec((PROJ_ROWS, width), lambda i: (i, 0))
    return pl.pallas_call(
        _inproj_kernel,
        grid=(rows // PROJ_ROWS,),
        in_specs=[row_blk(D_MODEL), pl.BlockSpec((1, D_MODEL), const),
                  pl.BlockSpec((D_MODEL, n_main), const), pl.BlockSpec((ATT_WIDTH, D_MODEL), const)],
        out_specs=[pl.BlockSpec((PROJ_ROWS, SSM_WIDTH), lambda i: (i % tiles_per_seq, i // tiles_per_seq)),
                   row_blk(POOL_WIDTH), row_blk(ATT_WIDTH), row_blk(ATT_WIDTH),
                   pl.BlockSpec((1, ATT_WIDTH, PROJ_ROWS),
                                lambda i: (i // tiles_per_seq, 0, i % tiles_per_seq))],
        out_shape=[jax.ShapeDtypeStruct((seq, bsz * SSM_WIDTH), F32),
                   jax.ShapeDtypeStruct((rows, POOL_WIDTH), F32),
                   jax.ShapeDtypeStruct((rows, ATT_WIDTH), BF16),
                   jax.ShapeDtypeStruct((rows, ATT_WIDTH), BF16),
                   jax.ShapeDtypeStruct((bsz, ATT_WIDTH, seq), BF16)],
        compiler_params=pltpu.CompilerParams(dimension_semantics=("parallel",),
                                             vmem_limit_bytes=VMEM_LIMIT),
        name="in_projection",
    )(x2d, g.reshape(1, D_MODEL), w_main, w_vt)


def _ssm_kernel(u_ref, bm_ref, cm_ref, ar_ref, ai_ref, d_ref, gw_ref, gb_ref, o_ref, bu_ref, st_ref):
    bsz = st_ref.shape[0]
    ns = SSM_STATES

    @pl.when(pl.program_id(0) == 0)
    def _init():
        st_ref[...] = jnp.zeros_like(st_ref)

    u = u_ref[...]
    bu_ref[...] = jnp.dot(u.astype(BF16), bm_ref[...], preferred_element_type=F32)
    ar = jnp.broadcast_to(ar_ref[...], (bsz, ns))
    ai = jnp.broadcast_to(ai_ref[...], (bsz, ns))

    def step(t, carry):
        sr, si = carry
        r0 = pl.multiple_of(t * bsz, bsz)
        nr = ar * sr - ai * si + bu_ref[pl.ds(r0, bsz), 0:ns]
        ni = ar * si + ai * sr + bu_ref[pl.ds(r0, bsz), ns:2 * ns]
        bu_ref[pl.ds(r0, bsz), 0:ns] = nr
        bu_ref[pl.ds(r0, bsz), ns:2 * ns] = ni
        return nr, ni

    sr, si = lax.fori_loop(0, SSM_STEPS, step, (st_ref[:, 0:ns], st_ref[:, ns:2 * ns]), unroll=4)
    st_ref[:, 0:ns] = sr
    st_ref[:, ns:2 * ns] = si

    y = jnp.dot(bu_ref[...].astype(BF16), cm_ref[...], preferred_element_type=F32)
    y = y + d_ref[...] * u
    y = y * (0.5 * (1.0 + jnp.tanh(math.sqrt(2.0 / math.pi) * (y + 0.044715 * (y * y * y)))))
    z = jnp.dot(y.astype(BF16), gw_ref[...], preferred_element_type=F32) + gb_ref[...]
    o_ref[...] = (y * (1.0 / (1.0 + jnp.exp(-z)))).astype(o_ref.dtype)


def _ssm_params(a_re, a_im, log_dt, b_re, b_im, c_re, c_im):
    lam = lax.complex(a_re.astype(F32), a_im.astype(F32))
    dt = jnp.exp(log_dt.astype(F32))[:, None]
    lam_bar = jnp.exp(lam * dt)
    b = lax.complex(b_re.astype(F32), b_im.astype(F32))
    b_bar = ((lam_bar - 1.0) / lam)[..., None] * b
    eye = jnp.eye(SSM_GROUPS, dtype=F32)

    def bdiag_in(m):
        return jnp.einsum('gph,gk->ghkp', m, eye).reshape(SSM_WIDTH, SSM_STATES)

    def bdiag_out(m):
        return jnp.einsum('ghp,gk->gpkh', m, eye).reshape(SSM_STATES, SSM_WIDTH)

    bm = jnp.concatenate([bdiag_in(jnp.real(b_bar)), bdiag_in(jnp.imag(b_bar))], axis=1)
    cm = jnp.concatenate([bdiag_out(c_re.astype(F32)), -bdiag_out(c_im.astype(F32))], axis=0)
    return (bm.astype(BF16), cm.astype(BF16),
            jnp.real(lam_bar).reshape(1, SSM_STATES), jnp.imag(lam_bar).reshape(1, SSM_STATES))


def _ssm_mixer(u_tb, params, d, glu_w, glu_b, bsz, seq):
    bm, cm, ar, ai = params
    rows = SSM_STEPS * bsz
    const = lambda c: (0, 0)
    return pl.pallas_call(
        _ssm_kernel,
        grid=(seq // SSM_STEPS,),
        in_specs=[
            pl.BlockSpec((rows, SSM_WIDTH), lambda c: (c, 0)),
            pl.BlockSpec((SSM_WIDTH, 2 * SSM_STATES), const),
            pl.BlockSpec((2 * SSM_STATES, SSM_WIDTH), const),
            pl.BlockSpec((1, SSM_STATES), const),
            pl.BlockSpec((1, SSM_STATES), const),
            pl.BlockSpec((1, SSM_WIDTH), const),
            pl.BlockSpec((SSM_WIDTH, SSM_WIDTH), const),
            pl.BlockSpec((1, SSM_WIDTH), const),
        ],
        out_specs=pl.BlockSpec((rows, SSM_WIDTH), lambda c: (c, 0)),
        out_shape=jax.ShapeDtypeStruct((seq * bsz, SSM_WIDTH), BF16),
        scratch_shapes=[pltpu.VMEM((rows, 2 * SSM_STATES), F32), pltpu.VMEM((bsz, 2 * SSM_STATES), F32)],
        compiler_params=pltpu.CompilerParams(dimension_semantics=("arbitrary",),
                                             vmem_limit_bytes=VMEM_LIMIT),
        name="ssm_mixer",
    )(u_tb, bm, cm, ar, ai, d.reshape(1, SSM_WIDTH), glu_w.astype(BF16), glu_b.reshape(1, SSM_WIDTH))


def _pool_kernel(halo_ref, p_ref, w_ref, b_ref, s_ref, o_ref):
    i = pl.program_id(1)
    halo = jnp.where(i == 0, 0.0, halo_ref[0])
    p = p_ref[0]
    ext = jnp.concatenate([halo, p], axis=0)
    s2 = ext + pltpu.roll(ext, 1, 0)
    s4 = s2 + pltpu.roll(s2, 2, 0)
    s8 = s4 + pltpu.roll(s4, 4, 0)
    s16 = s8 + pltpu.roll(s8, 8, 0)
    sums = (s2, s4, s8, s16)
    lane = lax.broadcasted_iota(jnp.int32, p.shape, 1)
    t1 = (lax.broadcasted_iota(jnp.int32, p.shape, 0) + i * POOL_ROWS + 1).astype(F32)
    mean = None
    for gi, win in enumerate(POOL_WINDOWS):
        m = sums[gi][POOL_HALO:] / jnp.minimum(t1, float(win))
        mean = m if mean is None else jnp.where(lane >= gi * POOL_GROUP, m, mean)
    dlt = (mean - p).astype(BF16)
    y = jnp.dot(dlt, w_ref[...], preferred_element_type=F32)
    o_ref[0] = ((y + b_ref[...]) * s_ref[...]).astype(o_ref.dtype)


def _pool_mixer(p, w, b, scale):
    bsz, seq, _ = p.shape
    eye = jnp.eye(len(POOL_WINDOWS), dtype=F32)
    w_bd = jnp.einsum('gcd,gk->gckd', w.astype(F32), eye).reshape(POOL_WIDTH, POOL_WIDTH).astype(BF16)
    halo_blocks = POOL_ROWS // POOL_HALO
    const = lambda bi, i: (0, 0)
    return pl.pallas_call(
        _pool_kernel,
        grid=(bsz, seq // POOL_ROWS),
        in_specs=[
            pl.BlockSpec((1, POOL_HALO, POOL_WIDTH), lambda bi, i: (bi, jnp.maximum(i * halo_blocks - 1, 0), 0)),
            pl.BlockSpec((1, POOL_ROWS, POOL_WIDTH), lambda bi, i: (bi, i, 0)),
            pl.BlockSpec((POOL_WIDTH, POOL_WIDTH), const),
            pl.BlockSpec((1, POOL_WIDTH), const),
            pl.BlockSpec((1, POOL_WIDTH), const),
        ],
        out_specs=pl.BlockSpec((1, POOL_ROWS, POOL_WIDTH), lambda bi, i: (bi, i, 0)),
        out_shape=jax.ShapeDtypeStruct((bsz, seq, POOL_WIDTH), BF16),
        compiler_params=pltpu.CompilerParams(dimension_semantics=("parallel", "parallel")),
        name="pool_mixer",
    )(p, p, w_bd, b.reshape(1, POOL_WIDTH), scale.reshape(1, POOL_WIDTH))


def _bias_table_kernel(bias_ref, tab_ref):
    h = pl.program_id(0)
    kk = lax.broadcasted_iota(jnp.int32, (MOBA_BLOCK, MOBA_BLOCK), 0)
    qq = lax.broadcasted_iota(jnp.int32, (MOBA_BLOCK, MOBA_BLOCK), 1)
    for which in range(2):
        rel = qq - kk + which * MOBA_BLOCK
        val = jnp.full((MOBA_BLOCK, MOBA_BLOCK), bias_ref[0, h], F32)
        for b in range(1, REL_BUCKETS):
            val = jnp.where(rel >= _BUCKET_START[b], bias_ref[b, h], val)
        tab_ref[0, which] = jnp.where(rel >= 0, val * LOG2E, NEG_INF)


def _bias_tables(rel_bias):
    return pl.pallas_call(
        _bias_table_kernel,
        grid=(HEADS,),
        in_specs=[pl.BlockSpec(memory_space=pltpu.SMEM)],
        out_specs=pl.BlockSpec((1, 2, MOBA_BLOCK, MOBA_BLOCK), lambda h: (h, 0, 0, 0)),
        out_shape=jax.ShapeDtypeStruct((HEADS, 2, MOBA_BLOCK, MOBA_BLOCK), F32),
        name="bias_tables",
    )(rel_bias.astype(F32))


def _attn_kernel(bias_ref, q_ref, k_ref, vt_ref, tab_ref, o_ref,
                 km_ref, va_ref, neg_ref, acc_ref, m_ref, s_ref, ahead_ref):
    seq = k_ref.shape[1]
    nb = seq // MOBA_BLOCK
    pr = pl.program_id(1)
    qi = pl.program_id(2)
    lane = lax.broadcasted_iota(jnp.int32, (1, HEAD_PAIR), 1)
    head_lanes = [lane < HEAD_DIM, lane >= HEAD_DIM]

    @pl.when(qi == 0)
    def _prep():
        kf = k_ref[0].astype(F32).reshape(nb, MOBA_BLOCK, HEAD_PAIR)
        km = jnp.sum(kf, axis=1) * (1.0 / MOBA_BLOCK)
        for hh in range(2):
            km_ref[hh] = jnp.where(head_lanes[hh], km, 0.0).astype(BF16)
        ones = jnp.ones((ONES_ROWS, seq), BF16)
        va_ref[0, 0:HEAD_DIM, :] = vt_ref[0, 0:HEAD_DIM, :]
        va_ref[0, HEAD_DIM:, :] = ones
        va_ref[1, 0:ONES_ROWS, :] = ones
        va_ref[1, ONES_ROWS:, :] = vt_ref[0, HEAD_DIM:, :]

    q = q_ref[0].astype(F32)
    qh = [jnp.where(head_lanes[hh], q, 0.0).astype(BF16) for hh in range(2)]
    blk_id = lax.broadcasted_iota(jnp.int32, (nb, MOBA_BLOCK), 0)

    for hh in range(2):
        g = lax.dot_general(km_ref[hh], qh[hh], _NT, preferred_element_type=F32)
        g = jnp.where(blk_id < qi, g, NEG_INF)
        cnt = jnp.zeros((nb, MOBA_BLOCK), jnp.int32)
        for m in range(nb):
            gm = g[m:m + 1, :]
            cnt = cnt + ((gm > g) | ((gm == g) & (blk_id > m))).astype(jnp.int32)
        neg_ref[hh] = jnp.where((cnt < MOBA_TOPK) & (blk_id < qi), 0.0, NEG_INF)

    def key_rows(j):
        return pl.ds(pl.multiple_of(j * MOBA_BLOCK, MOBA_BLOCK), MOBA_BLOCK)

    def score(hh, j):
        return lax.dot_general(k_ref[0, key_rows(j), :], qh[hh], _NT, preferred_element_type=F32)

    def attend(items, m_prev, parked=None, ahead=None):
        units = [(hh, i) for i in range(len(items)) for hh in range(2)]
        scores = {}
        if parked is None:
            for hh, i in units:
                scores[hh, i] = score(hh, items[i][0])
        ahead_units = [] if ahead is None else [(hh, i) for i in range(len(ahead[0])) for hh in range(2)]
        per_unit = -(-len(ahead_units) // len(units))
        mbs, pvs, staged = {}, {}, 0
        for n, (hh, i) in enumerate(units):
            for ahh, ai in ahead_units[n * per_unit:(n + 1) * per_unit]:
                ahead_ref[ahead[1], 2 * ai + ahh] = score(ahh, ahead[0][ai])
            add = items[i][1][hh]
            s = scores[hh, i] if parked is None else ahead_ref[parked, 2 * i + hh]
            if add.shape[0] == 1:
                smax = jnp.max(s, axis=0, keepdims=True)
                mbs[hh, i] = smax + add
                p = jnp.exp2(s - smax)
            else:
                s_ref[staged] = s
                s = s_ref[staged] + add
                staged += 1
                mbs[hh, i] = jnp.max(s, axis=0, keepdims=True)
                p = jnp.exp2(s - mbs[hh, i])
            pvs[hh, i] = jnp.dot(va_ref[hh, :, key_rows(items[i][0])], p.astype(BF16),
                                 preferred_element_type=F32)
        out = []
        for hh in range(2):
            m_new = None if m_prev is None else m_prev[hh]
            for i in range(len(items)):
                m_new = mbs[hh, i] if m_new is None else jnp.maximum(m_new, mbs[hh, i])
            acc = None if m_prev is None else acc_ref[hh] * jnp.exp2(m_prev[hh] - m_new)
            for i in range(len(items)):
                term = pvs[hh, i] * jnp.exp2(mbs[hh, i] - m_new)
                acc = term if acc is None else acc + term
            acc_ref[hh] = acc
            out.append(m_new)
        return tuple(out)

    def store_max(ms):
        m_ref[0] = ms[0]
        m_ref[1] = ms[1]

    def own_item():
        return (qi, [tab_ref[hh, 0] for hh in range(2)])

    def prev_item():
        return (qi - 1, [tab_ref[hh, 1] + neg_ref[hh, pl.ds(qi - 1, 1), :] for hh in range(2)])

    def far_item(j):
        return (j, [neg_ref[hh, pl.ds(j, 1), :] + bias_ref[REL_BUCKETS - 1, 2 * pr + hh] * LOG2E
                    for hh in range(2)])

    n_far = jnp.maximum(qi - 1, 0)
    n_rem = n_far % FAR_GROUP
    n_groups = n_far // FAR_GROUP

    def group_blocks(g):
        return [n_rem + g * FAR_GROUP + b for b in range(FAR_GROUP)]

    @pl.when(qi == 0)
    def _first_block():
        store_max(attend([own_item()], None))

    for r in range(FAR_GROUP):
        for follows in (False, True):
            @pl.when((qi > 0) & (n_rem == r) & ((n_groups > 0) == follows))
            def _near_blocks(r=r, follows=follows):
                items = [own_item(), prev_item()] + [far_item(j) for j in range(r)]
                store_max(attend(items, None, ahead=(group_blocks(0), 0) if follows else None))

    max_groups = (nb - 2) // FAR_GROUP
    for g in range(max_groups):
        for follows in (False, True)[:1 if g + 1 == max_groups else 2]:
            @pl.when((g < n_groups) & ((g + 1 < n_groups) == follows))
            def _far_group(g=g, follows=follows):
                items = [far_item(j) for j in group_blocks(g)]
                ahead = (group_blocks(g + 1), (g + 1) % 2) if follows else None
                store_max(attend(items, (m_ref[0], m_ref[1]), parked=g % 2, ahead=ahead))

    oa = acc_ref[0]
    ob = acc_ref[1]
    out_t = jnp.concatenate([oa[0:HEAD_DIM] / oa[HEAD_DIM:HEAD_DIM + 1],
                             ob[ONES_ROWS:] / ob[0:1]], axis=0)
    o_ref[0] = out_t.T.astype(o_ref.dtype)


def _moba_attention(q, k, vt, tabs, rel_bias):
    bsz, seq, _ = q.shape
    assert seq % MOBA_BLOCK == 0
    nb = seq // MOBA_BLOCK
    return pl.pallas_call(
        _attn_kernel,
        grid=(bsz, HEADS // 2, nb),
        in_specs=[
            pl.BlockSpec(memory_space=pltpu.SMEM),
            pl.BlockSpec((1, MOBA_BLOCK, HEAD_PAIR), lambda b, p, i: (b, i, p)),
            pl.BlockSpec((1, seq, HEAD_PAIR), lambda b, p, i: (b, 0, p)),
            pl.BlockSpec((1, HEAD_PAIR, seq), lambda b, p, i: (b, p, 0)),
            pl.BlockSpec((2, 2, MOBA_BLOCK, MOBA_BLOCK), lambda b, p, i: (p, 0, 0, 0)),
        ],
        out_specs=pl.BlockSpec((1, MOBA_BLOCK, HEAD_PAIR), lambda b, p, i: (b, i, p)),
        out_shape=jax.ShapeDtypeStruct((bsz, seq, ATT_WIDTH), BF16),
        scratch_shapes=[
            pltpu.VMEM((2, nb, HEAD_PAIR), BF16),
            pltpu.VMEM((2, HEAD_DIM + ONES_ROWS, seq), BF16),
            pltpu.VMEM((2, nb, MOBA_BLOCK), F32),
            pltpu.VMEM((2, HEAD_DIM + ONES_ROWS, MOBA_BLOCK), F32),
            pltpu.VMEM((2, 1, MOBA_BLOCK), F32),
            pltpu.VMEM((4, MOBA_BLOCK, MOBA_BLOCK), F32),
            pltpu.VMEM((2, 2 * FAR_GROUP, MOBA_BLOCK, MOBA_BLOCK), F32),
        ],
        compiler_params=pltpu.CompilerParams(
            dimension_semantics=("parallel", "parallel", "arbitrary"), vmem_limit_bytes=VMEM_LIMIT),
        name="moba_attention",
    )(rel_bias.astype(F32), q, k, vt, tabs)


def _outproj_kernel(x_ref, ys_ref, yp_ref, ya_ref, w_ref, o_ref):
    s0, s1 = SSM_WIDTH, SSM_WIDTH + POOL_WIDTH
    acc = jnp.dot(ys_ref[...], w_ref[0:s0, :], preferred_element_type=F32)
    acc = acc + jnp.dot(yp_ref[...], w_ref[s0:s1, :], preferred_element_type=F32)
    acc = acc + jnp.dot(ya_ref[...], w_ref[s1:, :], preferred_element_type=F32)
    o_ref[...] = x_ref[...] + acc


def _out_projection(x2d, y_ssm_tm, y_pool, y_att, w_out):
    rows = x2d.shape[0]
    tiles_per_seq = y_ssm_tm.shape[0] // PROJ_ROWS
    row_blk = lambda width: pl.BlockSpec((PROJ_ROWS, width), lambda i: (i, 0))
    return pl.pallas_call(
        _outproj_kernel,
        grid=(rows // PROJ_ROWS,),
        in_specs=[row_blk(D_MODEL),
                  pl.BlockSpec((PROJ_ROWS, SSM_WIDTH), lambda i: (i % tiles_per_seq, i // tiles_per_seq)),
                  row_blk(POOL_WIDTH), row_blk(ATT_WIDTH),
                  pl.BlockSpec((D_MODEL, D_MODEL), lambda i: (0, 0))],
        out_specs=row_blk(D_MODEL),
        out_shape=jax.ShapeDtypeStruct((rows, D_MODEL), F32),
        compiler_params=pltpu.CompilerParams(dimension_semantics=("parallel",),
                                             vmem_limit_bytes=VMEM_LIMIT),
        name="out_projection",
    )(x2d, y_ssm_tm, y_pool, y_att, w_out)


def _routing_weights(logits_t):
    col = lambda i: logits_t[i:i + 1, :]
    gl = [col(g) for g in range(MOE_GROUPS)]
    gmax = _max_of(gl)
    sel, taken = [], None
    for g in range(MOE_GROUPS):
        hit = gl[g] == gmax
        if taken is not None:
            hit = hit & jnp.logical_not(taken)
        taken = hit if taken is None else (taken | hit)
        sel.append(hit)
    denom = None
    for g in range(MOE_GROUPS):
        e = jnp.exp(gl[g] - gmax)
        denom = e if denom is None else denom + e
    g_wt = 1.0 / denom
    el = []
    for e in range(MOE_PER_GROUP):
        v = col(MOE_GROUPS + (MOE_GROUPS - 1) * MOE_PER_GROUP + e)
        for g in range(MOE_GROUPS - 2, -1, -1):
            v = jnp.where(sel[g], col(MOE_GROUPS + g * MOE_PER_GROUP + e), v)
        el.append(v)
    rank = []
    for e in range(MOE_PER_GROUP):
        r = jnp.zeros_like(el[e], dtype=jnp.int32)
        for m in range(MOE_PER_GROUP):
            if m == e:
                continue
            beats = (el[m] >= el[e]) if m < e else (el[m] > el[e])
            r = r + beats.astype(jnp.int32)
        rank.append(r)
    v1 = _max_of(el)
    v2 = None
    for e in range(MOE_PER_GROUP):
        c = jnp.where(rank[e] == 1, el[e], 0.0)
        v2 = c if v2 is None else v2 + c
    e2 = jnp.exp(v2 - v1)
    w1 = 1.0 / (1.0 + e2)
    w2 = e2 / (1.0 + e2)
    tokens = logits_t.shape[1]
    row_id = lax.broadcasted_iota(jnp.int32, (MOE_EXPERTS, tokens), 0)
    comb_t = jnp.zeros((MOE_EXPERTS, tokens), F32)
    for g in range(MOE_GROUPS):
        for e in range(MOE_PER_GROUP):
            within = jnp.where(rank[e] == 0, w1, jnp.where(rank[e] == 1, w2, 0.0))
            c = jnp.where(sel[g], g_wt * within, 0.0)
            comb_t = jnp.where(row_id == g * MOE_PER_GROUP + e, c, comb_t)
    pad = jnp.zeros((ROUTER_LANES - MOE_EXPERTS, tokens), F32)
    return jnp.concatenate([comb_t, pad], axis=0).T


def _max_of(cols):
    out = cols[0]
    for c in cols[1:]:
        out = jnp.maximum(out, c)
    return out


def _moe_kernel(x_ref, g_ref, wr_ref, br_ref, wgu_ref, wd_ref, fg_ref, o_ref, h_ref, comb_ref, *, final_norm):
    e = pl.program_id(1)

    @pl.when(e == 0)
    def _route():
        x = x_ref[...]
        h = _rms_norm(x, g_ref[...]).astype(BF16)
        h_ref[...] = h
        logits_t = lax.dot_general(wr_ref[...], h, _NT, preferred_element_type=F32)
        comb_ref[...] = _routing_weights(logits_t + br_ref[...])
        o_ref[...] = x

    h = h_ref[...]
    gu = jnp.dot(h, wgu_ref[0], preferred_element_type=F32)
    hg, hu = gu[:, :D_EXPERT], gu[:, D_EXPERT:]
    lane = lax.broadcasted_iota(jnp.int32, comb_ref.shape, 1)
    c = jnp.sum(jnp.where(lane == e, comb_ref[...], 0.0), axis=1, keepdims=True)
    act = (hg * (1.0 / (1.0 + jnp.exp(-hg)))) * hu * c
    o_ref[...] += jnp.dot(act.astype(BF16), wd_ref[0], preferred_element_type=F32)

    if final_norm:
        @pl.when(e == MOE_EXPERTS - 1)
        def _final():
            o_ref[...] = _rms_norm(o_ref[...], fg_ref[...])


def _hier_moe(x2d, norm_g, w_router, b_router, w_gu, w_down, final_g, final_norm):
    rows = x2d.shape[0]
    const = lambda i, e: (0, 0)
    return pl.pallas_call(
        partial(_moe_kernel, final_norm=final_norm),
        grid=(rows // MOE_ROWS, MOE_EXPERTS),
        in_specs=[
            pl.BlockSpec((MOE_ROWS, D_MODEL), lambda i, e: (i, 0)),
            pl.BlockSpec((1, D_MODEL), const),
            pl.BlockSpec((ROUTER_ROWS, D_MODEL), const),
            pl.BlockSpec((ROUTER_ROWS, 1), const),
            pl.BlockSpec((1, D_MODEL, 2 * D_EXPERT), lambda i, e: (e, 0, 0)),
            pl.BlockSpec((1, D_EXPERT, D_MODEL), lambda i, e: (e, 0, 0)),
            pl.BlockSpec((1, D_MODEL), const),
        ],
        out_specs=pl.BlockSpec((MOE_ROWS, D_MODEL), lambda i, e: (i, 0)),
        out_shape=jax.ShapeDtypeStruct((rows, D_MODEL), F32),
        scratch_shapes=[pltpu.VMEM((MOE_ROWS, D_MODEL), BF16), pltpu.VMEM((MOE_ROWS, ROUTER_LANES), F32)],
        compiler_params=pltpu.CompilerParams(dimension_semantics=("parallel", "arbitrary"),
                                             vmem_limit_bytes=VMEM_LIMIT),
        name="hier_moe",
    )(x2d, norm_g.reshape(1, D_MODEL), w_router, b_router, w_gu, w_down, final_g.reshape(1, D_MODEL))


def _router_params(group_w, group_b, router_w, router_b):
    w = jnp.concatenate([group_w.astype(F32)] + [router_w[g].astype(F32) for g in range(MOE_GROUPS)], axis=1)
    b = jnp.concatenate([group_b.astype(F32), router_b.astype(F32).reshape(-1)])
    pad = ROUTER_ROWS - w.shape[1]
    return jnp.pad(w.T, ((0, pad), (0, 0))).astype(BF16), jnp.pad(b, (0, pad)).reshape(ROUTER_ROWS, 1)


def kernel(x, rel_bias, norm1_g, w_in, ssm_a_re, ssm_a_im, ssm_log_dt, ssm_b_re, ssm_b_im, ssm_c_re, ssm_c_im,
           ssm_d, ssm_glu_w, ssm_glu_b, pool_w, pool_b, pool_scale, w_out, norm2_g, moe_group_w, moe_group_b,
           moe_router_w, moe_router_b, moe_w_gate, moe_w_up, moe_w_down, final_norm_g):
    bsz, seq, dm = x.shape
    depth = w_in.shape[0]
    n_main = SSM_WIDTH + POOL_WIDTH + 2 * ATT_WIDTH
    x2d = x.astype(F32).reshape(bsz * seq, dm)
    tabs = _bias_tables(rel_bias)
    for l in range(depth):
        w_main = w_in[l, :, :n_main].astype(BF16)
        w_vt = w_in[l, :, n_main:].T.astype(BF16)
        u, p, q, k, vt = _in_projection(x2d, norm1_g[l], w_main, w_vt, bsz, seq)

        params = _ssm_params(ssm_a_re[l], ssm_a_im[l], ssm_log_dt[l], ssm_b_re[l], ssm_b_im[l],
                             ssm_c_re[l], ssm_c_im[l])
        y_ssm = _ssm_mixer(u.reshape(seq * bsz, SSM_WIDTH), params, ssm_d[l], ssm_glu_w[l], ssm_glu_b[l], bsz, seq)
        y_ssm = y_ssm.reshape(seq, bsz * SSM_WIDTH)

        y_pool = _pool_mixer(p.reshape(bsz, seq, POOL_WIDTH), pool_w[l], pool_b[l], pool_scale[l])
        y_att = _moba_attention(q.reshape(bsz, seq, ATT_WIDTH), k.reshape(bsz, seq, ATT_WIDTH), vt, tabs, rel_bias)

        x2d = _out_projection(x2d, y_ssm, y_pool.reshape(bsz * seq, POOL_WIDTH),
                              y_att.reshape(bsz * seq, ATT_WIDTH), w_out[l].astype(BF16))

        w_router, b_router = _router_params(moe_group_w[l], moe_group_b[l], moe_router_w[l], moe_router_b[l])
        w_gu = jnp.concatenate([moe_w_gate[l], moe_w_up[l]], axis=-1).astype(BF16)
        x2d = _hier_moe(x2d, norm2_g[l], w_router, b_router, w_gu, moe_w_down[l].astype(BF16),
                        final_norm_g, final_norm=(l == depth - 1))
    return x2d.reshape(bsz, seq, dm).astype(x.dtype)
```

```python
import math
from functools import partial

import jax
import jax.numpy as jnp
from jax import lax
from jax.experimental import pallas as pl
from jax.experimental.pallas import tpu as pltpu

F32 = jnp.float32
BF16 = jnp.bfloat16

LANES = 128
D_MODEL = 1024
SSM_WIDTH = 256
POOL_WIDTH = 256
ATT_WIDTH = 512
SSM_GROUP = 16
SSM_GROUPS = SSM_WIDTH // SSM_GROUP
SSM_STATE = 64
SSM_STATES = SSM_GROUPS * SSM_STATE
POOL_WINDOWS = (2, 4, 8, 16)
POOL_GROUP = POOL_WIDTH // len(POOL_WINDOWS)
POOL_HALO = 16
HEAD_DIM = 64
HEADS = ATT_WIDTH // HEAD_DIM
HEAD_PAIR = 2 * HEAD_DIM
ONES_ROWS = 16
MOBA_BLOCK = 256
MOBA_TOPK = 3
FAR_GROUP = 4
REL_BUCKETS = 32
REL_MAX_EXACT = REL_BUCKETS // 2
REL_MAX_DIST = 128
MOE_GROUPS = 4
MOE_PER_GROUP = 4
MOE_EXPERTS = MOE_GROUPS * MOE_PER_GROUP
D_EXPERT = D_MODEL // 4
ROUTER_ROWS = 32
ROUTER_LANES = 128
RMS_EPS = 1e-6
NEG_INF = -1e30
LOG2E = math.log2(math.e)

PROJ_ROWS = 512
MOE_ROWS = 1024
POOL_ROWS = 512
SSM_STEPS = 128
VMEM_LIMIT = 48 * 1024 * 1024

_NT = (((1,), (1,)), ((), ()))

_BUCKET_START = list(range(REL_MAX_EXACT)) + [
    math.ceil(REL_MAX_EXACT * (REL_MAX_DIST / REL_MAX_EXACT) ** (k / (REL_BUCKETS - REL_MAX_EXACT)))
    for k in range(REL_BUCKETS - REL_MAX_EXACT)
]


def _rms_norm(x, g):
    return x * lax.rsqrt(jnp.mean(x * x, axis=-1, keepdims=True) + RMS_EPS) * g


def _inproj_kernel(x_ref, g_ref, w_ref, wvt_ref, u_ref, p_ref, q_ref, k_ref, vt_ref):
    h = _rms_norm(x_ref[...], g_ref[...]).astype(BF16)
    pr = jnp.dot(h, w_ref[...], preferred_element_type=F32)
    s0, s1, s2 = SSM_WIDTH, SSM_WIDTH + POOL_WIDTH, SSM_WIDTH + POOL_WIDTH + ATT_WIDTH
    u_ref[...] = pr[:, :s0]
    p_ref[...] = pr[:, s0:s1]
    q_ref[...] = (pr[:, s1:s2] * (HEAD_DIM ** -0.5 * LOG2E)).astype(BF16)
    k_ref[...] = pr[:, s2:].astype(BF16)
    vt_ref[0] = lax.dot_general(wvt_ref[...], h, _NT, preferred_element_type=F32).astype(BF16)


def _in_projection(x2d, g, w_main, w_vt, bsz, seq):
    rows = x2d.shape[0]
    tiles_per_seq = seq // PROJ_ROWS
    n_main = w_main.shape[1]
    const = lambda i: (0, 0)
    row_blk = lambda width: pl.BlockSpec((PROJ_ROWS, width), lambda i: (i, 0))
    return pl.pallas_call(
        _inproj_kernel,
        grid=(rows // PROJ_ROWS,),
        in_specs=[row_blk(D_MODEL), pl.BlockSpec((1, D_MODEL), const),
                  pl.BlockSpec((D_MODEL, n_main), const), pl.BlockSpec((ATT_WIDTH, D_MODEL), const)],
        out_specs=[pl.BlockSpec((PROJ_ROWS, SSM_WIDTH), lambda i: (i % tiles_per_seq, i // tiles_per_seq)),
                   row_blk(POOL_WIDTH), row_blk(ATT_WIDTH), row_blk(ATT_WIDTH),
                   pl.BlockSpec((1, ATT_WIDTH, PROJ_ROWS),
                                lambda i: (i // tiles_per_seq, 0, i % tiles_per_seq))],
        out_shape=[jax.ShapeDtypeStruct((seq, bsz * SSM_WIDTH), F32),
                   jax.ShapeDtypeStruct((rows, POOL_WIDTH), F32),
                   jax.ShapeDtypeStruct((rows, ATT_WIDTH), BF16),
                   jax.ShapeDtypeStruct((rows, ATT_WIDTH), BF16),
                   jax.ShapeDtypeStruct((bsz, ATT_WIDTH, seq), BF16)],
        compiler_params=pltpu.CompilerParams(dimension_semantics=("parallel",),
                                             vmem_limit_bytes=VMEM_LIMIT),
        name="in_projection",
    )(x2d, g.reshape(1, D_MODEL), w_main, w_vt)


def _ssm_kernel(u_ref, bm_ref, cm_ref, ar_ref, ai_ref, d_ref, gw_ref, gb_ref, o_ref, bu_ref, st_ref, tb_ref):
    bsz = st_ref.shape[0]
    ns = SSM_STATES

    @pl.when(pl.program_id(0) == 0)
    def _init():
        st_ref[...] = jnp.zeros_like(st_ref)

    halves = SSM_WIDTH // LANES
    for b in range(bsz):
        for h in range(halves):
            c0 = b * SSM_WIDTH + h * LANES
            tb_ref[h, pl.ds(b, SSM_STEPS, stride=bsz), :] = u_ref[:, c0:c0 + LANES]
    u = jnp.concatenate([tb_ref[h] for h in range(halves)], axis=1)
    bu_ref[...] = jnp.dot(u.astype(BF16), bm_ref[...], preferred_element_type=F32)
    ar = jnp.broadcast_to(ar_ref[...], (bsz, ns))
    ai = jnp.broadcast_to(ai_ref[...], (bsz, ns))

    def step(t, carry):
        sr, si = carry
        r0 = pl.multiple_of(t * bsz, bsz)
        nr = ar * sr - ai * si + bu_ref[pl.ds(r0, bsz), 0:ns]
        ni = ar * si + ai * sr + bu_ref[pl.ds(r0, bsz), ns:2 * ns]
        bu_ref[pl.ds(r0, bsz), 0:ns] = nr
        bu_ref[pl.ds(r0, bsz), ns:2 * ns] = ni
        return nr, ni

    sr, si = lax.fori_loop(0, SSM_STEPS, step, (st_ref[:, 0:ns], st_ref[:, ns:2 * ns]), unroll=4)
    st_ref[:, 0:ns] = sr
    st_ref[:, ns:2 * ns] = si

    y = jnp.dot(bu_ref[...].astype(BF16), cm_ref[...], preferred_element_type=F32)
    y = y + d_ref[...] * u
    y = y * (0.5 * (1.0 + jnp.tanh(math.sqrt(2.0 / math.pi) * (y + 0.044715 * (y * y * y)))))
    z = jnp.dot(y.astype(BF16), gw_ref[...], preferred_element_type=F32) + gb_ref[...]
    out = y * (1.0 / (1.0 + jnp.exp(-z)))
    for h in range(halves):
        tb_ref[h] = out[:, h * LANES:(h + 1) * LANES]
    for b in range(bsz):
        for h in range(halves):
            c0 = b * SSM_WIDTH + h * LANES
            o_ref[:, c0:c0 + LANES] = tb_ref[h, pl.ds(b, SSM_STEPS, stride=bsz), :].astype(o_ref.dtype)


def _ssm_params(a_re, a_im, log_dt, b_re, b_im, c_re, c_im):
    lam = lax.complex(a_re.astype(F32), a_im.astype(F32))
    dt = jnp.exp(log_dt.astype(F32))[:, None]
    lam_bar = jnp.exp(lam * dt)
    b = lax.complex(b_re.astype(F32), b_im.astype(F32))
    b_bar = ((lam_bar - 1.0) / lam)[..., None] * b
    eye = jnp.eye(SSM_GROUPS, dtype=F32)

    def bdiag_in(m):
        return jnp.einsum('gph,gk->ghkp', m, eye).reshape(SSM_WIDTH, SSM_STATES)

    def bdiag_out(m):
        return jnp.einsum('ghp,gk->gpkh', m, eye).reshape(SSM_STATES, SSM_WIDTH)

    bm = jnp.concatenate([bdiag_in(jnp.real(b_bar)), bdiag_in(jnp.imag(b_bar))], axis=1)
    cm = jnp.concatenate([bdiag_out(c_re.astype(F32)), -bdiag_out(c_im.astype(F32))], axis=0)
    return (bm.astype(BF16), cm.astype(BF16),
            jnp.real(lam_bar).reshape(1, SSM_STATES), jnp.imag(lam_bar).reshape(1, SSM_STATES))


def _ssm_mixer(u_tm, params, d, glu_w, glu_b, bsz, seq):
    bm, cm, ar, ai = params
    rows = SSM_STEPS * bsz
    const = lambda c: (0, 0)
    return pl.pallas_call(
        _ssm_kernel,
        grid=(seq // SSM_STEPS,),
        in_specs=[
            pl.BlockSpec((SSM_STEPS, bsz * SSM_WIDTH), lambda c: (c, 0)),
            pl.BlockSpec((SSM_WIDTH, 2 * SSM_STATES), const),
            pl.BlockSpec((2 * SSM_STATES, SSM_WIDTH), const),
            pl.BlockSpec((1, SSM_STATES), const),
            pl.BlockSpec((1, SSM_STATES), const),
            pl.BlockSpec((1, SSM_WIDTH), const),
            pl.BlockSpec((SSM_WIDTH, SSM_WIDTH), const),
            pl.BlockSpec((1, SSM_WIDTH), const),
        ],
        out_specs=pl.BlockSpec((SSM_STEPS, bsz * SSM_WIDTH), lambda c: (c, 0)),
        out_shape=jax.ShapeDtypeStruct((seq, bsz * SSM_WIDTH), BF16),
        scratch_shapes=[pltpu.VMEM((rows, 2 * SSM_STATES), F32), pltpu.VMEM((bsz, 2 * SSM_STATES), F32),
                        pltpu.VMEM((SSM_WIDTH // LANES, rows, LANES), F32)],
        compiler_params=pltpu.CompilerParams(dimension_semantics=("arbitrary",),
                                             vmem_limit_bytes=VMEM_LIMIT),
        name="ssm_mixer",
    )(u_tm, bm, cm, ar, ai, d.reshape(1, SSM_WIDTH), glu_w.astype(BF16), glu_b.reshape(1, SSM_WIDTH))


def _pool_kernel(halo_ref, p_ref, w_ref, b_ref, s_ref, o_ref):
    i = pl.program_id(1)
    halo = jnp.where(i == 0, 0.0, halo_ref[0])
    p = p_ref[0]
    ext = jnp.concatenate([halo, p], axis=0)
    s2 = ext + pltpu.roll(ext, 1, 0)
    s4 = s2 + pltpu.roll(s2, 2, 0)
    s8 = s4 + pltpu.roll(s4, 4, 0)
    s16 = s8 + pltpu.roll(s8, 8, 0)
    sums = (s2, s4, s8, s16)
    lane = lax.broadcasted_iota(jnp.int32, p.shape, 1)
    t1 = (lax.broadcasted_iota(jnp.int32, p.shape, 0) + i * POOL_ROWS + 1).astype(F32)
    mean = None
    for gi, win in enumerate(POOL_WINDOWS):
        m = sums[gi][POOL_HALO:] / jnp.minimum(t1, float(win))
        mean = m if mean is None else jnp.where(lane >= gi * POOL_GROUP, m, mean)
    dlt = (mean - p).astype(BF16)
    y = jnp.dot(dlt, w_ref[...], preferred_element_type=F32)
    o_ref[0] = ((y + b_ref[...]) * s_ref[...]).astype(o_ref.dtype)


def _pool_mixer(p, w, b, scale):
    bsz, seq, _ = p.shape
    eye = jnp.eye(len(POOL_WINDOWS), dtype=F32)
    w_bd = jnp.einsum('gcd,gk->gckd', w.astype(F32), eye).reshape(POOL_WIDTH, POOL_WIDTH).astype(BF16)
    halo_blocks = POOL_ROWS // POOL_HALO
    const = lambda bi, i: (0, 0)
    return pl.pallas_call(
        _pool_kernel,
        grid=(bsz, seq // POOL_ROWS),
        in_specs=[
            pl.BlockSpec((1, POOL_HALO, POOL_WIDTH), lambda bi, i: (bi, jnp.maximum(i * halo_blocks - 1, 0), 0)),
            pl.BlockSpec((1, POOL_ROWS, POOL_WIDTH), lambda bi, i: (bi, i, 0)),
            pl.BlockSpec((POOL_WIDTH, POOL_WIDTH), const),
            pl.BlockSpec((1, POOL_WIDTH), const),
            pl.BlockSpec((1, POOL_WIDTH), const),
        ],
        out_specs=pl.BlockSpec((1, POOL_ROWS, POOL_WIDTH), lambda bi, i: (bi, i, 0)),
        out_shape=jax.ShapeDtypeStruct((bsz, seq, POOL_WIDTH), BF16),
        compiler_params=pltpu.CompilerParams(dimension_semantics=("parallel", "parallel")),
        name="pool_mixer",
    )(p, p, w_bd, b.reshape(1, POOL_WIDTH), scale.reshape(1, POOL_WIDTH))


def _bias_table_kernel(bias_ref, tab_ref):
    h = pl.program_id(0)
    kk = lax.broadcasted_iota(jnp.int32, (MOBA_BLOCK, MOBA_BLOCK), 0)
    qq = lax.broadcasted_iota(jnp.int32, (MOBA_BLOCK, MOBA_BLOCK), 1)
    for which in range(2):
        rel = qq - kk + which * MOBA_BLOCK
        val = jnp.full((MOBA_BLOCK, MOBA_BLOCK), bias_ref[0, h], F32)
        for b in range(1, REL_BUCKETS):
            val = jnp.where(rel >= _BUCKET_START[b], bias_ref[b, h], val)
        tab_ref[0, which] = jnp.where(rel >= 0, val * LOG2E, NEG_INF)


def _bias_tables(rel_bias):
    return pl.pallas_call(
        _bias_table_kernel,
        grid=(HEADS,),
        in_specs=[pl.BlockSpec(memory_space=pltpu.SMEM)],
        out_specs=pl.BlockSpec((1, 2, MOBA_BLOCK, MOBA_BLOCK), lambda h: (h, 0, 0, 0)),
        out_shape=jax.ShapeDtypeStruct((HEADS, 2, MOBA_BLOCK, MOBA_BLOCK), F32),
        name="bias_tables",
    )(rel_bias.astype(F32))


def _attn_kernel(bias_ref, q_ref, k_ref, vt_ref, tab_ref, o_ref,
                 qh_ref, va_ref, neg_ref, acc_ref, m_ref, s_ref, ahead_ref):
    seq = k_ref.shape[1]
    nb = seq // MOBA_BLOCK
    pr = pl.program_id(1)
    qi = pl.program_id(2)

    @pl.when(qi == 0)
    def _prep():
        lane = lax.broadcasted_iota(jnp.int32, (1, HEAD_PAIR), 1)
        head_lanes = [lane < HEAD_DIM, lane >= HEAD_DIM]
        kf = k_ref[0].astype(F32).reshape(nb, MOBA_BLOCK, HEAD_PAIR)
        km = jnp.sum(kf, axis=1) * (1.0 / MOBA_BLOCK)
        q_all = q_ref[0].astype(F32)
        ones = jnp.ones((ONES_ROWS, seq), BF16)
        va_ref[0, 0:HEAD_DIM, :] = vt_ref[0, 0:HEAD_DIM, :]
        va_ref[0, HEAD_DIM:, :] = ones
        va_ref[1, 0:ONES_ROWS, :] = ones
        va_ref[1, ONES_ROWS:, :] = vt_ref[0, HEAD_DIM:, :]
        blk_id = lax.broadcasted_iota(jnp.int32, (nb, seq), 0)
        past = blk_id < lax.broadcasted_iota(jnp.int32, (nb, seq), 1) // MOBA_BLOCK
        for hh in range(2):
            qh_ref[hh] = jnp.where(head_lanes[hh], q_all, 0.0).astype(BF16)
            km_h = jnp.where(head_lanes[hh], km, 0.0).astype(BF16)
            g = lax.dot_general(km_h, qh_ref[hh], _NT, preferred_element_type=F32)
            g = jnp.where(past, g, NEG_INF)
            cnt = jnp.zeros((nb, seq), jnp.int32)
            for m in range(nb):
                gm = g[m:m + 1, :]
                cnt = cnt + ((gm > g) | ((gm == g) & (blk_id > m))).astype(jnp.int32)
            neg = jnp.where((cnt < MOBA_TOPK) & past, 0.0, NEG_INF)
            for qb in range(nb):
                neg_ref[hh, qb] = neg[:, qb * MOBA_BLOCK:(qb + 1) * MOBA_BLOCK]

    qh = [qh_ref[hh, pl.ds(pl.multiple_of(qi * MOBA_BLOCK, MOBA_BLOCK), MOBA_BLOCK), :] for hh in range(2)]

    def key_rows(j):
        return pl.ds(pl.multiple_of(j * MOBA_BLOCK, MOBA_BLOCK), MOBA_BLOCK)

    def score(hh, j):
        return lax.dot_general(k_ref[0, key_rows(j), :], qh[hh], _NT, preferred_element_type=F32)

    def attend(items, m_prev, parked=None, ahead=None):
        units = [(hh, i) for i in range(len(items)) for hh in range(2)]
        scores = {}
        if parked is None:
            for hh, i in units:
                scores[hh, i] = score(hh, items[i][0])
        ahead_units = [] if ahead is None else [(hh, i) for i in range(len(ahead[0])) for hh in range(2)]
        per_unit = -(-len(ahead_units) // len(units))
        mbs, pvs, staged = {}, {}, 0
        for n, (hh, i) in enumerate(units):
            for ahh, ai in ahead_units[n * per_unit:(n + 1) * per_unit]:
                ahead_ref[ahead[1], 2 * ai + ahh] = score(ahh, ahead[0][ai])
            add = items[i][1][hh]
            s = scores[hh, i] if parked is None else ahead_ref[parked, 2 * i + hh]
            if add.shape[0] == 1:
                smax = jnp.max(s, axis=0, keepdims=True)
                mbs[hh, i] = smax + add
                p = jnp.exp2(s - smax)
            else:
                s_ref[staged] = s
                s = s_ref[staged] + add
                staged += 1
                mbs[hh, i] = jnp.max(s, axis=0, keepdims=True)
                p = jnp.exp2(s - mbs[hh, i])
            pvs[hh, i] = jnp.dot(va_ref[hh, :, key_rows(items[i][0])], p.astype(BF16),
                                 preferred_element_type=F32)
        out = []
        for hh in range(2):
            m_new = None if m_prev is None else m_prev[hh]
            for i in range(len(items)):
                m_new = mbs[hh, i] if m_new is None else jnp.maximum(m_new, mbs[hh, i])
            acc = None if m_prev is None else acc_ref[hh] * jnp.exp2(m_prev[hh] - m_new)
            for i in range(len(items)):
                term = pvs[hh, i] * jnp.exp2(mbs[hh, i] - m_new)
                acc = term if acc is None else acc + term
            acc_ref[hh] = acc
            out.append(m_new)
        return tuple(out)

    def store_max(ms):
        m_ref[0] = ms[0]
        m_ref[1] = ms[1]

    def own_item():
        return (qi, [tab_ref[hh, 0] for hh in range(2)])

    def prev_item():
        return (qi - 1, [tab_ref[hh, 1] + neg_ref[hh, qi, pl.ds(qi - 1, 1), :] for hh in range(2)])

    def far_item(j):
        return (j, [neg_ref[hh, qi, pl.ds(j, 1), :] + bias_ref[REL_BUCKETS - 1, 2 * pr + hh] * LOG2E
                    for hh in range(2)])

    n_far = jnp.maximum(qi - 1, 0)
    n_rem = n_far % FAR_GROUP
    n_groups = n_far // FAR_GROUP

    def group_blocks(g):
        return [n_rem + g * FAR_GROUP + b for b in range(FAR_GROUP)]

    @pl.when(qi == 0)
    def _first_block():
        store_max(attend([own_item()], None))

    for r in range(FAR_GROUP):
        for follows in (False, True):
            @pl.when((qi > 0) & (n_rem == r) & ((n_groups > 0) == follows))
            def _near_blocks(r=r, follows=follows):
                items = [own_item(), prev_item()] + [far_item(j) for j in range(r)]
                store_max(attend(items, None, ahead=(group_blocks(0), 0) if follows else None))

    max_groups = (nb - 2) // FAR_GROUP
    for g in range(max_groups):
        for follows in (False, True)[:1 if g + 1 == max_groups else 2]:
            @pl.when((g < n_groups) & ((g + 1 < n_groups) == follows))
            def _far_group(g=g, follows=follows):
                items = [far_item(j) for j in group_blocks(g)]
                ahead = (group_blocks(g + 1), (g + 1) % 2) if follows else None
                store_max(attend(items, (m_ref[0], m_ref[1]), parked=g % 2, ahead=ahead))

    oa = acc_ref[0]
    ob = acc_ref[1]
    out_t = jnp.concatenate([oa[0:HEAD_DIM] / oa[HEAD_DIM:HEAD_DIM + 1],
                             ob[ONES_ROWS:] / ob[0:1]], axis=0)
    o_ref[0] = out_t.T.astype(o_ref.dtype)


def _moba_attention(q, k, vt, tabs, rel_bias):
    bsz, seq, _ = q.shape
    assert seq % MOBA_BLOCK == 0
    nb = seq // MOBA_BLOCK
    return pl.pallas_call(
        _attn_kernel,
        grid=(bsz, HEADS // 2, nb),
        in_specs=[
            pl.BlockSpec(memory_space=pltpu.SMEM),
            pl.BlockSpec((1, seq, HEAD_PAIR), lambda b, p, i: (b, 0, p)),
            pl.BlockSpec((1, seq, HEAD_PAIR), lambda b, p, i: (b, 0, p)),
            pl.BlockSpec((1, HEAD_PAIR, seq), lambda b, p, i: (b, p, 0)),
            pl.BlockSpec((2, 2, MOBA_BLOCK, MOBA_BLOCK), lambda b, p, i: (p, 0, 0, 0)),
        ],
        out_specs=pl.BlockSpec((1, MOBA_BLOCK, HEAD_PAIR), lambda b, p, i: (b, i, p)),
        out_shape=jax.ShapeDtypeStruct((bsz, seq, ATT_WIDTH), BF16),
        scratch_shapes=[
            pltpu.VMEM((2, seq, HEAD_PAIR), BF16),
            pltpu.VMEM((2, HEAD_DIM + ONES_ROWS, seq), BF16),
            pltpu.VMEM((2, nb, nb, MOBA_BLOCK), F32),
            pltpu.VMEM((2, HEAD_DIM + ONES_ROWS, MOBA_BLOCK), F32),
            pltpu.VMEM((2, 1, MOBA_BLOCK), F32),
            pltpu.VMEM((4, MOBA_BLOCK, MOBA_BLOCK), F32),
            pltpu.VMEM((2, 2 * FAR_GROUP, MOBA_BLOCK, MOBA_BLOCK), F32),
        ],
        compiler_params=pltpu.CompilerParams(
            dimension_semantics=("parallel", "parallel", "arbitrary"), vmem_limit_bytes=VMEM_LIMIT),
        name="moba_attention",
    )(rel_bias.astype(F32), q, k, vt, tabs)


def _routing_weights(logits_t):
    col = lambda i: logits_t[i:i + 1, :]
    gl = [col(g) for g in range(MOE_GROUPS)]
    gmax = _max_of(gl)
    sel, taken = [], None
    for g in range(MOE_GROUPS):
        hit = gl[g] == gmax
        if taken is not None:
            hit = hit & jnp.logical_not(taken)
        taken = hit if taken is None else (taken | hit)
        sel.append(hit)
    denom = None
    for g in range(MOE_GROUPS):
        e = jnp.exp(gl[g] - gmax)
        denom = e if denom is None else denom + e
    g_wt = 1.0 / denom
    el = []
    for e in range(MOE_PER_GROUP):
        v = col(MOE_GROUPS + (MOE_GROUPS - 1) * MOE_PER_GROUP + e)
        for g in range(MOE_GROUPS - 2, -1, -1):
            v = jnp.where(sel[g], col(MOE_GROUPS + g * MOE_PER_GROUP + e), v)
        el.append(v)
    rank = []
    for e in range(MOE_PER_GROUP):
        r = jnp.zeros_like(el[e], dtype=jnp.int32)
        for m in range(MOE_PER_GROUP):
            if m == e:
                continue
            beats = (el[m] >= el[e]) if m < e else (el[m] > el[e])
            r = r + beats.astype(jnp.int32)
        rank.append(r)
    v1 = _max_of(el)
    v2 = None
    for e in range(MOE_PER_GROUP):
        c = jnp.where(rank[e] == 1, el[e], 0.0)
        v2 = c if v2 is None else v2 + c
    e2 = jnp.exp(v2 - v1)
    w1 = 1.0 / (1.0 + e2)
    w2 = e2 / (1.0 + e2)
    tokens = logits_t.shape[1]
    row_id = lax.broadcasted_iota(jnp.int32, (MOE_EXPERTS, tokens), 0)
    comb_t = jnp.zeros((MOE_EXPERTS, tokens), F32)
    for g in range(MOE_GROUPS):
        for e in range(MOE_PER_GROUP):
            within = jnp.where(rank[e] == 0, w1, jnp.where(rank[e] == 1, w2, 0.0))
            c = jnp.where(sel[g], g_wt * within, 0.0)
            comb_t = jnp.where(row_id == g * MOE_PER_GROUP + e, c, comb_t)
    pad = jnp.zeros((ROUTER_LANES - MOE_EXPERTS, tokens), F32)
    return jnp.concatenate([comb_t, pad], axis=0).T


def _max_of(cols):
    out = cols[0]
    for c in cols[1:]:
        out = jnp.maximum(out, c)
    return out


def _moe_kernel(x_ref, ys_ref, yp_ref, ya_ref, wo_ref, g_ref, wr_ref, br_ref, wgu_ref, wd_ref, fg_ref,
                o_ref, h_ref, comb_ref, *, final_norm):
    e = pl.program_id(1)

    @pl.when(e == 0)
    def _mix_and_route():
        s0, s1 = SSM_WIDTH, SSM_WIDTH + POOL_WIDTH
        mix = jnp.dot(ys_ref[...], wo_ref[0:s0, :], preferred_element_type=F32)
        mix = mix + jnp.dot(yp_ref[...], wo_ref[s0:s1, :], preferred_element_type=F32)
        mix = mix + jnp.dot(ya_ref[...], wo_ref[s1:, :], preferred_element_type=F32)
        x = x_ref[...] + mix
        h = _rms_norm(x, g_ref[...]).astype(BF16)
        h_ref[...] = h
        logits_t = lax.dot_general(wr_ref[...], h, _NT, preferred_element_type=F32)
        comb_ref[...] = _routing_weights(logits_t + br_ref[...])
        o_ref[...] = x

    h = h_ref[...]
    gu = jnp.dot(h, wgu_ref[0], preferred_element_type=F32)
    hg, hu = gu[:, :D_EXPERT], gu[:, D_EXPERT:]
    lane = lax.broadcasted_iota(jnp.int32, comb_ref.shape, 1)
    c = jnp.sum(jnp.where(lane == e, comb_ref[...], 0.0), axis=1, keepdims=True)
    act = (hg * (1.0 / (1.0 + jnp.exp(-hg)))) * hu * c
    o_ref[...] += jnp.dot(act.astype(BF16), wd_ref[0], preferred_element_type=F32)

    if final_norm:
        @pl.when(e == MOE_EXPERTS - 1)
        def _final():
            o_ref[...] = _rms_norm(o_ref[...], fg_ref[...])


def _mix_and_moe(x2d, y_ssm_tm, y_pool, y_att, w_out, norm_g, w_router, b_router, w_gu, w_down, final_g,
                 final_norm):
    rows = x2d.shape[0]
    tiles_per_seq = y_ssm_tm.shape[0] // MOE_ROWS
    const = lambda i, e: (0, 0)
    row_blk = lambda width: pl.BlockSpec((MOE_ROWS, width), lambda i, e: (i, 0))
    return pl.pallas_call(
        partial(_moe_kernel, final_norm=final_norm),
        grid=(rows // MOE_ROWS, MOE_EXPERTS),
        in_specs=[
            row_blk(D_MODEL),
            pl.BlockSpec((MOE_ROWS, SSM_WIDTH), lambda i, e: (i % tiles_per_seq, i // tiles_per_seq)),
            row_blk(POOL_WIDTH),
            row_blk(ATT_WIDTH),
            pl.BlockSpec((D_MODEL, D_MODEL), const),
            pl.BlockSpec((1, D_MODEL), const),
            pl.BlockSpec((ROUTER_ROWS, D_MODEL), const),
            pl.BlockSpec((ROUTER_ROWS, 1), const),
            pl.BlockSpec((1, D_MODEL, 2 * D_EXPERT), lambda i, e: (e, 0, 0)),
            pl.BlockSpec((1, D_EXPERT, D_MODEL), lambda i, e: (e, 0, 0)),
            pl.BlockSpec((1, D_MODEL), const),
        ],
        out_specs=pl.BlockSpec((MOE_ROWS, D_MODEL), lambda i, e: (i, 0)),
        out_shape=jax.ShapeDtypeStruct((rows, D_MODEL), F32),
        scratch_shapes=[pltpu.VMEM((MOE_ROWS, D_MODEL), BF16), pltpu.VMEM((MOE_ROWS, ROUTER_LANES), F32)],
        compiler_params=pltpu.CompilerParams(dimension_semantics=("parallel", "arbitrary"),
                                             vmem_limit_bytes=VMEM_LIMIT),
        name="mix_and_moe",
    )(x2d, y_ssm_tm, y_pool, y_att, w_out, norm_g.reshape(1, D_MODEL), w_router, b_router, w_gu, w_down,
      final_g.reshape(1, D_MODEL))


def _router_params(group_w, group_b, router_w, router_b):
    w = jnp.concatenate([group_w.astype(F32)] + [router_w[g].astype(F32) for g in range(MOE_GROUPS)], axis=1)
    b = jnp.concatenate([group_b.astype(F32), router_b.astype(F32).reshape(-1)])
    pad = ROUTER_ROWS - w.shape[1]
    return jnp.pad(w.T, ((0, pad), (0, 0))).astype(BF16), jnp.pad(b, (0, pad)).reshape(ROUTER_ROWS, 1)


def kernel(x, rel_bias, norm1_g, w_in, ssm_a_re, ssm_a_im, ssm_log_dt, ssm_b_re, ssm_b_im, ssm_c_re, ssm_c_im,
           ssm_d, ssm_glu_w, ssm_glu_b, pool_w, pool_b, pool_scale, w_out, norm2_g, moe_group_w, moe_group_b,
           moe_router_w, moe_router_b, moe_w_gate, moe_w_up, moe_w_down, final_norm_g):
    bsz, seq, dm = x.shape
    depth = w_in.shape[0]
    n_main = SSM_WIDTH + POOL_WIDTH + 2 * ATT_WIDTH
    x2d = x.astype(F32).reshape(bsz * seq, dm)
    tabs = _bias_tables(rel_bias)
    for l in range(depth):
        w_main = w_in[l, :, :n_main].astype(BF16)
        w_vt = w_in[l, :, n_main:].T.astype(BF16)
        u, p, q, k, vt = _in_projection(x2d, norm1_g[l], w_main, w_vt, bsz, seq)

        params = _ssm_params(ssm_a_re[l], ssm_a_im[l], ssm_log_dt[l], ssm_b_re[l], ssm_b_im[l],
                             ssm_c_re[l], ssm_c_im[l])
        y_ssm = _ssm_mixer(u, params, ssm_d[l], ssm_glu_w[l], ssm_glu_b[l], bsz, seq)

        y_pool = _pool_mixer(p.reshape(bsz, seq, POOL_WIDTH), pool_w[l], pool_b[l], pool_scale[l])
        y_att = _moba_attention(q.reshape(bsz, seq, ATT_WIDTH), k.reshape(bsz, seq, ATT_WIDTH), vt, tabs, rel_bias)

        w_router, b_router = _router_params(moe_group_w[l], moe_group_b[l], moe_router_w[l], moe_router_b[l])
        w_gu = jnp.concatenate([moe_w_gate[l], moe_w_up[l]], axis=-1).astype(BF16)
        x2d = _mix_and_moe(x2d, y_ssm, y_pool.reshape(bsz * seq, POOL_WIDTH), y_att.reshape(bsz * seq, ATT_WIDTH),
                           w_out[l].astype(BF16), norm2_g[l], w_router, b_router, w_gu,
                           moe_w_down[l].astype(BF16), final_norm_g, final_norm=(l == depth - 1))
    return x2d.reshape(bsz, seq, dm).astype(x.dtype)
```

```python
import math
from functools import partial

import jax
import jax.numpy as jnp
from jax import lax
from jax.experimental import pallas as pl
from jax.experimental.pallas import tpu as pltpu

F32 = jnp.float32
BF16 = jnp.bfloat16

LANES = 128
D_MODEL = 1024
SSM_WIDTH = 256
POOL_WIDTH = 256
ATT_WIDTH = 512
SSM_GROUP = 16
SSM_GROUPS = SSM_WIDTH // SSM_GROUP
SSM_STATE = 64
SSM_STATES = SSM_GROUPS * SSM_STATE
POOL_WINDOWS = (2, 4, 8, 16)
POOL_GROUP = POOL_WIDTH // len(POOL_WINDOWS)
POOL_HALO = 16
HEAD_DIM = 64
HEADS = ATT_WIDTH // HEAD_DIM
HEAD_PAIR = 2 * HEAD_DIM
ONES_ROWS = 16
MOBA_BLOCK = 256
MOBA_TOPK = 3
FAR_GROUP = 4
REL_BUCKETS = 32
REL_MAX_EXACT = REL_BUCKETS // 2
REL_MAX_DIST = 128
MOE_GROUPS = 4
MOE_PER_GROUP = 4
MOE_EXPERTS = MOE_GROUPS * MOE_PER_GROUP
D_EXPERT = D_MODEL // 4
ROUTER_ROWS = 32
ROUTER_LANES = 128
RMS_EPS = 1e-6
NEG_INF = -1e30
LOG2E = math.log2(math.e)

PROJ_ROWS = 512
MOE_ROWS = 1024
EXPERTS_PER_STEP = 4
POOL_ROWS = 512
SSM_STEPS = 128
VMEM_LIMIT = 48 * 1024 * 1024

_NT = (((1,), (1,)), ((), ()))

_BUCKET_START = list(range(REL_MAX_EXACT)) + [
    math.ceil(REL_MAX_EXACT * (REL_MAX_DIST / REL_MAX_EXACT) ** (k / (REL_BUCKETS - REL_MAX_EXACT)))
    for k in range(REL_BUCKETS - REL_MAX_EXACT)
]


def _rms_norm(x, g):
    return x * lax.rsqrt(jnp.mean(x * x, axis=-1, keepdims=True) + RMS_EPS) * g


def _inproj_kernel(x_ref, g_ref, w_ref, wvt_ref, u_ref, p_ref, q_ref, k_ref, vt_ref):
    h = _rms_norm(x_ref[...], g_ref[...]).astype(BF16)
    pr = jnp.dot(h, w_ref[...], preferred_element_type=F32)
    s0, s1, s2 = SSM_WIDTH, SSM_WIDTH + POOL_WIDTH, SSM_WIDTH + POOL_WIDTH + ATT_WIDTH
    u_ref[...] = pr[:, :s0]
    p_ref[...] = pr[:, s0:s1]
    q_ref[...] = (pr[:, s1:s2] * (HEAD_DIM ** -0.5 * LOG2E)).astype(BF16)
    k_ref[...] = pr[:, s2:].astype(BF16)
    vt_ref[0] = lax.dot_general(wvt_ref[...], h, _NT, preferred_element_type=F32).astype(BF16)


def _in_projection(x2d, g, w_main, w_vt, bsz, seq):
    rows = x2d.shape[0]
    tiles_per_seq = seq // PROJ_ROWS
    n_main = w_main.shape[1]
    const = lambda i: (0, 0)
    row_blk = lambda width: pl.BlockSpec((PROJ_ROWS, width), lambda i: (i, 0))
    return pl.pallas_call(
        _inproj_kernel,
        grid=(rows // PROJ_ROWS,),
        in_specs=[row_blk(D_MODEL), pl.BlockSpec((1, D_MODEL), const),
                  pl.BlockSpec((D_MODEL, n_main), const), pl.BlockSpec((ATT_WIDTH, D_MODEL), const)],
        out_specs=[pl.BlockSpec((PROJ_ROWS, SSM_WIDTH), lambda i: (i % tiles_per_seq, i // tiles_per_seq)),
                   row_blk(POOL_WIDTH), row_blk(ATT_WIDTH), row_blk(ATT_WIDTH),
                   pl.BlockSpec((1, ATT_WIDTH, PROJ_ROWS),
                                lambda i: (i // tiles_per_seq, 0, i % tiles_per_seq))],
        out_shape=[jax.ShapeDtypeStruct((seq, bsz * SSM_WIDTH), F32),
                   jax.ShapeDtypeStruct((rows, POOL_WIDTH), F32),
                   jax.ShapeDtypeStruct((rows, ATT_WIDTH), BF16),
                   jax.ShapeDtypeStruct((rows, ATT_WIDTH), BF16),
                   jax.ShapeDtypeStruct((bsz, ATT_WIDTH, seq), BF16)],
        compiler_params=pltpu.CompilerParams(dimension_semantics=("parallel",),
                                             vmem_limit_bytes=VMEM_LIMIT),
        name="in_projection",
    )(x2d, g.reshape(1, D_MODEL), w_main, w_vt)


def _ssm_kernel(u_ref, bm_ref, cm_ref, ar_ref, ai_ref, d_ref, gw_ref, gb_ref, o_ref, bu_ref, st_ref, tb_ref):
    bsz = st_ref.shape[0]
    ns = SSM_STATES

    @pl.when(pl.program_id(0) == 0)
    def _init():
        st_ref[...] = jnp.zeros_like(st_ref)

    halves = SSM_WIDTH // LANES
    for b in range(bsz):
        for h in range(halves):
            c0 = b * SSM_WIDTH + h * LANES
            tb_ref[h, pl.ds(b, SSM_STEPS, stride=bsz), :] = u_ref[:, c0:c0 + LANES]
    u = jnp.concatenate([tb_ref[h] for h in range(halves)], axis=1)
    bu_ref[...] = jnp.dot(u.astype(BF16), bm_ref[...], preferred_element_type=F32)
    ar = jnp.broadcast_to(ar_ref[...], (bsz, ns))
    ai = jnp.broadcast_to(ai_ref[...], (bsz, ns))

    def step(t, carry):
        sr, si = carry
        r0 = pl.multiple_of(t * bsz, bsz)
        nr = ar * sr - ai * si + bu_ref[pl.ds(r0, bsz), 0:ns]
        ni = ar * si + ai * sr + bu_ref[pl.ds(r0, bsz), ns:2 * ns]
        bu_ref[pl.ds(r0, bsz), 0:ns] = nr
        bu_ref[pl.ds(r0, bsz), ns:2 * ns] = ni
        return nr, ni

    sr, si = lax.fori_loop(0, SSM_STEPS, step, (st_ref[:, 0:ns], st_ref[:, ns:2 * ns]), unroll=4)
    st_ref[:, 0:ns] = sr
    st_ref[:, ns:2 * ns] = si

    y = jnp.dot(bu_ref[...].astype(BF16), cm_ref[...], preferred_element_type=F32)
    y = y + d_ref[...] * u
    y = y * (0.5 * (1.0 + jnp.tanh(math.sqrt(2.0 / math.pi) * (y + 0.044715 * (y * y * y)))))
    z = jnp.dot(y.astype(BF16), gw_ref[...], preferred_element_type=F32) + gb_ref[...]
    out = y * (1.0 / (1.0 + jnp.exp(-z)))
    for h in range(halves):
        tb_ref[h] = out[:, h * LANES:(h + 1) * LANES]
    for b in range(bsz):
        for h in range(halves):
            c0 = b * SSM_WIDTH + h * LANES
            o_ref[:, c0:c0 + LANES] = tb_ref[h, pl.ds(b, SSM_STEPS, stride=bsz), :].astype(o_ref.dtype)


def _ssm_params(a_re, a_im, log_dt, b_re, b_im, c_re, c_im):
    lam = lax.complex(a_re.astype(F32), a_im.astype(F32))
    dt = jnp.exp(log_dt.astype(F32))[:, None]
    lam_bar = jnp.exp(lam * dt)
    b = lax.complex(b_re.astype(F32), b_im.astype(F32))
    b_bar = ((lam_bar - 1.0) / lam)[..., None] * b
    eye = jnp.eye(SSM_GROUPS, dtype=F32)

    def bdiag_in(m):
        return jnp.einsum('gph,gk->ghkp', m, eye).reshape(SSM_WIDTH, SSM_STATES)

    def bdiag_out(m):
        return jnp.einsum('ghp,gk->gpkh', m, eye).reshape(SSM_STATES, SSM_WIDTH)

    bm = jnp.concatenate([bdiag_in(jnp.real(b_bar)), bdiag_in(jnp.imag(b_bar))], axis=1)
    cm = jnp.concatenate([bdiag_out(c_re.astype(F32)), -bdiag_out(c_im.astype(F32))], axis=0)
    return (bm.astype(BF16), cm.astype(BF16),
            jnp.real(lam_bar).reshape(1, SSM_STATES), jnp.imag(lam_bar).reshape(1, SSM_STATES))


def _ssm_mixer(u_tm, params, d, glu_w, glu_b, bsz, seq):
    bm, cm, ar, ai = params
    rows = SSM_STEPS * bsz
    const = lambda c: (0, 0)
    return pl.pallas_call(
        _ssm_kernel,
        grid=(seq // SSM_STEPS,),
        in_specs=[
            pl.BlockSpec((SSM_STEPS, bsz * SSM_WIDTH), lambda c: (c, 0)),
            pl.BlockSpec((SSM_WIDTH, 2 * SSM_STATES), const),
            pl.BlockSpec((2 * SSM_STATES, SSM_WIDTH), const),
            pl.BlockSpec((1, SSM_STATES), const),
            pl.BlockSpec((1, SSM_STATES), const),
            pl.BlockSpec((1, SSM_WIDTH), const),
            pl.BlockSpec((SSM_WIDTH, SSM_WIDTH), const),
            pl.BlockSpec((1, SSM_WIDTH), const),
        ],
        out_specs=pl.BlockSpec((SSM_STEPS, bsz * SSM_WIDTH), lambda c: (c, 0)),
        out_shape=jax.ShapeDtypeStruct((seq, bsz * SSM_WIDTH), BF16),
        scratch_shapes=[pltpu.VMEM((rows, 2 * SSM_STATES), F32), pltpu.VMEM((bsz, 2 * SSM_STATES), F32),
                        pltpu.VMEM((SSM_WIDTH // LANES, rows, LANES), F32)],
        compiler_params=pltpu.CompilerParams(dimension_semantics=("arbitrary",),
                                             vmem_limit_bytes=VMEM_LIMIT),
        name="ssm_mixer",
    )(u_tm, bm, cm, ar, ai, d.reshape(1, SSM_WIDTH), glu_w.astype(BF16), glu_b.reshape(1, SSM_WIDTH))


def _pool_kernel(halo_ref, p_ref, w_ref, b_ref, s_ref, o_ref):
    i = pl.program_id(1)
    halo = jnp.where(i == 0, 0.0, halo_ref[0])
    p = p_ref[0]
    ext = jnp.concatenate([halo, p], axis=0)
    s2 = ext + pltpu.roll(ext, 1, 0)
    s4 = s2 + pltpu.roll(s2, 2, 0)
    s8 = s4 + pltpu.roll(s4, 4, 0)
    s16 = s8 + pltpu.roll(s8, 8, 0)
    sums = (s2, s4, s8, s16)
    lane = lax.broadcasted_iota(jnp.int32, p.shape, 1)
    t1 = (lax.broadcasted_iota(jnp.int32, p.shape, 0) + i * POOL_ROWS + 1).astype(F32)
    mean = None
    for gi, win in enumerate(POOL_WINDOWS):
        m = sums[gi][POOL_HALO:] / jnp.minimum(t1, float(win))
        mean = m if mean is None else jnp.where(lane >= gi * POOL_GROUP, m, mean)
    dlt = (mean - p).astype(BF16)
    y = jnp.dot(dlt, w_ref[...], preferred_element_type=F32)
    o_ref[0] = ((y + b_ref[...]) * s_ref[...]).astype(o_ref.dtype)


def _pool_mixer(p, w, b, scale):
    bsz, seq, _ = p.shape
    eye = jnp.eye(len(POOL_WINDOWS), dtype=F32)
    w_bd = jnp.einsum('gcd,gk->gckd', w.astype(F32), eye).reshape(POOL_WIDTH, POOL_WIDTH).astype(BF16)
    halo_blocks = POOL_ROWS // POOL_HALO
    const = lambda bi, i: (0, 0)
    return pl.pallas_call(
        _pool_kernel,
        grid=(bsz, seq // POOL_ROWS),
        in_specs=[
            pl.BlockSpec((1, POOL_HALO, POOL_WIDTH), lambda bi, i: (bi, jnp.maximum(i * halo_blocks - 1, 0), 0)),
            pl.BlockSpec((1, POOL_ROWS, POOL_WIDTH), lambda bi, i: (bi, i, 0)),
            pl.BlockSpec((POOL_WIDTH, POOL_WIDTH), const),
            pl.BlockSpec((1, POOL_WIDTH), const),
            pl.BlockSpec((1, POOL_WIDTH), const),
        ],
        out_specs=pl.BlockSpec((1, POOL_ROWS, POOL_WIDTH), lambda bi, i: (bi, i, 0)),
        out_shape=jax.ShapeDtypeStruct((bsz, seq, POOL_WIDTH), BF16),
        compiler_params=pltpu.CompilerParams(dimension_semantics=("parallel", "parallel")),
        name="pool_mixer",
    )(p, p, w_bd, b.reshape(1, POOL_WIDTH), scale.reshape(1, POOL_WIDTH))


def _bias_table_kernel(bias_ref, tab_ref):
    h = pl.program_id(0)
    kk = lax.broadcasted_iota(jnp.int32, (MOBA_BLOCK, MOBA_BLOCK), 0)
    qq = lax.broadcasted_iota(jnp.int32, (MOBA_BLOCK, MOBA_BLOCK), 1)
    for which in range(2):
        rel = qq - kk + which * MOBA_BLOCK
        val = jnp.full((MOBA_BLOCK, MOBA_BLOCK), bias_ref[0, h], F32)
        for b in range(1, REL_BUCKETS):
            val = jnp.where(rel >= _BUCKET_START[b], bias_ref[b, h], val)
        tab_ref[0, which] = jnp.where(rel >= 0, val * LOG2E, NEG_INF)


def _bias_tables(rel_bias):
    return pl.pallas_call(
        _bias_table_kernel,
        grid=(HEADS,),
        in_specs=[pl.BlockSpec(memory_space=pltpu.SMEM)],
        out_specs=pl.BlockSpec((1, 2, MOBA_BLOCK, MOBA_BLOCK), lambda h: (h, 0, 0, 0)),
        out_shape=jax.ShapeDtypeStruct((HEADS, 2, MOBA_BLOCK, MOBA_BLOCK), F32),
        name="bias_tables",
    )(rel_bias.astype(F32))


def _attn_kernel(bias_ref, q_ref, k_ref, vt_ref, tab_ref, o_ref,
                 qh_ref, va_ref, neg_ref, acc_ref, m_ref, s_ref, ahead_ref):
    seq = k_ref.shape[1]
    nb = seq // MOBA_BLOCK
    pr = pl.program_id(1)
    qi = pl.program_id(2)

    @pl.when(qi == 0)
    def _prep():
        lane = lax.broadcasted_iota(jnp.int32, (1, HEAD_PAIR), 1)
        head_lanes = [lane < HEAD_DIM, lane >= HEAD_DIM]
        kf = k_ref[0].astype(F32).reshape(nb, MOBA_BLOCK, HEAD_PAIR)
        km = jnp.sum(kf, axis=1) * (1.0 / MOBA_BLOCK)
        q_all = q_ref[0].astype(F32)
        ones = jnp.ones((ONES_ROWS, seq), BF16)
        va_ref[0, 0:HEAD_DIM, :] = vt_ref[0, 0:HEAD_DIM, :]
        va_ref[0, HEAD_DIM:, :] = ones
        va_ref[1, 0:ONES_ROWS, :] = ones
        va_ref[1, ONES_ROWS:, :] = vt_ref[0, HEAD_DIM:, :]
        blk_id = lax.broadcasted_iota(jnp.int32, (nb, seq), 0)
        past = blk_id < lax.broadcasted_iota(jnp.int32, (nb, seq), 1) // MOBA_BLOCK
        for hh in range(2):
            qh_ref[hh] = jnp.where(head_lanes[hh], q_all, 0.0).astype(BF16)
            km_h = jnp.where(head_lanes[hh], km, 0.0).astype(BF16)
            g = lax.dot_general(km_h, qh_ref[hh], _NT, preferred_element_type=F32)
            g = jnp.where(past, g, NEG_INF)
            cnt = jnp.zeros((nb, seq), jnp.int32)
            for m in range(nb):
                gm = g[m:m + 1, :]
                cnt = cnt + ((gm > g) | ((gm == g) & (blk_id > m))).astype(jnp.int32)
            neg = jnp.where((cnt < MOBA_TOPK) & past, 0.0, NEG_INF)
            for qb in range(nb):
                neg_ref[hh, qb] = neg[:, qb * MOBA_BLOCK:(qb + 1) * MOBA_BLOCK]

    qh = [qh_ref[hh, pl.ds(pl.multiple_of(qi * MOBA_BLOCK, MOBA_BLOCK), MOBA_BLOCK), :] for hh in range(2)]

    def key_rows(j):
        return pl.ds(pl.multiple_of(j * MOBA_BLOCK, MOBA_BLOCK), MOBA_BLOCK)

    def score(hh, j):
        return lax.dot_general(k_ref[0, key_rows(j), :], qh[hh], _NT, preferred_element_type=F32)

    def attend(items, m_prev, parked=None, ahead=None):
        units = [(hh, i) for i in range(len(items)) for hh in range(2)]
        scores = {}
        if parked is None:
            for hh, i in units:
                scores[hh, i] = score(hh, items[i][0])
        ahead_units = [] if ahead is None else [(hh, i) for i in range(len(ahead[0])) for hh in range(2)]
        per_unit = -(-len(ahead_units) // len(units))
        mbs, pvs, staged = {}, {}, 0
        for n, (hh, i) in enumerate(units):
            for ahh, ai in ahead_units[n * per_unit:(n + 1) * per_unit]:
                ahead_ref[ahead[1], 2 * ai + ahh] = score(ahh, ahead[0][ai])
            add = items[i][1][hh]
            s = scores[hh, i] if parked is None else ahead_ref[parked, 2 * i + hh]
            if add.shape[0] == 1:
                smax = jnp.max(s, axis=0, keepdims=True)
                mbs[hh, i] = smax + add
                p = jnp.exp2(s - smax)
            else:
                s_ref[staged] = s
                s = s_ref[staged] + add
                staged += 1
                mbs[hh, i] = jnp.max(s, axis=0, keepdims=True)
                p = jnp.exp2(s - mbs[hh, i])
            pvs[hh, i] = jnp.dot(va_ref[hh, :, key_rows(items[i][0])], p.astype(BF16),
                                 preferred_element_type=F32)
        out = []
        for hh in range(2):
            m_new = None if m_prev is None else m_prev[hh]
            for i in range(len(items)):
                m_new = mbs[hh, i] if m_new is None else jnp.maximum(m_new, mbs[hh, i])
            acc = None if m_prev is None else acc_ref[hh] * jnp.exp2(m_prev[hh] - m_new)
            for i in range(len(items)):
                term = pvs[hh, i] * jnp.exp2(mbs[hh, i] - m_new)
                acc = term if acc is None else acc + term
            acc_ref[hh] = acc
            out.append(m_new)
        return tuple(out)

    def store_max(ms):
        m_ref[0] = ms[0]
        m_ref[1] = ms[1]

    def own_item():
        return (qi, [tab_ref[hh, 0] for hh in range(2)])

    def prev_item():
        return (qi - 1, [tab_ref[hh, 1] + neg_ref[hh, qi, pl.ds(qi - 1, 1), :] for hh in range(2)])

    def far_item(j):
        return (j, [neg_ref[hh, qi, pl.ds(j, 1), :] + bias_ref[REL_BUCKETS - 1, 2 * pr + hh] * LOG2E
                    for hh in range(2)])

    n_far = jnp.maximum(qi - 1, 0)
    n_rem = n_far % FAR_GROUP
    n_groups = n_far // FAR_GROUP

    def group_blocks(g):
        return [n_rem + g * FAR_GROUP + b for b in range(FAR_GROUP)]

    @pl.when(qi == 0)
    def _first_block():
        store_max(attend([own_item()], None))

    for r in range(FAR_GROUP):
        for follows in (False, True):
            @pl.when((qi > 0) & (n_rem == r) & ((n_groups > 0) == follows))
            def _near_blocks(r=r, follows=follows):
                items = [own_item(), prev_item()] + [far_item(j) for j in range(r)]
                store_max(attend(items, None, ahead=(group_blocks(0), 0) if follows else None))

    max_groups = (nb - 2) // FAR_GROUP
    for g in range(max_groups):
        for follows in (False, True)[:1 if g + 1 == max_groups else 2]:
            @pl.when((g < n_groups) & ((g + 1 < n_groups) == follows))
            def _far_group(g=g, follows=follows):
                items = [far_item(j) for j in group_blocks(g)]
                ahead = (group_blocks(g + 1), (g + 1) % 2) if follows else None
                store_max(attend(items, (m_ref[0], m_ref[1]), parked=g % 2, ahead=ahead))

    oa = acc_ref[0]
    ob = acc_ref[1]
    out_t = jnp.concatenate([oa[0:HEAD_DIM] / oa[HEAD_DIM:HEAD_DIM + 1],
                             ob[ONES_ROWS:] / ob[0:1]], axis=0)
    o_ref[0] = out_t.T.astype(o_ref.dtype)


def _moba_attention(q, k, vt, tabs, rel_bias):
    bsz, seq, _ = q.shape
    assert seq % MOBA_BLOCK == 0
    nb = seq // MOBA_BLOCK
    return pl.pallas_call(
        _attn_kernel,
        grid=(bsz, HEADS // 2, nb),
        in_specs=[
            pl.BlockSpec(memory_space=pltpu.SMEM),
            pl.BlockSpec((1, seq, HEAD_PAIR), lambda b, p, i: (b, 0, p)),
            pl.BlockSpec((1, seq, HEAD_PAIR), lambda b, p, i: (b, 0, p)),
            pl.BlockSpec((1, HEAD_PAIR, seq), lambda b, p, i: (b, p, 0)),
            pl.BlockSpec((2, 2, MOBA_BLOCK, MOBA_BLOCK), lambda b, p, i: (p, 0, 0, 0)),
        ],
        out_specs=pl.BlockSpec((1, MOBA_BLOCK, HEAD_PAIR), lambda b, p, i: (b, i, p)),
        out_shape=jax.ShapeDtypeStruct((bsz, seq, ATT_WIDTH), BF16),
        scratch_shapes=[
            pltpu.VMEM((2, seq, HEAD_PAIR), BF16),
            pltpu.VMEM((2, HEAD_DIM + ONES_ROWS, seq), BF16),
            pltpu.VMEM((2, nb, nb, MOBA_BLOCK), F32),
            pltpu.VMEM((2, HEAD_DIM + ONES_ROWS, MOBA_BLOCK), F32),
            pltpu.VMEM((2, 1, MOBA_BLOCK), F32),
            pltpu.VMEM((4, MOBA_BLOCK, MOBA_BLOCK), F32),
            pltpu.VMEM((2, 2 * FAR_GROUP, MOBA_BLOCK, MOBA_BLOCK), F32),
        ],
        compiler_params=pltpu.CompilerParams(
            dimension_semantics=("parallel", "parallel", "arbitrary"), vmem_limit_bytes=VMEM_LIMIT),
        name="moba_attention",
    )(rel_bias.astype(F32), q, k, vt, tabs)


def _routing_weights(logits_t):
    col = lambda i: logits_t[i:i + 1, :]
    gl = [col(g) for g in range(MOE_GROUPS)]
    gmax = _max_of(gl)
    sel, taken = [], None
    for g in range(MOE_GROUPS):
        hit = gl[g] == gmax
        if taken is not None:
            hit = hit & jnp.logical_not(taken)
        taken = hit if taken is None else (taken | hit)
        sel.append(hit)
    denom = None
    for g in range(MOE_GROUPS):
        e = jnp.exp(gl[g] - gmax)
        denom = e if denom is None else denom + e
    g_wt = 1.0 / denom
    el = []
    for e in range(MOE_PER_GROUP):
        v = col(MOE_GROUPS + (MOE_GROUPS - 1) * MOE_PER_GROUP + e)
        for g in range(MOE_GROUPS - 2, -1, -1):
            v = jnp.where(sel[g], col(MOE_GROUPS + g * MOE_PER_GROUP + e), v)
        el.append(v)
    rank = []
    for e in range(MOE_PER_GROUP):
        r = jnp.zeros_like(el[e], dtype=jnp.int32)
        for m in range(MOE_PER_GROUP):
            if m == e:
                continue
            beats = (el[m] >= el[e]) if m < e else (el[m] > el[e])
            r = r + beats.astype(jnp.int32)
        rank.append(r)
    v1 = _max_of(el)
    v2 = None
    for e in range(MOE_PER_GROUP):
        c = jnp.where(rank[e] == 1, el[e], 0.0)
        v2 = c if v2 is None else v2 + c
    e2 = jnp.exp(v2 - v1)
    w1 = 1.0 / (1.0 + e2)
    w2 = e2 / (1.0 + e2)
    tokens = logits_t.shape[1]
    row_id = lax.broadcasted_iota(jnp.int32, (MOE_EXPERTS, tokens), 0)
    comb_t = jnp.zeros((MOE_EXPERTS, tokens), F32)
    for g in range(MOE_GROUPS):
        for e in range(MOE_PER_GROUP):
            within = jnp.where(rank[e] == 0, w1, jnp.where(rank[e] == 1, w2, 0.0))
            c = jnp.where(sel[g], g_wt * within, 0.0)
            comb_t = jnp.where(row_id == g * MOE_PER_GROUP + e, c, comb_t)
    pad = jnp.zeros((ROUTER_LANES - MOE_EXPERTS, tokens), F32)
    return jnp.concatenate([comb_t, pad], axis=0).T


def _max_of(cols):
    out = cols[0]
    for c in cols[1:]:
        out = jnp.maximum(out, c)
    return out


def _moe_kernel(x_ref, ys_ref, yp_ref, ya_ref, wo_ref, g_ref, wr_ref, br_ref, wgu_ref, wd_ref, fg_ref,
                o_ref, h_ref, comb_ref, *, final_norm):
    step = pl.program_id(1)

    @pl.when(step == 0)
    def _mix_and_route():
        s0, s1 = SSM_WIDTH, SSM_WIDTH + POOL_WIDTH
        mix = jnp.dot(ys_ref[...], wo_ref[0:s0, :], preferred_element_type=F32)
        mix = mix + jnp.dot(yp_ref[...], wo_ref[s0:s1, :], preferred_element_type=F32)
        mix = mix + jnp.dot(ya_ref[...], wo_ref[s1:, :], preferred_element_type=F32)
        x = x_ref[...] + mix
        h = _rms_norm(x, g_ref[...]).astype(BF16)
        h_ref[...] = h
        logits_t = lax.dot_general(wr_ref[...], h, _NT, preferred_element_type=F32)
        comb_ref[...] = _routing_weights(logits_t + br_ref[...])
        o_ref[...] = x

    h = h_ref[...]
    lane = lax.broadcasted_iota(jnp.int32, comb_ref.shape, 1)
    y = None
    for j in range(EXPERTS_PER_STEP):
        gu = jnp.dot(h, wgu_ref[j], preferred_element_type=F32)
        hg, hu = gu[:, :D_EXPERT], gu[:, D_EXPERT:]
        c = jnp.sum(jnp.where(lane == step * EXPERTS_PER_STEP + j, comb_ref[...], 0.0), axis=1, keepdims=True)
        act = (hg * (1.0 / (1.0 + jnp.exp(-hg)))) * hu * c
        yj = jnp.dot(act.astype(BF16), wd_ref[j], preferred_element_type=F32)
        y = yj if y is None else y + yj
    o_ref[...] += y

    if final_norm:
        @pl.when(step == MOE_EXPERTS // EXPERTS_PER_STEP - 1)
        def _final():
            o_ref[...] = _rms_norm(o_ref[...], fg_ref[...])


def _mix_and_moe(x2d, y_ssm_tm, y_pool, y_att, w_out, norm_g, w_router, b_router, w_gu, w_down, final_g,
                 final_norm):
    rows = x2d.shape[0]
    tiles_per_seq = y_ssm_tm.shape[0] // MOE_ROWS
    const = lambda i, e: (0, 0)
    row_blk = lambda width: pl.BlockSpec((MOE_ROWS, width), lambda i, e: (i, 0))
    return pl.pallas_call(
        partial(_moe_kernel, final_norm=final_norm),
        grid=(rows // MOE_ROWS, MOE_EXPERTS // EXPERTS_PER_STEP),
        in_specs=[
            row_blk(D_MODEL),
            pl.BlockSpec((MOE_ROWS, SSM_WIDTH), lambda i, e: (i % tiles_per_seq, i // tiles_per_seq)),
            row_blk(POOL_WIDTH),
            row_blk(ATT_WIDTH),
            pl.BlockSpec((D_MODEL, D_MODEL), const),
            pl.BlockSpec((1, D_MODEL), const),
            pl.BlockSpec((ROUTER_ROWS, D_MODEL), const),
            pl.BlockSpec((ROUTER_ROWS, 1), const),
            pl.BlockSpec((EXPERTS_PER_STEP, D_MODEL, 2 * D_EXPERT), lambda i, e: (e, 0, 0)),
            pl.BlockSpec((EXPERTS_PER_STEP, D_EXPERT, D_MODEL), lambda i, e: (e, 0, 0)),
            pl.BlockSpec((1, D_MODEL), const),
        ],
        out_specs=pl.BlockSpec((MOE_ROWS, D_MODEL), lambda i, e: (i, 0)),
        out_shape=jax.ShapeDtypeStruct((rows, D_MODEL), F32),
        scratch_shapes=[pltpu.VMEM((MOE_ROWS, D_MODEL), BF16), pltpu.VMEM((MOE_ROWS, ROUTER_LANES), F32)],
        compiler_params=pltpu.CompilerParams(dimension_semantics=("parallel", "arbitrary"),
                                             vmem_limit_bytes=VMEM_LIMIT),
        name="mix_and_moe",
    )(x2d, y_ssm_tm, y_pool, y_att, w_out, norm_g.reshape(1, D_MODEL), w_router, b_router, w_gu, w_down,
      final_g.reshape(1, D_MODEL))


def _router_params(group_w, group_b, router_w, router_b):
    w = jnp.concatenate([group_w.astype(F32)] + [router_w[g].astype(F32) for g in range(MOE_GROUPS)], axis=1)
    b = jnp.concatenate([group_b.astype(F32), router_b.astype(F32).reshape(-1)])
    pad = ROUTER_ROWS - w.shape[1]
    return jnp.pad(w.T, ((0, pad), (0, 0))).astype(BF16), jnp.pad(b, (0, pad)).reshape(ROUTER_ROWS, 1)


def kernel(x, rel_bias, norm1_g, w_in, ssm_a_re, ssm_a_im, ssm_log_dt, ssm_b_re, ssm_b_im, ssm_c_re, ssm_c_im,
           ssm_d, ssm_glu_w, ssm_glu_b, pool_w, pool_b, pool_scale, w_out, norm2_g, moe_group_w, moe_group_b,
           moe_router_w, moe_router_b, moe_w_gate, moe_w_up, moe_w_down, final_norm_g):
    bsz, seq, dm = x.shape
    depth = w_in.shape[0]
    n_main = SSM_WIDTH + POOL_WIDTH + 2 * ATT_WIDTH
    x2d = x.astype(F32).reshape(bsz * seq, dm)
    tabs = _bias_tables(rel_bias)
    for l in range(depth):
        w_main = w_in[l, :, :n_main].astype(BF16)
        w_vt = w_in[l, :, n_main:].T.astype(BF16)
        u, p, q, k, vt = _in_projection(x2d, norm1_g[l], w_main, w_vt, bsz, seq)

        params = _ssm_params(ssm_a_re[l], ssm_a_im[l], ssm_log_dt[l], ssm_b_re[l], ssm_b_im[l],
                             ssm_c_re[l], ssm_c_im[l])
        y_ssm = _ssm_mixer(u, params, ssm_d[l], ssm_glu_w[l], ssm_glu_b[l], bsz, seq)

        y_pool = _pool_mixer(p.reshape(bsz, seq, POOL_WIDTH), pool_w[l], pool_b[l], pool_scale[l])
        y_att = _moba_attention(q.reshape(bsz, seq, ATT_WIDTH), k.reshape(bsz, seq, ATT_WIDTH), vt, tabs, rel_bias)

        w_router, b_router = _router_params(moe_group_w[l], moe_group_b[l], moe_router_w[l], moe_router_b[l])
        w_gu = jnp.concatenate([moe_w_gate[l], moe_w_up[l]], axis=-1).astype(BF16)
        x2d = _mix_and_moe(x2d, y_ssm, y_pool.reshape(bsz * seq, POOL_WIDTH), y_att.reshape(bsz * seq, ATT_WIDTH),
                           w_out[l].astype(BF16), norm2_g[l], w_router, b_router, w_gu,
                           moe_w_down[l].astype(BF16), final_norm_g, final_norm=(l == depth - 1))
    return x2d.reshape(bsz, seq, dm).astype(x.dtype)
```

```python
import math
from functools import partial

import jax
import jax.numpy as jnp
from jax import lax
from jax.experimental import pallas as pl
from jax.experimental.pallas import tpu as pltpu

F32 = jnp.float32
BF16 = jnp.bfloat16

LANES = 128
SUBLANES = 8
D_MODEL = 1024
SSM_WIDTH = 256
POOL_WIDTH = 256
ATT_WIDTH = 512
SSM_GROUP = 16
SSM_GROUPS = SSM_WIDTH // SSM_GROUP
SSM_STATE = 64
SSM_STATES = SSM_GROUPS * SSM_STATE
POOL_WINDOWS = (2, 4, 8, 16)
POOL_GROUP = POOL_WIDTH // len(POOL_WINDOWS)
POOL_HALO = 16
HEAD_DIM = 64
HEADS = ATT_WIDTH // HEAD_DIM
HEAD_PAIR = 2 * HEAD_DIM
ONES_ROWS = 16
MOBA_BLOCK = 256
MOBA_TOPK = 3
FAR_GROUP = 4
REL_BUCKETS = 32
REL_MAX_EXACT = REL_BUCKETS // 2
REL_MAX_DIST = 128
MOE_GROUPS = 4
MOE_PER_GROUP = 4
MOE_EXPERTS = MOE_GROUPS * MOE_PER_GROUP
D_EXPERT = D_MODEL // 4
ROUTER_ROWS = 32
ROUTER_LANES = 128
RMS_EPS = 1e-6
NEG_INF = -1e30
LOG2E = math.log2(math.e)

PROJ_ROWS = 1024
MOE_ROWS = 1024
EXPERTS_PER_STEP = 4
POOL_ROWS = 2048
SSM_STEPS = 128
VMEM_LIMIT = 48 * 1024 * 1024

_NT = (((1,), (1,)), ((), ()))

_BUCKET_START = list(range(REL_MAX_EXACT)) + [
    math.ceil(REL_MAX_EXACT * (REL_MAX_DIST / REL_MAX_EXACT) ** (k / (REL_BUCKETS - REL_MAX_EXACT)))
    for k in range(REL_BUCKETS - REL_MAX_EXACT)
]


def _rms_norm(x, g):
    return x * lax.rsqrt(jnp.mean(x * x, axis=-1, keepdims=True) + RMS_EPS) * g


def _inproj_kernel(x_ref, g_ref, w_ref, wvt_ref, u_ref, p_ref, q_ref, k_ref, vt_ref):
    h = _rms_norm(x_ref[...], g_ref[...]).astype(BF16)
    pr = jnp.dot(h, w_ref[...], preferred_element_type=F32)
    s0, s1, s2 = SSM_WIDTH, SSM_WIDTH + POOL_WIDTH, SSM_WIDTH + POOL_WIDTH + ATT_WIDTH
    u_ref[...] = pr[:, :s0]
    p_ref[...] = pr[:, s0:s1]
    q_ref[...] = (pr[:, s1:s2] * (HEAD_DIM ** -0.5 * LOG2E)).astype(BF16)
    k_ref[...] = pr[:, s2:].astype(BF16)
    vt_ref[0] = lax.dot_general(wvt_ref[...], h, _NT, preferred_element_type=F32).astype(BF16)


def _in_projection(x2d, g, w_main, w_vt, bsz, seq):
    rows = x2d.shape[0]
    tiles_per_seq = seq // PROJ_ROWS
    n_main = w_main.shape[1]
    const = lambda i: (0, 0)
    row_blk = lambda width: pl.BlockSpec((PROJ_ROWS, width), lambda i: (i, 0))
    return pl.pallas_call(
        _inproj_kernel,
        grid=(rows // PROJ_ROWS,),
        in_specs=[row_blk(D_MODEL), pl.BlockSpec((1, D_MODEL), const),
                  pl.BlockSpec((D_MODEL, n_main), const), pl.BlockSpec((ATT_WIDTH, D_MODEL), const)],
        out_specs=[pl.BlockSpec((PROJ_ROWS, SSM_WIDTH), lambda i: (i % tiles_per_seq, i // tiles_per_seq)),
                   row_blk(POOL_WIDTH), row_blk(ATT_WIDTH), row_blk(ATT_WIDTH),
                   pl.BlockSpec((1, ATT_WIDTH, PROJ_ROWS),
                                lambda i: (i // tiles_per_seq, 0, i % tiles_per_seq))],
        out_shape=[jax.ShapeDtypeStruct((seq, bsz * SSM_WIDTH), F32),
                   jax.ShapeDtypeStruct((rows, POOL_WIDTH), F32),
                   jax.ShapeDtypeStruct((rows, ATT_WIDTH), BF16),
                   jax.ShapeDtypeStruct((rows, ATT_WIDTH), BF16),
                   jax.ShapeDtypeStruct((bsz, ATT_WIDTH, seq), BF16)],
        compiler_params=pltpu.CompilerParams(dimension_semantics=("parallel",),
                                             vmem_limit_bytes=VMEM_LIMIT),
        name="in_projection",
    )(x2d, g.reshape(1, D_MODEL), w_main, w_vt)


def _ssm_kernel(u_ref, bm_ref, cm_ref, ar_ref, ai_ref, d_ref, gw_ref, gb_ref, o_ref, bu_ref, st_ref, tb_ref):
    bsz = st_ref.shape[0]
    ns = SSM_STATES

    @pl.when(pl.program_id(0) == 0)
    def _init():
        st_ref[...] = jnp.zeros_like(st_ref)

    halves = SSM_WIDTH // LANES
    for b in range(bsz):
        for h in range(halves):
            c0 = b * SSM_WIDTH + h * LANES
            tb_ref[h, pl.ds(b, SSM_STEPS, stride=bsz), :] = u_ref[:, c0:c0 + LANES]
    u = jnp.concatenate([tb_ref[h] for h in range(halves)], axis=1)
    bu_ref[...] = jnp.dot(u.astype(BF16), bm_ref[...], preferred_element_type=F32)
    ar = jnp.broadcast_to(ar_ref[...], (bsz, ns))
    ai = jnp.broadcast_to(ai_ref[...], (bsz, ns))

    def step(t, carry):
        sr, si = carry
        r0 = pl.multiple_of(t * bsz, bsz)
        nr = ar * sr - ai * si + bu_ref[pl.ds(r0, bsz), 0:ns]
        ni = ar * si + ai * sr + bu_ref[pl.ds(r0, bsz), ns:2 * ns]
        bu_ref[pl.ds(r0, bsz), 0:ns] = nr
        bu_ref[pl.ds(r0, bsz), ns:2 * ns] = ni
        return nr, ni

    sr, si = lax.fori_loop(0, SSM_STEPS, step, (st_ref[:, 0:ns], st_ref[:, ns:2 * ns]), unroll=4)
    st_ref[:, 0:ns] = sr
    st_ref[:, ns:2 * ns] = si

    y = jnp.dot(bu_ref[...].astype(BF16), cm_ref[...], preferred_element_type=F32)
    y = y + d_ref[...] * u
    y = y * (0.5 * (1.0 + jnp.tanh(math.sqrt(2.0 / math.pi) * (y + 0.044715 * (y * y * y)))))
    z = jnp.dot(y.astype(BF16), gw_ref[...], preferred_element_type=F32) + gb_ref[...]
    out = y * (1.0 / (1.0 + jnp.exp(-z)))
    for h in range(halves):
        tb_ref[h] = out[:, h * LANES:(h + 1) * LANES]
    for b in range(bsz):
        for h in range(halves):
            c0 = b * SSM_WIDTH + h * LANES
            o_ref[:, c0:c0 + LANES] = tb_ref[h, pl.ds(b, SSM_STEPS, stride=bsz), :].astype(o_ref.dtype)


def _ssm_params(a_re, a_im, log_dt, b_re, b_im, c_re, c_im):
    lam = lax.complex(a_re.astype(F32), a_im.astype(F32))
    dt = jnp.exp(log_dt.astype(F32))[:, None]
    lam_bar = jnp.exp(lam * dt)
    b = lax.complex(b_re.astype(F32), b_im.astype(F32))
    b_bar = ((lam_bar - 1.0) / lam)[..., None] * b
    eye = jnp.eye(SSM_GROUPS, dtype=F32)

    def bdiag_in(m):
        return jnp.einsum('gph,gk->ghkp', m, eye).reshape(SSM_WIDTH, SSM_STATES)

    def bdiag_out(m):
        return jnp.einsum('ghp,gk->gpkh', m, eye).reshape(SSM_STATES, SSM_WIDTH)

    bm = jnp.concatenate([bdiag_in(jnp.real(b_bar)), bdiag_in(jnp.imag(b_bar))], axis=1)
    cm = jnp.concatenate([bdiag_out(c_re.astype(F32)), -bdiag_out(c_im.astype(F32))], axis=0)
    return (bm.astype(BF16), cm.astype(BF16),
            jnp.real(lam_bar).reshape(1, SSM_STATES), jnp.imag(lam_bar).reshape(1, SSM_STATES))


def _ssm_mixer(u_tm, params, d, glu_w, glu_b, bsz, seq):
    bm, cm, ar, ai = params
    rows = SSM_STEPS * bsz
    const = lambda c: (0, 0)
    return pl.pallas_call(
        _ssm_kernel,
        grid=(seq // SSM_STEPS,),
        in_specs=[
            pl.BlockSpec((SSM_STEPS, bsz * SSM_WIDTH), lambda c: (c, 0)),
            pl.BlockSpec((SSM_WIDTH, 2 * SSM_STATES), const),
            pl.BlockSpec((2 * SSM_STATES, SSM_WIDTH), const),
            pl.BlockSpec((1, SSM_STATES), const),
            pl.BlockSpec((1, SSM_STATES), const),
            pl.BlockSpec((1, SSM_WIDTH), const),
            pl.BlockSpec((SSM_WIDTH, SSM_WIDTH), const),
            pl.BlockSpec((1, SSM_WIDTH), const),
        ],
        out_specs=pl.BlockSpec((SSM_STEPS, bsz * SSM_WIDTH), lambda c: (c, 0)),
        out_shape=jax.ShapeDtypeStruct((seq, bsz * SSM_WIDTH), BF16),
        scratch_shapes=[pltpu.VMEM((rows, 2 * SSM_STATES), F32), pltpu.VMEM((bsz, 2 * SSM_STATES), F32),
                        pltpu.VMEM((SSM_WIDTH // LANES, rows, LANES), F32)],
        compiler_params=pltpu.CompilerParams(dimension_semantics=("arbitrary",),
                                             vmem_limit_bytes=VMEM_LIMIT),
        name="ssm_mixer",
    )(u_tm, bm, cm, ar, ai, d.reshape(1, SSM_WIDTH), glu_w.astype(BF16), glu_b.reshape(1, SSM_WIDTH))


def _pool_kernel(halo_ref, p_ref, w_ref, b_ref, s_ref, o_ref):
    i = pl.program_id(1)
    halo = jnp.where(i == 0, 0.0, halo_ref[0])
    p = p_ref[0]
    ext = jnp.concatenate([halo, p], axis=0)
    s2 = ext + pltpu.roll(ext, 1, 0)
    s4 = s2 + pltpu.roll(s2, 2, 0)
    s8 = s4 + pltpu.roll(s4, 4, 0)
    s16 = s8 + pltpu.roll(s8, 8, 0)
    sums = (s2, s4, s8, s16)
    lane = lax.broadcasted_iota(jnp.int32, p.shape, 1)
    t1 = (lax.broadcasted_iota(jnp.int32, p.shape, 0) + i * POOL_ROWS + 1).astype(F32)
    mean = None
    for gi, win in enumerate(POOL_WINDOWS):
        m = sums[gi][POOL_HALO:] / jnp.minimum(t1, float(win))
        mean = m if mean is None else jnp.where(lane >= gi * POOL_GROUP, m, mean)
    dlt = (mean - p).astype(BF16)
    y = jnp.dot(dlt, w_ref[...], preferred_element_type=F32)
    o_ref[0] = ((y + b_ref[...]) * s_ref[...]).astype(o_ref.dtype)


def _pool_mixer(p, w, b, scale):
    bsz, seq, _ = p.shape
    eye = jnp.eye(len(POOL_WINDOWS), dtype=F32)
    w_bd = jnp.einsum('gcd,gk->gckd', w.astype(F32), eye).reshape(POOL_WIDTH, POOL_WIDTH).astype(BF16)
    halo_blocks = POOL_ROWS // POOL_HALO
    const = lambda bi, i: (0, 0)
    return pl.pallas_call(
        _pool_kernel,
        grid=(bsz, seq // POOL_ROWS),
        in_specs=[
            pl.BlockSpec((1, POOL_HALO, POOL_WIDTH), lambda bi, i: (bi, jnp.maximum(i * halo_blocks - 1, 0), 0)),
            pl.BlockSpec((1, POOL_ROWS, POOL_WIDTH), lambda bi, i: (bi, i, 0)),
            pl.BlockSpec((POOL_WIDTH, POOL_WIDTH), const),
            pl.BlockSpec((1, POOL_WIDTH), const),
            pl.BlockSpec((1, POOL_WIDTH), const),
        ],
        out_specs=pl.BlockSpec((1, POOL_ROWS, POOL_WIDTH), lambda bi, i: (bi, i, 0)),
        out_shape=jax.ShapeDtypeStruct((bsz, seq, POOL_WIDTH), BF16),
        compiler_params=pltpu.CompilerParams(dimension_semantics=("parallel", "parallel")),
        name="pool_mixer",
    )(p, p, w_bd, b.reshape(1, POOL_WIDTH), scale.reshape(1, POOL_WIDTH))


def _bias_table_kernel(bias_ref, tab_ref):
    h = pl.program_id(0)
    kk = lax.broadcasted_iota(jnp.int32, (MOBA_BLOCK, MOBA_BLOCK), 0)
    qq = lax.broadcasted_iota(jnp.int32, (MOBA_BLOCK, MOBA_BLOCK), 1)
    for which in range(2):
        rel = qq - kk + which * MOBA_BLOCK
        val = jnp.full((MOBA_BLOCK, MOBA_BLOCK), bias_ref[0, h], F32)
        for b in range(1, REL_BUCKETS):
            val = jnp.where(rel >= _BUCKET_START[b], bias_ref[b, h], val)
        tab_ref[0, which] = jnp.where(rel >= 0, val * LOG2E, NEG_INF)


def _bias_tables(rel_bias):
    return pl.pallas_call(
        _bias_table_kernel,
        grid=(HEADS,),
        in_specs=[pl.BlockSpec(memory_space=pltpu.SMEM)],
        out_specs=pl.BlockSpec((1, 2, MOBA_BLOCK, MOBA_BLOCK), lambda h: (h, 0, 0, 0)),
        out_shape=jax.ShapeDtypeStruct((HEADS, 2, MOBA_BLOCK, MOBA_BLOCK), F32),
        name="bias_tables",
    )(rel_bias.astype(F32))


def _attn_kernel(bias_ref, q_ref, k_ref, vt_ref, tab_ref, o_ref,
                 qh_ref, va_ref, neg_ref, acc_ref, m_ref, s_ref, ahead_ref):
    seq = k_ref.shape[1]
    nb = seq // MOBA_BLOCK

    def prep():
        lane = lax.broadcasted_iota(jnp.int32, (1, HEAD_PAIR), 1)
        head_lanes = [lane < HEAD_DIM, lane >= HEAD_DIM]
        kf = k_ref[0].astype(F32).reshape(nb, MOBA_BLOCK, HEAD_PAIR)
        km = jnp.sum(kf, axis=1) * (1.0 / MOBA_BLOCK)
        q_all = q_ref[0].astype(F32)
        ones = jnp.ones((ONES_ROWS, seq), BF16)
        va_ref[0, 0:HEAD_DIM, :] = vt_ref[0, 0:HEAD_DIM, :]
        va_ref[0, HEAD_DIM:, :] = ones
        va_ref[1, 0:ONES_ROWS, :] = ones
        va_ref[1, ONES_ROWS:, :] = vt_ref[0, HEAD_DIM:, :]
        blk_id = lax.broadcasted_iota(jnp.int32, (nb, seq), 0)
        past = blk_id < lax.broadcasted_iota(jnp.int32, (nb, seq), 1) // MOBA_BLOCK
        for hh in range(2):
            qh_ref[hh] = jnp.where(head_lanes[hh], q_all, 0.0).astype(BF16)
            km_h = jnp.where(head_lanes[hh], km, 0.0).astype(BF16)
            g = lax.dot_general(km_h, qh_ref[hh], _NT, preferred_element_type=F32)
            g = jnp.where(past, g, NEG_INF)
            cnt = jnp.zeros((nb, seq), jnp.int32)
            for m in range(nb):
                gm = g[m:m + 1, :]
                cnt = cnt + ((gm > g) | ((gm == g) & (blk_id > m))).astype(jnp.int32)
            neg = jnp.where((cnt < MOBA_TOPK) & past, 0.0, NEG_INF)
            for qb in range(nb):
                neg_ref[hh, qb] = neg[:, qb * MOBA_BLOCK:(qb + 1) * MOBA_BLOCK]

    prep()

    def q_block(qi, carry):
        _attn_q_block(qi, pl.program_id(1), nb, bias_ref, k_ref, tab_ref, o_ref,
                      qh_ref, va_ref, neg_ref, acc_ref, m_ref, s_ref, ahead_ref)
        return carry

    lax.fori_loop(0, nb, q_block, 0)


def _attn_q_block(qi, pr, nb, bias_ref, k_ref, tab_ref, o_ref,
                  qh_ref, va_ref, neg_ref, acc_ref, m_ref, s_ref, ahead_ref):
    qh = [qh_ref[hh, pl.ds(pl.multiple_of(qi * MOBA_BLOCK, MOBA_BLOCK), MOBA_BLOCK), :] for hh in range(2)]

    def key_rows(j):
        return pl.ds(pl.multiple_of(j * MOBA_BLOCK, MOBA_BLOCK), MOBA_BLOCK)

    def score(hh, j):
        return lax.dot_general(k_ref[0, key_rows(j), :], qh[hh], _NT, preferred_element_type=F32)

    def attend(items, m_prev, parked=None, ahead=None):
        units = [(hh, i) for i in range(len(items)) for hh in range(2)]
        scores = {}
        if parked is None:
            for hh, i in units:
                scores[hh, i] = score(hh, items[i][0])
        ahead_units = [] if ahead is None else [(hh, i) for i in range(len(ahead[0])) for hh in range(2)]
        per_unit = -(-len(ahead_units) // len(units))
        mbs, pvs, staged = {}, {}, 0
        for n, (hh, i) in enumerate(units):
            for ahh, ai in ahead_units[n * per_unit:(n + 1) * per_unit]:
                ahead_ref[ahead[1], 2 * ai + ahh] = score(ahh, ahead[0][ai])
            add = items[i][1][hh]
            s = scores[hh, i] if parked is None else ahead_ref[parked, 2 * i + hh]
            if add.shape[0] == 1:
                smax = jnp.max(s, axis=0, keepdims=True)
                mbs[hh, i] = smax + add
                p = jnp.exp2(s - smax)
            else:
                s_ref[staged] = s
                s = s_ref[staged] + add
                staged += 1
                mbs[hh, i] = jnp.max(s, axis=0, keepdims=True)
                p = jnp.exp2(s - mbs[hh, i])
            pvs[hh, i] = jnp.dot(va_ref[hh, :, key_rows(items[i][0])], p.astype(BF16),
                                 preferred_element_type=F32)
        out = []
        for hh in range(2):
            m_new = None if m_prev is None else m_prev[hh]
            for i in range(len(items)):
                m_new = mbs[hh, i] if m_new is None else jnp.maximum(m_new, mbs[hh, i])
            acc = None if m_prev is None else acc_ref[hh] * jnp.exp2(m_prev[hh] - m_new)
            for i in range(len(items)):
                term = pvs[hh, i] * jnp.exp2(mbs[hh, i] - m_new)
                acc = term if acc is None else acc + term
            acc_ref[hh] = acc
            out.append(m_new)
        return tuple(out)

    def store_max(ms):
        m_ref[0] = ms[0]
        m_ref[1] = ms[1]

    def finalize(qb):
        oa = acc_ref[0]
        ob = acc_ref[1]
        out_t = jnp.concatenate([oa[0:HEAD_DIM] / oa[HEAD_DIM:HEAD_DIM + 1],
                                 ob[ONES_ROWS:] / ob[0:1]], axis=0)
        o_ref[0, key_rows(qb), :] = out_t.T.astype(o_ref.dtype)

    def own_item():
        return (qi, [tab_ref[hh, 0] for hh in range(2)])

    def prev_item():
        return (qi - 1, [tab_ref[hh, 1] + neg_ref[hh, qi, pl.ds(qi - 1, 1), :] for hh in range(2)])

    def far_item(j):
        return (j, [neg_ref[hh, qi, pl.ds(j, 1), :] + bias_ref[REL_BUCKETS - 1, 2 * pr + hh] * LOG2E
                    for hh in range(2)])

    n_far = jnp.maximum(qi - 1, 0)
    n_rem = n_far % FAR_GROUP
    n_groups = n_far // FAR_GROUP

    def group_blocks(g):
        return [n_rem + g * FAR_GROUP + b for b in range(FAR_GROUP)]

    @pl.when(qi == 0)
    def _first_block():
        store_max(attend([own_item()], None))

    for r in range(FAR_GROUP):
        for follows in (False, True):
            @pl.when((qi > 0) & (n_rem == r) & ((n_groups > 0) == follows))
            def _near_blocks(r=r, follows=follows):
                finalize(qi - 1)
                items = [own_item(), prev_item()] + [far_item(j) for j in range(r)]
                store_max(attend(items, None, ahead=(group_blocks(0), 0) if follows else None))

    max_groups = (nb - 2) // FAR_GROUP
    for g in range(max_groups):
        for follows in (False, True)[:1 if g + 1 == max_groups else 2]:
            @pl.when((g < n_groups) & ((g + 1 < n_groups) == follows))
            def _far_group(g=g, follows=follows):
                items = [far_item(j) for j in group_blocks(g)]
                ahead = (group_blocks(g + 1), (g + 1) % 2) if follows else None
                store_max(attend(items, (m_ref[0], m_ref[1]), parked=g % 2, ahead=ahead))

    @pl.when(qi == nb - 1)
    def _last_block():
        finalize(qi)


def _moba_attention(q, k, vt, tabs, rel_bias):
    bsz, seq, _ = q.shape
    assert seq % MOBA_BLOCK == 0
    nb = seq // MOBA_BLOCK
    return pl.pallas_call(
        _attn_kernel,
        grid=(bsz, HEADS // 2),
        in_specs=[
            pl.BlockSpec(memory_space=pltpu.SMEM),
            pl.BlockSpec((1, seq, HEAD_PAIR), lambda b, p: (b, 0, p)),
            pl.BlockSpec((1, seq, HEAD_PAIR), lambda b, p: (b, 0, p)),
            pl.BlockSpec((1, HEAD_PAIR, seq), lambda b, p: (b, p, 0)),
            pl.BlockSpec((2, 2, MOBA_BLOCK, MOBA_BLOCK), lambda b, p: (p, 0, 0, 0)),
        ],
        out_specs=pl.BlockSpec((1, seq, HEAD_PAIR), lambda b, p: (b, 0, p)),
        out_shape=jax.ShapeDtypeStruct((bsz, seq, ATT_WIDTH), BF16),
        scratch_shapes=[
            pltpu.VMEM((2, seq, HEAD_PAIR), BF16),
            pltpu.VMEM((2, HEAD_DIM + ONES_ROWS, seq), BF16),
            pltpu.VMEM((2, nb, nb, MOBA_BLOCK), F32),
            pltpu.VMEM((2, HEAD_DIM + ONES_ROWS, MOBA_BLOCK), F32),
            pltpu.VMEM((2, 1, MOBA_BLOCK), F32),
            pltpu.VMEM((4, MOBA_BLOCK, MOBA_BLOCK), F32),
            pltpu.VMEM((2, 2 * FAR_GROUP, MOBA_BLOCK, MOBA_BLOCK), F32),
        ],
        compiler_params=pltpu.CompilerParams(
            dimension_semantics=("parallel", "parallel"), vmem_limit_bytes=VMEM_LIMIT),
        name="moba_attention",
    )(rel_bias.astype(F32), q, k, vt, tabs)


def _routing_weights(logits_t):
    col = lambda i: logits_t[i:i + 1, :]
    gl = [col(g) for g in range(MOE_GROUPS)]
    gmax = _max_of(gl)
    sel, taken = [], None
    for g in range(MOE_GROUPS):
        hit = gl[g] == gmax
        if taken is not None:
            hit = hit & jnp.logical_not(taken)
        taken = hit if taken is None else (taken | hit)
        sel.append(hit)
    denom = None
    for g in range(MOE_GROUPS):
        e = jnp.exp(gl[g] - gmax)
        denom = e if denom is None else denom + e
    g_wt = 1.0 / denom
    el = []
    for e in range(MOE_PER_GROUP):
        v = col(MOE_GROUPS + (MOE_GROUPS - 1) * MOE_PER_GROUP + e)
        for g in range(MOE_GROUPS - 2, -1, -1):
            v = jnp.where(sel[g], col(MOE_GROUPS + g * MOE_PER_GROUP + e), v)
        el.append(v)
    rank = []
    for e in range(MOE_PER_GROUP):
        r = jnp.zeros_like(el[e], dtype=jnp.int32)
        for m in range(MOE_PER_GROUP):
            if m == e:
                continue
            beats = (el[m] >= el[e]) if m < e else (el[m] > el[e])
            r = r + beats.astype(jnp.int32)
        rank.append(r)
    v1 = _max_of(el)
    v2 = None
    for e in range(MOE_PER_GROUP):
        c = jnp.where(rank[e] == 1, el[e], 0.0)
        v2 = c if v2 is None else v2 + c
    e2 = jnp.exp(v2 - v1)
    w1 = 1.0 / (1.0 + e2)
    w2 = e2 / (1.0 + e2)
    tokens = logits_t.shape[1]
    row_id = lax.broadcasted_iota(jnp.int32, (MOE_EXPERTS, tokens), 0)
    comb_t = jnp.zeros((MOE_EXPERTS, tokens), F32)
    for g in range(MOE_GROUPS):
        for e in range(MOE_PER_GROUP):
            within = jnp.where(rank[e] == 0, w1, jnp.where(rank[e] == 1, w2, 0.0))
            c = jnp.where(sel[g], g_wt * within, 0.0)
            comb_t = jnp.where(row_id == g * MOE_PER_GROUP + e, c, comb_t)
    pad = jnp.zeros((ROUTER_LANES - MOE_EXPERTS, tokens), F32)
    return jnp.concatenate([comb_t, pad], axis=0).T


def _max_of(cols):
    out = cols[0]
    for c in cols[1:]:
        out = jnp.maximum(out, c)
    return out


def _moe_kernel(x_ref, ys_ref, yp_ref, ya_ref, wo_ref, g_ref, wr_ref, br_ref, wgu_ref, wd_ref, fg_ref,
                o_ref, h_ref, comb_ref, *, final_norm):
    step = pl.program_id(1)

    @pl.when(step == 0)
    def _mix_and_route():
        s0, s1 = SSM_WIDTH, SSM_WIDTH + POOL_WIDTH
        mix = jnp.dot(ys_ref[...], wo_ref[0:s0, :], preferred_element_type=F32)
        mix = mix + jnp.dot(yp_ref[...], wo_ref[s0:s1, :], preferred_element_type=F32)
        mix = mix + jnp.dot(ya_ref[...], wo_ref[s1:, :], preferred_element_type=F32)
        x = x_ref[...] + mix
        h = _rms_norm(x, g_ref[...]).astype(BF16)
        h_ref[...] = h
        logits_t = lax.dot_general(wr_ref[...], h, _NT, preferred_element_type=F32)
        comb_ref[...] = _routing_weights(logits_t + br_ref[...])
        o_ref[...] = x

    h = h_ref[...]
    lane = lax.broadcasted_iota(jnp.int32, comb_ref.shape, 1)
    y = None
    for j in range(EXPERTS_PER_STEP):
        gu = jnp.dot(h, wgu_ref[j], preferred_element_type=F32)
        hg, hu = gu[:, :D_EXPERT], gu[:, D_EXPERT:]
        c = jnp.sum(jnp.where(lane == step * EXPERTS_PER_STEP + j, comb_ref[...], 0.0), axis=1, keepdims=True)
        act = (hg * (1.0 / (1.0 + jnp.exp(-hg)))) * hu * c
        yj = jnp.dot(act.astype(BF16), wd_ref[j], preferred_element_type=F32)
        y = yj if y is None else y + yj
    o_ref[...] += y

    if final_norm:
        @pl.when(step == MOE_EXPERTS // EXPERTS_PER_STEP - 1)
        def _final():
            o_ref[...] = _rms_norm(o_ref[...], fg_ref[...])


def _mix_and_moe(x2d, y_ssm_tm, y_pool, y_att, w_out, norm_g, w_router, b_router, w_gu, w_down, final_g,
                 final_norm):
    rows = x2d.shape[0]
    tiles_per_seq = y_ssm_tm.shape[0] // MOE_ROWS
    const = lambda i, e: (0, 0)
    row_blk = lambda width: pl.BlockSpec((MOE_ROWS, width), lambda i, e: (i, 0))
    return pl.pallas_call(
        partial(_moe_kernel, final_norm=final_norm),
        grid=(rows // MOE_ROWS, MOE_EXPERTS // EXPERTS_PER_STEP),
        in_specs=[
            row_blk(D_MODEL),
            pl.BlockSpec((MOE_ROWS, SSM_WIDTH), lambda i, e: (i % tiles_per_seq, i // tiles_per_seq)),
            row_blk(POOL_WIDTH),
            row_blk(ATT_WIDTH),
            pl.BlockSpec((D_MODEL, D_MODEL), const),
            pl.BlockSpec((1, D_MODEL), const),
            pl.BlockSpec((ROUTER_ROWS, D_MODEL), const),
            pl.BlockSpec((ROUTER_ROWS, 1), const),
            pl.BlockSpec((EXPERTS_PER_STEP, D_MODEL, 2 * D_EXPERT), lambda i, e: (e, 0, 0)),
            pl.BlockSpec((EXPERTS_PER_STEP, D_EXPERT, D_MODEL), lambda i, e: (e, 0, 0)),
            pl.BlockSpec((1, D_MODEL), const),
        ],
        out_specs=pl.BlockSpec((MOE_ROWS, D_MODEL), lambda i, e: (i, 0)),
        out_shape=jax.ShapeDtypeStruct((rows, D_MODEL), F32),
        scratch_shapes=[pltpu.VMEM((MOE_ROWS, D_MODEL), BF16), pltpu.VMEM((MOE_ROWS, ROUTER_LANES), F32)],
        compiler_params=pltpu.CompilerParams(dimension_semantics=("parallel", "arbitrary"),
                                             vmem_limit_bytes=VMEM_LIMIT),
        name="mix_and_moe",
    )(x2d, y_ssm_tm, y_pool, y_att, w_out, norm_g.reshape(1, D_MODEL), w_router, b_router, w_gu, w_down,
      final_g.reshape(1, D_MODEL))


def _router_params(group_w, group_b, router_w, router_b):
    w = jnp.concatenate([group_w.astype(F32)] + [router_w[g].astype(F32) for g in range(MOE_GROUPS)], axis=1)
    b = jnp.concatenate([group_b.astype(F32), router_b.astype(F32).reshape(-1)])
    pad = ROUTER_ROWS - w.shape[1]
    return jnp.pad(w.T, ((0, pad), (0, 0))).astype(BF16), jnp.pad(b, (0, pad)).reshape(ROUTER_ROWS, 1)


def kernel(x, rel_bias, norm1_g, w_in, ssm_a_re, ssm_a_im, ssm_log_dt, ssm_b_re, ssm_b_im, ssm_c_re, ssm_c_im,
           ssm_d, ssm_glu_w, ssm_glu_b, pool_w, pool_b, pool_scale, w_out, norm2_g, moe_group_w, moe_group_b,
           moe_router_w, moe_router_b, moe_w_gate, moe_w_up, moe_w_down, final_norm_g):
    bsz, seq, dm = x.shape
    depth = w_in.shape[0]
    assert dm == D_MODEL and depth >= 1
    assert bsz % SUBLANES == 0, "the scan keeps one time step of all batches in whole sublane tiles"
    assert all(seq % t == 0 for t in (PROJ_ROWS, MOE_ROWS, POOL_ROWS, SSM_STEPS, MOBA_BLOCK))
    n_main = SSM_WIDTH + POOL_WIDTH + 2 * ATT_WIDTH
    x2d = x.astype(F32).reshape(bsz * seq, dm)
    tabs = _bias_tables(rel_bias)
    for l in range(depth):
        w_main = w_in[l, :, :n_main].astype(BF16)
        w_vt = w_in[l, :, n_main:].T.astype(BF16)
        u, p, q, k, vt = _in_projection(x2d, norm1_g[l], w_main, w_vt, bsz, seq)

        params = _ssm_params(ssm_a_re[l], ssm_a_im[l], ssm_log_dt[l], ssm_b_re[l], ssm_b_im[l],
                             ssm_c_re[l], ssm_c_im[l])
        y_ssm = _ssm_mixer(u, params, ssm_d[l], ssm_glu_w[l], ssm_glu_b[l], bsz, seq)

        y_pool = _pool_mixer(p.reshape(bsz, seq, POOL_WIDTH), pool_w[l], pool_b[l], pool_scale[l])
        y_att = _moba_attention(q.reshape(bsz, seq, ATT_WIDTH), k.reshape(bsz, seq, ATT_WIDTH), vt, tabs, rel_bias)

        w_router, b_router = _router_params(moe_group_w[l], moe_group_b[l], moe_router_w[l], moe_router_b[l])
        w_gu = jnp.concatenate([moe_w_gate[l], moe_w_up[l]], axis=-1).astype(BF16)
        x2d = _mix_and_moe(x2d, y_ssm, y_pool.reshape(bsz * seq, POOL_WIDTH), y_att.reshape(bsz * seq, ATT_WIDTH),
                           w_out[l].astype(BF16), norm2_g[l], w_router, b_router, w_gu,
                           moe_w_down[l].astype(BF16), final_norm_g, final_norm=(l == depth - 1))
    return x2d.reshape(bsz, seq, dm).astype(x.dtype)
```

```python
import math
from functools import partial

import jax
import jax.numpy as jnp
from jax import lax
from jax.experimental import pallas as pl
from jax.experimental.pallas import tpu as pltpu

F32 = jnp.float32
BF16 = jnp.bfloat16

LANES = 128
SUBLANES = 8
D_MODEL = 1024
SSM_WIDTH = 256
POOL_WIDTH = 256
ATT_WIDTH = 512
SSM_GROUP = 16
SSM_GROUPS = SSM_WIDTH // SSM_GROUP
SSM_STATE = 64
SSM_STATES = SSM_GROUPS * SSM_STATE
POOL_WINDOWS = (2, 4, 8, 16)
POOL_GROUP = POOL_WIDTH // len(POOL_WINDOWS)
POOL_HALO = 16
HEAD_DIM = 64
HEADS = ATT_WIDTH // HEAD_DIM
HEAD_PAIR = 2 * HEAD_DIM
ONES_ROWS = 16
MOBA_BLOCK = 256
MOBA_TOPK = 3
FAR_GROUP = 4
REL_BUCKETS = 32
REL_MAX_EXACT = REL_BUCKETS // 2
REL_MAX_DIST = 128
MOE_GROUPS = 4
MOE_PER_GROUP = 4
MOE_EXPERTS = MOE_GROUPS * MOE_PER_GROUP
D_EXPERT = D_MODEL // 4
ROUTER_ROWS = 32
ROUTER_LANES = 128
RMS_EPS = 1e-6
NEG_INF = -1e30
TAKEN = -3e38
LOG2E = math.log2(math.e)

PROJ_ROWS = 1024
MOE_ROWS = 1024
EXPERTS_PER_STEP = 4
SSM_STEPS = 128
VMEM_LIMIT = 48 * 1024 * 1024

_NT = (((1,), (1,)), ((), ()))

_BUCKET_START = list(range(REL_MAX_EXACT)) + [
    math.ceil(REL_MAX_EXACT * (REL_MAX_DIST / REL_MAX_EXACT) ** (k / (REL_BUCKETS - REL_MAX_EXACT)))
    for k in range(REL_BUCKETS - REL_MAX_EXACT)
]


def _rms_norm(x, g):
    return x * lax.rsqrt(jnp.mean(x * x, axis=-1, keepdims=True) + RMS_EPS) * g


def _inproj_kernel(x_ref, g_ref, w_ref, wvt_ref, pw_ref, pb_ref, ps_ref,
                   u_ref, yp_ref, q_ref, k_ref, vt_ref, halo_ref, *, tiles_per_seq):
    h = _rms_norm(x_ref[...], g_ref[...]).astype(BF16)
    pr = jnp.dot(h, w_ref[...], preferred_element_type=F32)
    s0, s1, s2 = SSM_WIDTH, SSM_WIDTH + POOL_WIDTH, SSM_WIDTH + POOL_WIDTH + ATT_WIDTH
    u_ref[...] = pr[:, :s0]
    t_tile = pl.program_id(0) % tiles_per_seq
    p = pr[:, s0:s1]
    halo = jnp.where(t_tile == 0, 0.0, halo_ref[...])
    halo_ref[...] = p[PROJ_ROWS - POOL_HALO:, :]
    yp_ref[...] = _pool_mix(p, halo, t_tile * PROJ_ROWS, pw_ref[...], pb_ref[...], ps_ref[...]).astype(BF16)
    q_ref[...] = (pr[:, s1:s2] * (HEAD_DIM ** -0.5 * LOG2E)).astype(BF16)
    k_ref[...] = pr[:, s2:].astype(BF16)
    vt_ref[0] = lax.dot_general(wvt_ref[...], h, _NT, preferred_element_type=F32).astype(BF16)


def _pool_mix(p, halo, t0, w_bd, b, scale):
    ext = jnp.concatenate([halo, p], axis=0)
    s2 = ext + pltpu.roll(ext, 1, 0)
    s4 = s2 + pltpu.roll(s2, 2, 0)
    s8 = s4 + pltpu.roll(s4, 4, 0)
    s16 = s8 + pltpu.roll(s8, 8, 0)
    sums = (s2, s4, s8, s16)
    lane = lax.broadcasted_iota(jnp.int32, p.shape, 1)
    t1 = (lax.broadcasted_iota(jnp.int32, p.shape, 0) + t0 + 1).astype(F32)
    mean = None
    for gi, win in enumerate(POOL_WINDOWS):
        m = sums[gi][POOL_HALO:] / jnp.minimum(t1, float(win))
        mean = m if mean is None else jnp.where(lane >= gi * POOL_GROUP, m, mean)
    y = jnp.dot((mean - p).astype(BF16), w_bd, preferred_element_type=F32)
    return (y + b) * scale


def _pool_params(w, b, scale):
    eye = jnp.eye(len(POOL_WINDOWS), dtype=F32)
    w_bd = jnp.einsum('gcd,gk->gckd', w.astype(F32), eye).reshape(POOL_WIDTH, POOL_WIDTH).astype(BF16)
    return w_bd, b.astype(F32).reshape(1, POOL_WIDTH), scale.astype(F32).reshape(1, POOL_WIDTH)


def _in_projection(x2d, g, w_main, w_vt, pool_params, bsz, seq):
    rows = x2d.shape[0]
    tiles_per_seq = seq // PROJ_ROWS
    n_main = w_main.shape[1]
    const = lambda i: (0, 0)
    row_blk = lambda width: pl.BlockSpec((PROJ_ROWS, width), lambda i: (i, 0))
    return pl.pallas_call(
        partial(_inproj_kernel, tiles_per_seq=tiles_per_seq),
        grid=(rows // PROJ_ROWS,),
        in_specs=[row_blk(D_MODEL), pl.BlockSpec((1, D_MODEL), const),
                  pl.BlockSpec((D_MODEL, n_main), const), pl.BlockSpec((ATT_WIDTH, D_MODEL), const),
                  pl.BlockSpec((POOL_WIDTH, POOL_WIDTH), const), pl.BlockSpec((1, POOL_WIDTH), const),
                  pl.BlockSpec((1, POOL_WIDTH), const)],
        out_specs=[pl.BlockSpec((PROJ_ROWS, SSM_WIDTH), lambda i: (i % tiles_per_seq, i // tiles_per_seq)),
                   row_blk(POOL_WIDTH), row_blk(ATT_WIDTH), row_blk(ATT_WIDTH),
                   pl.BlockSpec((1, ATT_WIDTH, PROJ_ROWS),
                                lambda i: (i // tiles_per_seq, 0, i % tiles_per_seq))],
        out_shape=[jax.ShapeDtypeStruct((seq, bsz * SSM_WIDTH), F32),
                   jax.ShapeDtypeStruct((rows, POOL_WIDTH), BF16),
                   jax.ShapeDtypeStruct((rows, ATT_WIDTH), BF16),
                   jax.ShapeDtypeStruct((rows, ATT_WIDTH), BF16),
                   jax.ShapeDtypeStruct((bsz, ATT_WIDTH, seq), BF16)],
        scratch_shapes=[pltpu.VMEM((POOL_HALO, POOL_WIDTH), F32)],
        compiler_params=pltpu.CompilerParams(dimension_semantics=("arbitrary",),
                                             vmem_limit_bytes=VMEM_LIMIT),
        name="in_projection",
    )(x2d, g.reshape(1, D_MODEL), w_main, w_vt, *pool_params)


def _ssm_kernel(u_ref, bm_ref, cm_ref, ar_ref, ai_ref, d_ref, gw_ref, gb_ref, o_ref, bu_ref, st_ref, tb_ref):
    bsz = st_ref.shape[0]
    ns = SSM_STATES

    @pl.when(pl.program_id(0) == 0)
    def _init():
        st_ref[...] = jnp.zeros_like(st_ref)

    halves = SSM_WIDTH // LANES
    for b in range(bsz):
        for h in range(halves):
            c0 = b * SSM_WIDTH + h * LANES
            tb_ref[h, pl.ds(b, SSM_STEPS, stride=bsz), :] = u_ref[:, c0:c0 + LANES]
    u = jnp.concatenate([tb_ref[h] for h in range(halves)], axis=1)
    bu_ref[...] = jnp.dot(u.astype(BF16), bm_ref[...], preferred_element_type=F32)
    ar = jnp.broadcast_to(ar_ref[...], (bsz, ns))
    ai = jnp.broadcast_to(ai_ref[...], (bsz, ns))

    def step(t, carry):
        sr, si = carry
        r0 = pl.multiple_of(t * bsz, bsz)
        nr = ar * sr - ai * si + bu_ref[pl.ds(r0, bsz), 0:ns]
        ni = ar * si + ai * sr + bu_ref[pl.ds(r0, bsz), ns:2 * ns]
        bu_ref[pl.ds(r0, bsz), 0:ns] = nr
        bu_ref[pl.ds(r0, bsz), ns:2 * ns] = ni
        return nr, ni

    sr, si = lax.fori_loop(0, SSM_STEPS, step, (st_ref[:, 0:ns], st_ref[:, ns:2 * ns]), unroll=4)
    st_ref[:, 0:ns] = sr
    st_ref[:, ns:2 * ns] = si

    y = jnp.dot(bu_ref[...].astype(BF16), cm_ref[...], preferred_element_type=F32)
    y = y + d_ref[...] * u
    y = y * (0.5 * (1.0 + jnp.tanh(math.sqrt(2.0 / math.pi) * (y + 0.044715 * (y * y * y)))))
    z = jnp.dot(y.astype(BF16), gw_ref[...], preferred_element_type=F32) + gb_ref[...]
    out = y * (1.0 / (1.0 + jnp.exp(-z)))
    for h in range(halves):
        tb_ref[h] = out[:, h * LANES:(h + 1) * LANES]
    for b in range(bsz):
        for h in range(halves):
            c0 = b * SSM_WIDTH + h * LANES
            o_ref[:, c0:c0 + LANES] = tb_ref[h, pl.ds(b, SSM_STEPS, stride=bsz), :].astype(o_ref.dtype)


def _ssm_params(a_re, a_im, log_dt, b_re, b_im, c_re, c_im):
    lam = lax.complex(a_re.astype(F32), a_im.astype(F32))
    dt = jnp.exp(log_dt.astype(F32))[:, None]
    lam_bar = jnp.exp(lam * dt)
    b = lax.complex(b_re.astype(F32), b_im.astype(F32))
    b_bar = ((lam_bar - 1.0) / lam)[..., None] * b
    eye = jnp.eye(SSM_GROUPS, dtype=F32)

    def bdiag_in(m):
        return jnp.einsum('gph,gk->ghkp', m, eye).reshape(SSM_WIDTH, SSM_STATES)

    def bdiag_out(m):
        return jnp.einsum('ghp,gk->gpkh', m, eye).reshape(SSM_STATES, SSM_WIDTH)

    bm = jnp.concatenate([bdiag_in(jnp.real(b_bar)), bdiag_in(jnp.imag(b_bar))], axis=1)
    cm = jnp.concatenate([bdiag_out(c_re.astype(F32)), -bdiag_out(c_im.astype(F32))], axis=0)
    return (bm.astype(BF16), cm.astype(BF16),
            jnp.real(lam_bar).reshape(1, SSM_STATES), jnp.imag(lam_bar).reshape(1, SSM_STATES))


def _ssm_mixer(u_tm, params, d, glu_w, glu_b, bsz, seq):
    bm, cm, ar, ai = params
    rows = SSM_STEPS * bsz
    const = lambda c: (0, 0)
    return pl.pallas_call(
        _ssm_kernel,
        grid=(seq // SSM_STEPS,),
        in_specs=[
            pl.BlockSpec((SSM_STEPS, bsz * SSM_WIDTH), lambda c: (c, 0)),
            pl.BlockSpec((SSM_WIDTH, 2 * SSM_STATES), const),
            pl.BlockSpec((2 * SSM_STATES, SSM_WIDTH), const),
            pl.BlockSpec((1, SSM_STATES), const),
            pl.BlockSpec((1, SSM_STATES), const),
            pl.BlockSpec((1, SSM_WIDTH), const),
            pl.BlockSpec((SSM_WIDTH, SSM_WIDTH), const),
            pl.BlockSpec((1, SSM_WIDTH), const),
        ],
        out_specs=pl.BlockSpec((SSM_STEPS, bsz * SSM_WIDTH), lambda c: (c, 0)),
        out_shape=jax.ShapeDtypeStruct((seq, bsz * SSM_WIDTH), BF16),
        scratch_shapes=[pltpu.VMEM((rows, 2 * SSM_STATES), F32), pltpu.VMEM((bsz, 2 * SSM_STATES), F32),
                        pltpu.VMEM((SSM_WIDTH // LANES, rows, LANES), F32)],
        compiler_params=pltpu.CompilerParams(dimension_semantics=("arbitrary",),
                                             vmem_limit_bytes=VMEM_LIMIT),
        name="ssm_mixer",
    )(u_tm, bm, cm, ar, ai, d.reshape(1, SSM_WIDTH), glu_w.astype(BF16), glu_b.reshape(1, SSM_WIDTH))


def _bias_table_kernel(bias_ref, tab_ref):
    h = pl.program_id(0)
    kk = lax.broadcasted_iota(jnp.int32, (MOBA_BLOCK, MOBA_BLOCK), 0)
    qq = lax.broadcasted_iota(jnp.int32, (MOBA_BLOCK, MOBA_BLOCK), 1)
    for which in range(2):
        rel = qq - kk + which * MOBA_BLOCK
        val = jnp.full((MOBA_BLOCK, MOBA_BLOCK), bias_ref[0, h], F32)
        for b in range(1, REL_BUCKETS):
            val = jnp.where(rel >= _BUCKET_START[b], bias_ref[b, h], val)
        tab_ref[0, which] = jnp.where(rel >= 0, val * LOG2E, NEG_INF)


def _bias_tables(rel_bias):
    return pl.pallas_call(
        _bias_table_kernel,
        grid=(HEADS,),
        in_specs=[pl.BlockSpec(memory_space=pltpu.SMEM)],
        out_specs=pl.BlockSpec((1, 2, MOBA_BLOCK, MOBA_BLOCK), lambda h: (h, 0, 0, 0)),
        out_shape=jax.ShapeDtypeStruct((HEADS, 2, MOBA_BLOCK, MOBA_BLOCK), F32),
        name="bias_tables",
    )(rel_bias.astype(F32))


def _attn_kernel(bias_ref, q_ref, k_ref, vt_ref, tab_ref, o_ref,
                 qh_ref, va_ref, neg_ref, acc_ref, m_ref, s_ref, ahead_ref):
    seq = k_ref.shape[1]
    nb = seq // MOBA_BLOCK

    def prep():
        lane = lax.broadcasted_iota(jnp.int32, (1, HEAD_PAIR), 1)
        head_lanes = [lane < HEAD_DIM, lane >= HEAD_DIM]
        kf = k_ref[0].astype(F32).reshape(nb, MOBA_BLOCK, HEAD_PAIR)
        km = jnp.sum(kf, axis=1) * (1.0 / MOBA_BLOCK)
        q_all = q_ref[0].astype(F32)
        ones = jnp.ones((ONES_ROWS, seq), BF16)
        va_ref[0, 0:HEAD_DIM, :] = vt_ref[0, 0:HEAD_DIM, :]
        va_ref[0, HEAD_DIM:, :] = ones
        va_ref[1, 0:ONES_ROWS, :] = ones
        va_ref[1, ONES_ROWS:, :] = vt_ref[0, HEAD_DIM:, :]
        blk_id = lax.broadcasted_iota(jnp.int32, (nb, seq), 0)
        past = blk_id < lax.broadcasted_iota(jnp.int32, (nb, seq), 1) // MOBA_BLOCK
        for hh in range(2):
            qh_ref[hh] = jnp.where(head_lanes[hh], q_all, 0.0).astype(BF16)
            km_h = jnp.where(head_lanes[hh], km, 0.0).astype(BF16)
            g = lax.dot_general(km_h, qh_ref[hh], _NT, preferred_element_type=F32)
            g = jnp.where(past, g, NEG_INF)
            sel = None
            for _ in range(min(MOBA_TOPK, nb)):
                top = jnp.max(g, axis=0, keepdims=True)
                first = jnp.min(jnp.where(g == top, blk_id, nb), axis=0, keepdims=True)
                hit = blk_id == first
                sel = hit if sel is None else (sel | hit)
                g = jnp.where(hit, TAKEN, g)
            neg = jnp.where(sel & past, 0.0, NEG_INF)
            for qb in range(nb):
                neg_ref[hh, qb] = neg[:, qb * MOBA_BLOCK:(qb + 1) * MOBA_BLOCK]

    prep()

    def q_block(qi, carry):
        _attn_q_block(qi, pl.program_id(1), nb, bias_ref, k_ref, tab_ref, o_ref,
                      qh_ref, va_ref, neg_ref, acc_ref, m_ref, s_ref, ahead_ref)
        return carry

    lax.fori_loop(0, nb, q_block, 0)


def _attn_q_block(qi, pr, nb, bias_ref, k_ref, tab_ref, o_ref,
                  qh_ref, va_ref, neg_ref, acc_ref, m_ref, s_ref, ahead_ref):
    qh = [qh_ref[hh, pl.ds(pl.multiple_of(qi * MOBA_BLOCK, MOBA_BLOCK), MOBA_BLOCK), :] for hh in range(2)]

    def key_rows(j):
        return pl.ds(pl.multiple_of(j * MOBA_BLOCK, MOBA_BLOCK), MOBA_BLOCK)

    def score(hh, j):
        return lax.dot_general(k_ref[0, key_rows(j), :], qh[hh], _NT, preferred_element_type=F32)

    def attend(items, m_prev, parked=None, ahead=None):
        units = [(hh, i) for i in range(len(items)) for hh in range(2)]
        scores = {}
        if parked is None:
            for hh, i in units:
                scores[hh, i] = score(hh, items[i][0])
        ahead_units = [] if ahead is None else [(hh, i) for i in range(len(ahead[0])) for hh in range(2)]
        per_unit = -(-len(ahead_units) // len(units))
        mbs, pvs, staged = {}, {}, 0
        for n, (hh, i) in enumerate(units):
            for ahh, ai in ahead_units[n * per_unit:(n + 1) * per_unit]:
                ahead_ref[ahead[1], 2 * ai + ahh] = score(ahh, ahead[0][ai])
            add = items[i][1][hh]
            s = scores[hh, i] if parked is None else ahead_ref[parked, 2 * i + hh]
            if add.shape[0] == 1:
                smax = jnp.max(s, axis=0, keepdims=True)
                mbs[hh, i] = smax + add
                p = jnp.exp2(s - smax)
            else:
                s_ref[staged] = s
                s = s_ref[staged] + add
                staged += 1
                mbs[hh, i] = jnp.max(s, axis=0, keepdims=True)
                p = jnp.exp2(s - mbs[hh, i])
            pvs[hh, i] = jnp.dot(va_ref[hh, :, key_rows(items[i][0])], p.astype(BF16),
                                 preferred_element_type=F32)
        out = []
        for hh in range(2):
            m_new = None if m_prev is None else m_prev[hh]
            for i in range(len(items)):
                m_new = mbs[hh, i] if m_new is None else jnp.maximum(m_new, mbs[hh, i])
            acc = None if m_prev is None else acc_ref[hh] * jnp.exp2(m_prev[hh] - m_new)
            for i in range(len(items)):
                term = pvs[hh, i] * jnp.exp2(mbs[hh, i] - m_new)
                acc = term if acc is None else acc + term
            acc_ref[hh] = acc
            out.append(m_new)
        return tuple(out)

    def store_max(ms):
        m_ref[0] = ms[0]
        m_ref[1] = ms[1]

    def finalize(qb):
        oa = acc_ref[0]
        ob = acc_ref[1]
        out_t = jnp.concatenate([oa[0:HEAD_DIM] / oa[HEAD_DIM:HEAD_DIM + 1],
                                 ob[ONES_ROWS:] / ob[0:1]], axis=0)
        o_ref[0, key_rows(qb), :] = out_t.T.astype(o_ref.dtype)

    def own_item():
        return (qi, [tab_ref[hh, 0] for hh in range(2)])

    def prev_item():
        return (qi - 1, [tab_ref[hh, 1] + neg_ref[hh, qi, pl.ds(qi - 1, 1), :] for hh in range(2)])

    def far_item(j):
        return (j, [neg_ref[hh, qi, pl.ds(j, 1), :] + bias_ref[REL_BUCKETS - 1, 2 * pr + hh] * LOG2E
                    for hh in range(2)])

    n_far = jnp.maximum(qi - 1, 0)
    n_rem = n_far % FAR_GROUP
    n_groups = n_far // FAR_GROUP

    def group_blocks(g):
        return [n_rem + g * FAR_GROUP + b for b in range(FAR_GROUP)]

    @pl.when(qi == 0)
    def _first_block():
        store_max(attend([own_item()], None))

    for r in range(FAR_GROUP):
        for follows in (False, True):
            @pl.when((qi > 0) & (n_rem == r) & ((n_groups > 0) == follows))
            def _near_blocks(r=r, follows=follows):
                finalize(qi - 1)
                items = [own_item(), prev_item()] + [far_item(j) for j in range(r)]
                store_max(attend(items, None, ahead=(group_blocks(0), 0) if follows else None))

    max_groups = (nb - 2) // FAR_GROUP
    for g in range(max_groups):
        for follows in (False, True)[:1 if g + 1 == max_groups else 2]:
            @pl.when((g < n_groups) & ((g + 1 < n_groups) == follows))
            def _far_group(g=g, follows=follows):
                items = [far_item(j) for j in group_blocks(g)]
                ahead = (group_blocks(g + 1), (g + 1) % 2) if follows else None
                store_max(attend(items, (m_ref[0], m_ref[1]), parked=g % 2, ahead=ahead))

    @pl.when(qi == nb - 1)
    def _last_block():
        finalize(qi)


def _moba_attention(q, k, vt, tabs, rel_bias):
    bsz, seq, _ = q.shape
    assert seq % MOBA_BLOCK == 0
    nb = seq // MOBA_BLOCK
    return pl.pallas_call(
        _attn_kernel,
        grid=(bsz, HEADS // 2),
        in_specs=[
            pl.BlockSpec(memory_space=pltpu.SMEM),
            pl.BlockSpec((1, seq, HEAD_PAIR), lambda b, p: (b, 0, p)),
            pl.BlockSpec((1, seq, HEAD_PAIR), lambda b, p: (b, 0, p)),
            pl.BlockSpec((1, HEAD_PAIR, seq), lambda b, p: (b, p, 0)),
            pl.BlockSpec((2, 2, MOBA_BLOCK, MOBA_BLOCK), lambda b, p: (p, 0, 0, 0)),
        ],
        out_specs=pl.BlockSpec((1, seq, HEAD_PAIR), lambda b, p: (b, 0, p)),
        out_shape=jax.ShapeDtypeStruct((bsz, seq, ATT_WIDTH), BF16),
        scratch_shapes=[
            pltpu.VMEM((2, seq, HEAD_PAIR), BF16),
            pltpu.VMEM((2, HEAD_DIM + ONES_ROWS, seq), BF16),
            pltpu.VMEM((2, nb, nb, MOBA_BLOCK), F32),
            pltpu.VMEM((2, HEAD_DIM + ONES_ROWS, MOBA_BLOCK), F32),
            pltpu.VMEM((2, 1, MOBA_BLOCK), F32),
            pltpu.VMEM((4, MOBA_BLOCK, MOBA_BLOCK), F32),
            pltpu.VMEM((2, 2 * FAR_GROUP, MOBA_BLOCK, MOBA_BLOCK), F32),
        ],
        compiler_params=pltpu.CompilerParams(
            dimension_semantics=("parallel", "parallel"), vmem_limit_bytes=VMEM_LIMIT),
        name="moba_attention",
    )(rel_bias.astype(F32), q, k, vt, tabs)


def _routing_weights(logits_t):
    col = lambda i: logits_t[i:i + 1, :]
    gl = [col(g) for g in range(MOE_GROUPS)]
    gmax = _max_of(gl)
    sel, taken = [], None
    for g in range(MOE_GROUPS):
        hit = gl[g] == gmax
        if taken is not None:
            hit = hit & jnp.logical_not(taken)
        taken = hit if taken is None else (taken | hit)
        sel.append(hit)
    denom = None
    for g in range(MOE_GROUPS):
        e = jnp.exp(gl[g] - gmax)
        denom = e if denom is None else denom + e
    g_wt = 1.0 / denom
    el = []
    for e in range(MOE_PER_GROUP):
        v = col(MOE_GROUPS + (MOE_GROUPS - 1) * MOE_PER_GROUP + e)
        for g in range(MOE_GROUPS - 2, -1, -1):
            v = jnp.where(sel[g], col(MOE_GROUPS + g * MOE_PER_GROUP + e), v)
        el.append(v)
    rank = []
    for e in range(MOE_PER_GROUP):
        r = jnp.zeros_like(el[e], dtype=jnp.int32)
        for m in range(MOE_PER_GROUP):
            if m == e:
                continue
            beats = (el[m] >= el[e]) if m < e else (el[m] > el[e])
            r = r + beats.astype(jnp.int32)
        rank.append(r)
    v1 = _max_of(el)
    v2 = None
    for e in range(MOE_PER_GROUP):
        c = jnp.where(rank[e] == 1, el[e], 0.0)
        v2 = c if v2 is None else v2 + c
    e2 = jnp.exp(v2 - v1)
    w1 = 1.0 / (1.0 + e2)
    w2 = e2 / (1.0 + e2)
    tokens = logits_t.shape[1]
    row_id = lax.broadcasted_iota(jnp.int32, (MOE_EXPERTS, tokens), 0)
    comb_t = jnp.zeros((MOE_EXPERTS, tokens), F32)
    for g in range(MOE_GROUPS):
        for e in range(MOE_PER_GROUP):
            within = jnp.where(rank[e] == 0, w1, jnp.where(rank[e] == 1, w2, 0.0))
            c = jnp.where(sel[g], g_wt * within, 0.0)
            comb_t = jnp.where(row_id == g * MOE_PER_GROUP + e, c, comb_t)
    pad = jnp.zeros((ROUTER_LANES - MOE_EXPERTS, tokens), F32)
    return jnp.concatenate([comb_t, pad], axis=0).T


def _max_of(cols):
    out = cols[0]
    for c in cols[1:]:
        out = jnp.maximum(out, c)
    return out


def _moe_kernel(x_ref, ys_ref, yp_ref, ya_ref, wo_ref, g_ref, wr_ref, br_ref, wgu_ref, wd_ref, fg_ref,
                o_ref, h_ref, comb_ref, *, final_norm):
    step = pl.program_id(1)

    @pl.when(step == 0)
    def _mix_and_route():
        s0, s1 = SSM_WIDTH, SSM_WIDTH + POOL_WIDTH
        mix = jnp.dot(ys_ref[...], wo_ref[0:s0, :], preferred_element_type=F32)
        mix = mix + jnp.dot(yp_ref[...], wo_ref[s0:s1, :], preferred_element_type=F32)
        mix = mix + jnp.dot(ya_ref[...], wo_ref[s1:, :], preferred_element_type=F32)
        x = x_ref[...] + mix
        h = _rms_norm(x, g_ref[...]).astype(BF16)
        h_ref[...] = h
        logits_t = lax.dot_general(wr_ref[...], h, _NT, preferred_element_type=F32)
        comb_ref[...] = _routing_weights(logits_t + br_ref[...])
        o_ref[...] = x

    h = h_ref[...]
    lane = lax.broadcasted_iota(jnp.int32, comb_ref.shape, 1)
    y = None
    for j in range(EXPERTS_PER_STEP):
        gu = jnp.dot(h, wgu_ref[j], preferred_element_type=F32)
        hg, hu = gu[:, :D_EXPERT], gu[:, D_EXPERT:]
        c = jnp.sum(jnp.where(lane == step * EXPERTS_PER_STEP + j, comb_ref[...], 0.0), axis=1, keepdims=True)
        act = (hg * (1.0 / (1.0 + jnp.exp(-hg)))) * hu * c
        yj = jnp.dot(act.astype(BF16), wd_ref[j], preferred_element_type=F32)
        y = yj if y is None else y + yj
    o_ref[...] += y

    if final_norm:
        @pl.when(step == MOE_EXPERTS // EXPERTS_PER_STEP - 1)
        def _final():
            o_ref[...] = _rms_norm(o_ref[...], fg_ref[...])


def _mix_and_moe(x2d, y_ssm_tm, y_pool, y_att, w_out, norm_g, w_router, b_router, w_gu, w_down, final_g,
                 final_norm):
    rows = x2d.shape[0]
    tiles_per_seq = y_ssm_tm.shape[0] // MOE_ROWS
    const = lambda i, e: (0, 0)
    row_blk = lambda width: pl.BlockSpec((MOE_ROWS, width), lambda i, e: (i, 0))
    return pl.pallas_call(
        partial(_moe_kernel, final_norm=final_norm),
        grid=(rows // MOE_ROWS, MOE_EXPERTS // EXPERTS_PER_STEP),
        in_specs=[
            row_blk(D_MODEL),
            pl.BlockSpec((MOE_ROWS, SSM_WIDTH), lambda i, e: (i % tiles_per_seq, i // tiles_per_seq)),
            row_blk(POOL_WIDTH),
            row_blk(ATT_WIDTH),
            pl.BlockSpec((D_MODEL, D_MODEL), const),
            pl.BlockSpec((1, D_MODEL), const),
            pl.BlockSpec((ROUTER_ROWS, D_MODEL), const),
            pl.BlockSpec((ROUTER_ROWS, 1), const),
            pl.BlockSpec((EXPERTS_PER_STEP, D_MODEL, 2 * D_EXPERT), lambda i, e: (e, 0, 0)),
            pl.BlockSpec((EXPERTS_PER_STEP, D_EXPERT, D_MODEL), lambda i, e: (e, 0, 0)),
            pl.BlockSpec((1, D_MODEL), const),
        ],
        out_specs=pl.BlockSpec((MOE_ROWS, D_MODEL), lambda i, e: (i, 0)),
        out_shape=jax.ShapeDtypeStruct((rows, D_MODEL), F32),
        scratch_shapes=[pltpu.VMEM((MOE_ROWS, D_MODEL), BF16), pltpu.VMEM((MOE_ROWS, ROUTER_LANES), F32)],
        compiler_params=pltpu.CompilerParams(dimension_semantics=("parallel", "arbitrary"),
                                             vmem_limit_bytes=VMEM_LIMIT),
        name="mix_and_moe",
    )(x2d, y_ssm_tm, y_pool, y_att, w_out, norm_g.reshape(1, D_MODEL), w_router, b_router, w_gu, w_down,
      final_g.reshape(1, D_MODEL))


def _router_params(group_w, group_b, router_w, router_b):
    w = jnp.concatenate([group_w.astype(F32)] + [router_w[g].astype(F32) for g in range(MOE_GROUPS)], axis=1)
    b = jnp.concatenate([group_b.astype(F32), router_b.astype(F32).reshape(-1)])
    pad = ROUTER_ROWS - w.shape[1]
    return jnp.pad(w.T, ((0, pad), (0, 0))).astype(BF16), jnp.pad(b, (0, pad)).reshape(ROUTER_ROWS, 1)


def kernel(x, rel_bias, norm1_g, w_in, ssm_a_re, ssm_a_im, ssm_log_dt, ssm_b_re, ssm_b_im, ssm_c_re, ssm_c_im,
           ssm_d, ssm_glu_w, ssm_glu_b, pool_w, pool_b, pool_scale, w_out, norm2_g, moe_group_w, moe_group_b,
           moe_router_w, moe_router_b, moe_w_gate, moe_w_up, moe_w_down, final_norm_g):
    bsz, seq, dm = x.shape
    depth = w_in.shape[0]
    assert dm == D_MODEL and depth >= 1
    assert bsz % SUBLANES == 0, "the scan keeps one time step of all batches in whole sublane tiles"
    assert all(seq % t == 0 for t in (PROJ_ROWS, MOE_ROWS, SSM_STEPS, MOBA_BLOCK))
    n_main = SSM_WIDTH + POOL_WIDTH + 2 * ATT_WIDTH
    x2d = x.astype(F32).reshape(bsz * seq, dm)
    tabs = _bias_tables(rel_bias)
    for l in range(depth):
        w_main = w_in[l, :, :n_main].astype(BF16)
        w_vt = w_in[l, :, n_main:].T.astype(BF16)
        u, y_pool, q, k, vt = _in_projection(x2d, norm1_g[l], w_main, w_vt,
                                             _pool_params(pool_w[l], pool_b[l], pool_scale[l]), bsz, seq)

        params = _ssm_params(ssm_a_re[l], ssm_a_im[l], ssm_log_dt[l], ssm_b_re[l], ssm_b_im[l],
                             ssm_c_re[l], ssm_c_im[l])
        y_ssm = _ssm_mixer(u, params, ssm_d[l], ssm_glu_w[l], ssm_glu_b[l], bsz, seq)

        y_att = _moba_attention(q.reshape(bsz, seq, ATT_WIDTH), k.reshape(bsz, seq, ATT_WIDTH), vt, tabs, rel_bias)

        w_router, b_router = _router_params(moe_group_w[l], moe_group_b[l], moe_router_w[l], moe_router_b[l])
        w_gu = jnp.concatenate([moe_w_gate[l], moe_w_up[l]], axis=-1).astype(BF16)
        x2d = _mix_and_moe(x2d, y_ssm, y_pool, y_att.reshape(bsz * seq, ATT_WIDTH),
                           w_out[l].astype(BF16), norm2_g[l], w_router, b_router, w_gu,
                           moe_w_down[l].astype(BF16), final_norm_g, final_norm=(l == depth - 1))
    return x2d.reshape(bsz, seq, dm).astype(x.dtype)
```

```python
import math
from functools import partial

import jax
import jax.numpy as jnp
from jax import lax
from jax.experimental import pallas as pl
from jax.experimental.pallas import tpu as pltpu

F32 = jnp.float32
BF16 = jnp.bfloat16

LANES = 128
SUBLANES = 8
D_MODEL = 1024
SSM_WIDTH = 256
POOL_WIDTH = 256
ATT_WIDTH = 512
SSM_GROUP = 16
SSM_GROUPS = SSM_WIDTH // SSM_GROUP
SSM_STATE = 64
SSM_STATES = SSM_GROUPS * SSM_STATE
POOL_WINDOWS = (2, 4, 8, 16)
POOL_GROUP = POOL_WIDTH // len(POOL_WINDOWS)
POOL_HALO = 16
HEAD_DIM = 64
HEADS = ATT_WIDTH // HEAD_DIM
HEAD_PAIR = 2 * HEAD_DIM
ONES_ROWS = 16
MOBA_BLOCK = 256
MOBA_TOPK = 3
FAR_GROUP = 4
KEY_PART = 128
SCORE_AHEAD = 5
REL_BUCKETS = 32
REL_MAX_EXACT = REL_BUCKETS // 2
REL_MAX_DIST = 128
MOE_GROUPS = 4
MOE_PER_GROUP = 4
MOE_EXPERTS = MOE_GROUPS * MOE_PER_GROUP
D_EXPERT = D_MODEL // 4
ROUTER_ROWS = 32
ROUTER_LANES = 128
RMS_EPS = 1e-6
NEG_INF = -1e30
TAKEN = -3e38
LOG2E = math.log2(math.e)

PROJ_ROWS = 1024
MOE_ROWS = 1024
EXPERTS_PER_STEP = 4
SSM_STEPS = 128
VMEM_LIMIT = 48 * 1024 * 1024

_NT = (((1,), (1,)), ((), ()))

_BUCKET_START = list(range(REL_MAX_EXACT)) + [
    math.ceil(REL_MAX_EXACT * (REL_MAX_DIST / REL_MAX_EXACT) ** (k / (REL_BUCKETS - REL_MAX_EXACT)))
    for k in range(REL_BUCKETS - REL_MAX_EXACT)
]


def _rms_norm(x, g):
    return x * lax.rsqrt(jnp.mean(x * x, axis=-1, keepdims=True) + RMS_EPS) * g


def _inproj_kernel(x_ref, g_ref, w_ref, wvt_ref, pw_ref, pb_ref, ps_ref,
                   u_ref, yp_ref, q_ref, k_ref, vt_ref, halo_ref, *, tiles_per_seq):
    h = _rms_norm(x_ref[...], g_ref[...]).astype(BF16)
    pr = jnp.dot(h, w_ref[...], preferred_element_type=F32)
    s0, s1, s2 = SSM_WIDTH, SSM_WIDTH + POOL_WIDTH, SSM_WIDTH + POOL_WIDTH + ATT_WIDTH
    u_ref[...] = pr[:, :s0]
    t_tile = pl.program_id(0) % tiles_per_seq
    p = pr[:, s0:s1]
    halo = jnp.where(t_tile == 0, 0.0, halo_ref[...])
    halo_ref[...] = p[PROJ_ROWS - POOL_HALO:, :]
    yp_ref[...] = _pool_mix(p, halo, t_tile * PROJ_ROWS, pw_ref[...], pb_ref[...], ps_ref[...]).astype(BF16)
    q_ref[...] = (pr[:, s1:s2] * (HEAD_DIM ** -0.5 * LOG2E)).astype(BF16)
    k_ref[...] = pr[:, s2:].astype(BF16)
    vt_ref[0] = lax.dot_general(wvt_ref[...], h, _NT, preferred_element_type=F32).astype(BF16)


def _pool_mix(p, halo, t0, w_bd, b, scale):
    ext = jnp.concatenate([halo, p], axis=0)
    s2 = ext + pltpu.roll(ext, 1, 0)
    s4 = s2 + pltpu.roll(s2, 2, 0)
    s8 = s4 + pltpu.roll(s4, 4, 0)
    s16 = s8 + pltpu.roll(s8, 8, 0)
    sums = (s2, s4, s8, s16)
    lane = lax.broadcasted_iota(jnp.int32, p.shape, 1)
    t1 = (lax.broadcasted_iota(jnp.int32, p.shape, 0) + t0 + 1).astype(F32)
    mean = None
    for gi, win in enumerate(POOL_WINDOWS):
        m = sums[gi][POOL_HALO:] / jnp.minimum(t1, float(win))
        mean = m if mean is None else jnp.where(lane >= gi * POOL_GROUP, m, mean)
    y = jnp.dot((mean - p).astype(BF16), w_bd, preferred_element_type=F32)
    return (y + b) * scale


def _pool_params(w, b, scale):
    eye = jnp.eye(len(POOL_WINDOWS), dtype=F32)
    w_bd = jnp.einsum('gcd,gk->gckd', w.astype(F32), eye).reshape(POOL_WIDTH, POOL_WIDTH).astype(BF16)
    return w_bd, b.astype(F32).reshape(1, POOL_WIDTH), scale.astype(F32).reshape(1, POOL_WIDTH)


def _in_projection(x2d, g, w_main, w_vt, pool_params, bsz, seq):
    rows = x2d.shape[0]
    tiles_per_seq = seq // PROJ_ROWS
    n_main = w_main.shape[1]
    const = lambda i: (0, 0)
    row_blk = lambda width: pl.BlockSpec((PROJ_ROWS, width), lambda i: (i, 0))
    return pl.pallas_call(
        partial(_inproj_kernel, tiles_per_seq=tiles_per_seq),
        grid=(rows // PROJ_ROWS,),
        in_specs=[row_blk(D_MODEL), pl.BlockSpec((1, D_MODEL), const),
                  pl.BlockSpec((D_MODEL, n_main), const), pl.BlockSpec((ATT_WIDTH, D_MODEL), const),
                  pl.BlockSpec((POOL_WIDTH, POOL_WIDTH), const), pl.BlockSpec((1, POOL_WIDTH), const),
                  pl.BlockSpec((1, POOL_WIDTH), const)],
        out_specs=[pl.BlockSpec((PROJ_ROWS, SSM_WIDTH), lambda i: (i % tiles_per_seq, i // tiles_per_seq)),
                   row_blk(POOL_WIDTH), row_blk(ATT_WIDTH), row_blk(ATT_WIDTH),
                   pl.BlockSpec((1, ATT_WIDTH, PROJ_ROWS),
                                lambda i: (i // tiles_per_seq, 0, i % tiles_per_seq))],
        out_shape=[jax.ShapeDtypeStruct((seq, bsz * SSM_WIDTH), F32),
                   jax.ShapeDtypeStruct((rows, POOL_WIDTH), BF16),
                   jax.ShapeDtypeStruct((rows, ATT_WIDTH), BF16),
                   jax.ShapeDtypeStruct((rows, ATT_WIDTH), BF16),
                   jax.ShapeDtypeStruct((bsz, ATT_WIDTH, seq), BF16)],
        scratch_shapes=[pltpu.VMEM((POOL_HALO, POOL_WIDTH), F32)],
        compiler_params=pltpu.CompilerParams(dimension_semantics=("arbitrary",),
                                             vmem_limit_bytes=VMEM_LIMIT),
        name="in_projection",
    )(x2d, g.reshape(1, D_MODEL), w_main, w_vt, *pool_params)


def _ssm_kernel(u_ref, bm_ref, cm_ref, ar_ref, ai_ref, d_ref, gw_ref, gb_ref, o_ref, bu_ref, st_ref, tb_ref):
    bsz = st_ref.shape[0]
    ns = SSM_STATES

    @pl.when(pl.program_id(0) == 0)
    def _init():
        st_ref[...] = jnp.zeros_like(st_ref)

    halves = SSM_WIDTH // LANES
    for b in range(bsz):
        for h in range(halves):
            c0 = b * SSM_WIDTH + h * LANES
            tb_ref[h, pl.ds(b, SSM_STEPS, stride=bsz), :] = u_ref[:, c0:c0 + LANES]
    u = jnp.concatenate([tb_ref[h] for h in range(halves)], axis=1)
    bu_ref[...] = jnp.dot(u.astype(BF16), bm_ref[...], preferred_element_type=F32)
    ar = jnp.broadcast_to(ar_ref[...], (bsz, ns))
    ai = jnp.broadcast_to(ai_ref[...], (bsz, ns))

    def step(t, carry):
        sr, si = carry
        r0 = pl.multiple_of(t * bsz, bsz)
        nr = ar * sr - ai * si + bu_ref[pl.ds(r0, bsz), 0:ns]
        ni = ar * si + ai * sr + bu_ref[pl.ds(r0, bsz), ns:2 * ns]
        bu_ref[pl.ds(r0, bsz), 0:ns] = nr
        bu_ref[pl.ds(r0, bsz), ns:2 * ns] = ni
        return nr, ni

    sr, si = lax.fori_loop(0, SSM_STEPS, step, (st_ref[:, 0:ns], st_ref[:, ns:2 * ns]), unroll=4)
    st_ref[:, 0:ns] = sr
    st_ref[:, ns:2 * ns] = si

    y = jnp.dot(bu_ref[...].astype(BF16), cm_ref[...], preferred_element_type=F32)
    y = y + d_ref[...] * u
    y = y * (0.5 * (1.0 + jnp.tanh(math.sqrt(2.0 / math.pi) * (y + 0.044715 * (y * y * y)))))
    z = jnp.dot(y.astype(BF16), gw_ref[...], preferred_element_type=F32) + gb_ref[...]
    out = y * (1.0 / (1.0 + jnp.exp(-z)))
    for h in range(halves):
        tb_ref[h] = out[:, h * LANES:(h + 1) * LANES]
    for b in range(bsz):
        for h in range(halves):
            c0 = b * SSM_WIDTH + h * LANES
            o_ref[:, c0:c0 + LANES] = tb_ref[h, pl.ds(b, SSM_STEPS, stride=bsz), :].astype(o_ref.dtype)


def _ssm_params(a_re, a_im, log_dt, b_re, b_im, c_re, c_im):
    lam = lax.complex(a_re.astype(F32), a_im.astype(F32))
    dt = jnp.exp(log_dt.astype(F32))[:, None]
    lam_bar = jnp.exp(lam * dt)
    b = lax.complex(b_re.astype(F32), b_im.astype(F32))
    b_bar = ((lam_bar - 1.0) / lam)[..., None] * b
    eye = jnp.eye(SSM_GROUPS, dtype=F32)

    def bdiag_in(m):
        return jnp.einsum('gph,gk->ghkp', m, eye).reshape(SSM_WIDTH, SSM_STATES)

    def bdiag_out(m):
        return jnp.einsum('ghp,gk->gpkh', m, eye).reshape(SSM_STATES, SSM_WIDTH)

    bm = jnp.concatenate([bdiag_in(jnp.real(b_bar)), bdiag_in(jnp.imag(b_bar))], axis=1)
    cm = jnp.concatenate([bdiag_out(c_re.astype(F32)), -bdiag_out(c_im.astype(F32))], axis=0)
    return (bm.astype(BF16), cm.astype(BF16),
            jnp.real(lam_bar).reshape(1, SSM_STATES), jnp.imag(lam_bar).reshape(1, SSM_STATES))


def _ssm_mixer(u_tm, params, d, glu_w, glu_b, bsz, seq):
    bm, cm, ar, ai = params
    rows = SSM_STEPS * bsz
    const = lambda c: (0, 0)
    return pl.pallas_call(
        _ssm_kernel,
        grid=(seq // SSM_STEPS,),
        in_specs=[
            pl.BlockSpec((SSM_STEPS, bsz * SSM_WIDTH), lambda c: (c, 0)),
            pl.BlockSpec((SSM_WIDTH, 2 * SSM_STATES), const),
            pl.BlockSpec((2 * SSM_STATES, SSM_WIDTH), const),
            pl.BlockSpec((1, SSM_STATES), const),
            pl.BlockSpec((1, SSM_STATES), const),
            pl.BlockSpec((1, SSM_WIDTH), const),
            pl.BlockSpec((SSM_WIDTH, SSM_WIDTH), const),
            pl.BlockSpec((1, SSM_WIDTH), const),
        ],
        out_specs=pl.BlockSpec((SSM_STEPS, bsz * SSM_WIDTH), lambda c: (c, 0)),
        out_shape=jax.ShapeDtypeStruct((seq, bsz * SSM_WIDTH), BF16),
        scratch_shapes=[pltpu.VMEM((rows, 2 * SSM_STATES), F32), pltpu.VMEM((bsz, 2 * SSM_STATES), F32),
                        pltpu.VMEM((SSM_WIDTH // LANES, rows, LANES), F32)],
        compiler_params=pltpu.CompilerParams(dimension_semantics=("arbitrary",),
                                             vmem_limit_bytes=VMEM_LIMIT),
        name="ssm_mixer",
    )(u_tm, bm, cm, ar, ai, d.reshape(1, SSM_WIDTH), glu_w.astype(BF16), glu_b.reshape(1, SSM_WIDTH))


def _bias_table_kernel(bias_ref, tab_ref):
    h = pl.program_id(0)
    kk = lax.broadcasted_iota(jnp.int32, (MOBA_BLOCK, MOBA_BLOCK), 0)
    qq = lax.broadcasted_iota(jnp.int32, (MOBA_BLOCK, MOBA_BLOCK), 1)
    for which in range(2):
        rel = qq - kk + which * MOBA_BLOCK
        val = jnp.full((MOBA_BLOCK, MOBA_BLOCK), bias_ref[0, h], F32)
        for b in range(1, REL_BUCKETS):
            val = jnp.where(rel >= _BUCKET_START[b], bias_ref[b, h], val)
        tab_ref[0, which] = jnp.where(rel >= 0, val * LOG2E, NEG_INF)


def _bias_tables(rel_bias):
    return pl.pallas_call(
        _bias_table_kernel,
        grid=(HEADS,),
        in_specs=[pl.BlockSpec(memory_space=pltpu.SMEM)],
        out_specs=pl.BlockSpec((1, 2, MOBA_BLOCK, MOBA_BLOCK), lambda h: (h, 0, 0, 0)),
        out_shape=jax.ShapeDtypeStruct((HEADS, 2, MOBA_BLOCK, MOBA_BLOCK), F32),
        name="bias_tables",
    )(rel_bias.astype(F32))


def _attn_kernel(bias_ref, q_ref, k_ref, vt_ref, tab_ref, o_ref,
                 qh_ref, va_ref, neg_ref, acc_ref, m_ref, s_ref):
    seq = k_ref.shape[1]
    nb = seq // MOBA_BLOCK

    def prep():
        lane = lax.broadcasted_iota(jnp.int32, (1, HEAD_PAIR), 1)
        head_lanes = [lane < HEAD_DIM, lane >= HEAD_DIM]
        kf = k_ref[0].astype(F32).reshape(nb, MOBA_BLOCK, HEAD_PAIR)
        km = jnp.sum(kf, axis=1) * (1.0 / MOBA_BLOCK)
        q_all = q_ref[0].astype(F32)
        ones = jnp.ones((ONES_ROWS, seq), BF16)
        va_ref[0, 0:HEAD_DIM, :] = vt_ref[0, 0:HEAD_DIM, :]
        va_ref[0, HEAD_DIM:, :] = ones
        va_ref[1, 0:ONES_ROWS, :] = ones
        va_ref[1, ONES_ROWS:, :] = vt_ref[0, HEAD_DIM:, :]
        blk_id = lax.broadcasted_iota(jnp.int32, (nb, seq), 0)
        past = blk_id < lax.broadcasted_iota(jnp.int32, (nb, seq), 1) // MOBA_BLOCK
        for hh in range(2):
            qh_ref[hh] = jnp.where(head_lanes[hh], q_all, 0.0).astype(BF16)
            km_h = jnp.where(head_lanes[hh], km, 0.0).astype(BF16)
            g = lax.dot_general(km_h, qh_ref[hh], _NT, preferred_element_type=F32)
            g = jnp.where(past, g, NEG_INF)
            sel = None
            for _ in range(min(MOBA_TOPK, nb)):
                top = jnp.max(g, axis=0, keepdims=True)
                first = jnp.min(jnp.where(g == top, blk_id, nb), axis=0, keepdims=True)
                hit = blk_id == first
                sel = hit if sel is None else (sel | hit)
                g = jnp.where(hit, TAKEN, g)
            neg = jnp.where(sel & past, 0.0, NEG_INF)
            for qb in range(nb):
                neg_ref[hh, qb] = neg[:, qb * MOBA_BLOCK:(qb + 1) * MOBA_BLOCK]

    prep()

    def q_block(qi, carry):
        _attn_q_block(qi, pl.program_id(1), nb, bias_ref, k_ref, tab_ref, o_ref,
                      qh_ref, va_ref, neg_ref, acc_ref, m_ref, s_ref)
        return carry

    lax.fori_loop(0, nb, q_block, 0)


def _attn_q_block(qi, pr, nb, bias_ref, k_ref, tab_ref, o_ref,
                  qh_ref, va_ref, neg_ref, acc_ref, m_ref, s_ref):
    qh = [qh_ref[hh, pl.ds(pl.multiple_of(qi * MOBA_BLOCK, MOBA_BLOCK), MOBA_BLOCK), :] for hh in range(2)]

    def key_rows(j):
        return pl.ds(pl.multiple_of(j * MOBA_BLOCK, MOBA_BLOCK), MOBA_BLOCK)

    def part_rows(j, part):
        return pl.ds(pl.multiple_of(j * MOBA_BLOCK + part * KEY_PART, KEY_PART), KEY_PART)

    def attend(items, m_prev):
        units = [(hh, i, part) for i in range(len(items)) for part in range(MOBA_BLOCK // KEY_PART)
                 for hh in range(2)]
        scores, mbs, pvs, staged = {}, {}, {}, 0
        for n in range(len(units) + SCORE_AHEAD):
            if n < len(units):
                hh, i, part = units[n]
                scores[n] = lax.dot_general(k_ref[0, part_rows(items[i][0], part), :], qh[hh], _NT,
                                            preferred_element_type=F32)
            c = n - SCORE_AHEAD
            if c < 0:
                continue
            hh, i, part = units[c]
            s, add = scores.pop(c), items[i][1][hh]
            if add.shape[0] == 1:
                smax = jnp.max(s, axis=0, keepdims=True)
                mbs[c] = smax + add
                p = jnp.exp2(s - smax)
            else:
                s_ref[staged] = s
                s = s_ref[staged] + add[part * KEY_PART:(part + 1) * KEY_PART]
                staged += 1
                mbs[c] = jnp.max(s, axis=0, keepdims=True)
                p = jnp.exp2(s - mbs[c])
            pvs[c] = jnp.dot(va_ref[hh, :, part_rows(items[i][0], part)], p.astype(BF16),
                             preferred_element_type=F32)
        out = []
        for hh in range(2):
            mine = [c for c, u in enumerate(units) if u[0] == hh]
            m_new = None if m_prev is None else m_prev[hh]
            for c in mine:
                m_new = mbs[c] if m_new is None else jnp.maximum(m_new, mbs[c])
            acc = None if m_prev is None else acc_ref[hh] * jnp.exp2(m_prev[hh] - m_new)
            for c in mine:
                term = pvs[c] * jnp.exp2(mbs[c] - m_new)
                acc = term if acc is None else acc + term
            acc_ref[hh] = acc
            out.append(m_new)
        return tuple(out)

    def store_max(ms):
        m_ref[0] = ms[0]
        m_ref[1] = ms[1]

    def finalize(qb):
        oa = acc_ref[0]
        ob = acc_ref[1]
        out_t = jnp.concatenate([oa[0:HEAD_DIM] / oa[HEAD_DIM:HEAD_DIM + 1],
                                 ob[ONES_ROWS:] / ob[0:1]], axis=0)
        o_ref[0, key_rows(qb), :] = out_t.T.astype(o_ref.dtype)

    def own_item():
        return (qi, [tab_ref[hh, 0] for hh in range(2)])

    def prev_item():
        return (qi - 1, [tab_ref[hh, 1] + neg_ref[hh, qi, pl.ds(qi - 1, 1), :] for hh in range(2)])

    def far_item(j):
        return (j, [neg_ref[hh, qi, pl.ds(j, 1), :] + bias_ref[REL_BUCKETS - 1, 2 * pr + hh] * LOG2E
                    for hh in range(2)])

    n_far = jnp.maximum(qi - 1, 0)
    n_rem = n_far % FAR_GROUP
    n_groups = n_far // FAR_GROUP

    def group_blocks(g):
        return [n_rem + g * FAR_GROUP + b for b in range(FAR_GROUP)]

    @pl.when(qi == 0)
    def _first_block():
        store_max(attend([own_item()], None))

    for r in range(FAR_GROUP):
        @pl.when((qi > 0) & (n_rem == r))
        def _near_blocks(r=r):
            finalize(qi - 1)
            store_max(attend([own_item(), prev_item()] + [far_item(j) for j in range(r)], None))

    def far_group(g, ms):
        return attend([far_item(j) for j in group_blocks(g)], ms)

    store_max(lax.fori_loop(0, n_groups, far_group, (m_ref[0], m_ref[1])))

    @pl.when(qi == nb - 1)
    def _last_block():
        finalize(qi)


def _moba_attention(q, k, vt, tabs, rel_bias):
    bsz, seq, _ = q.shape
    assert seq % MOBA_BLOCK == 0
    nb = seq // MOBA_BLOCK
    return pl.pallas_call(
        _attn_kernel,
        grid=(bsz, HEADS // 2),
        in_specs=[
            pl.BlockSpec(memory_space=pltpu.SMEM),
            pl.BlockSpec((1, seq, HEAD_PAIR), lambda b, p: (b, 0, p)),
            pl.BlockSpec((1, seq, HEAD_PAIR), lambda b, p: (b, 0, p)),
            pl.BlockSpec((1, HEAD_PAIR, seq), lambda b, p: (b, p, 0)),
            pl.BlockSpec((2, 2, MOBA_BLOCK, MOBA_BLOCK), lambda b, p: (p, 0, 0, 0)),
        ],
        out_specs=pl.BlockSpec((1, seq, HEAD_PAIR), lambda b, p: (b, 0, p)),
        out_shape=jax.ShapeDtypeStruct((bsz, seq, ATT_WIDTH), BF16),
        scratch_shapes=[
            pltpu.VMEM((2, seq, HEAD_PAIR), BF16),
            pltpu.VMEM((2, HEAD_DIM + ONES_ROWS, seq), BF16),
            pltpu.VMEM((2, nb, nb, MOBA_BLOCK), F32),
            pltpu.VMEM((2, HEAD_DIM + ONES_ROWS, MOBA_BLOCK), F32),
            pltpu.VMEM((2, 1, MOBA_BLOCK), F32),
            pltpu.VMEM((4 * MOBA_BLOCK // KEY_PART, KEY_PART, MOBA_BLOCK), F32),
        ],
        compiler_params=pltpu.CompilerParams(
            dimension_semantics=("parallel", "parallel"), vmem_limit_bytes=VMEM_LIMIT),
        name="moba_attention",
    )(rel_bias.astype(F32), q, k, vt, tabs)


def _routing_weights(logits_t):
    col = lambda i: logits_t[i:i + 1, :]
    gl = [col(g) for g in range(MOE_GROUPS)]
    gmax = _max_of(gl)
    sel, taken = [], None
    for g in range(MOE_GROUPS):
        hit = gl[g] == gmax
        if taken is not None:
            hit = hit & jnp.logical_not(taken)
        taken = hit if taken is None else (taken | hit)
        sel.append(hit)
    denom = None
    for g in range(MOE_GROUPS):
        e = jnp.exp(gl[g] - gmax)
        denom = e if denom is None else denom + e
    g_wt = 1.0 / denom
    el = []
    for e in range(MOE_PER_GROUP):
        v = col(MOE_GROUPS + (MOE_GROUPS - 1) * MOE_PER_GROUP + e)
        for g in range(MOE_GROUPS - 2, -1, -1):
            v = jnp.where(sel[g], col(MOE_GROUPS + g * MOE_PER_GROUP + e), v)
        el.append(v)
    rank = []
    for e in range(MOE_PER_GROUP):
        r = jnp.zeros_like(el[e], dtype=jnp.int32)
        for m in range(MOE_PER_GROUP):
            if m == e:
                continue
            beats = (el[m] >= el[e]) if m < e else (el[m] > el[e])
            r = r + beats.astype(jnp.int32)
        rank.append(r)
    v1 = _max_of(el)
    v2 = None
    for e in range(MOE_PER_GROUP):
        c = jnp.where(rank[e] == 1, el[e], 0.0)
        v2 = c if v2 is None else v2 + c
    e2 = jnp.exp(v2 - v1)
    w1 = 1.0 / (1.0 + e2)
    w2 = e2 / (1.0 + e2)
    tokens = logits_t.shape[1]
    row_id = lax.broadcasted_iota(jnp.int32, (MOE_EXPERTS, tokens), 0)
    comb_t = jnp.zeros((MOE_EXPERTS, tokens), F32)
    for g in range(MOE_GROUPS):
        for e in range(MOE_PER_GROUP):
            within = jnp.where(rank[e] == 0, w1, jnp.where(rank[e] == 1, w2, 0.0))
            c = jnp.where(sel[g], g_wt * within, 0.0)
            comb_t = jnp.where(row_id == g * MOE_PER_GROUP + e, c, comb_t)
    pad = jnp.zeros((ROUTER_LANES - MOE_EXPERTS, tokens), F32)
    return jnp.concatenate([comb_t, pad], axis=0).T


def _max_of(cols):
    out = cols[0]
    for c in cols[1:]:
        out = jnp.maximum(out, c)
    return out


def _moe_kernel(x_ref, ys_ref, yp_ref, ya_ref, wo_ref, g_ref, wr_ref, br_ref, wgu_ref, wd_ref, fg_ref,
                o_ref, h_ref, comb_ref, *, final_norm):
    step = pl.program_id(1)

    @pl.when(step == 0)
    def _mix_and_route():
        s0, s1 = SSM_WIDTH, SSM_WIDTH + POOL_WIDTH
        mix = jnp.dot(ys_ref[...], wo_ref[0:s0, :], preferred_element_type=F32)
        mix = mix + jnp.dot(yp_ref[...], wo_ref[s0:s1, :], preferred_element_type=F32)
        mix = mix + jnp.dot(ya_ref[...], wo_ref[s1:, :], preferred_element_type=F32)
        x = x_ref[...] + mix
        h = _rms_norm(x, g_ref[...]).astype(BF16)
        h_ref[...] = h
        logits_t = lax.dot_general(wr_ref[...], h, _NT, preferred_element_type=F32)
        comb_ref[...] = _routing_weights(logits_t + br_ref[...])
        o_ref[...] = x

    h = h_ref[...]
    lane = lax.broadcasted_iota(jnp.int32, comb_ref.shape, 1)
    y = None
    for j in range(EXPERTS_PER_STEP):
        gu = jnp.dot(h, wgu_ref[j], preferred_element_type=F32)
        hg, hu = gu[:, :D_EXPERT], gu[:, D_EXPERT:]
        c = jnp.sum(jnp.where(lane == step * EXPERTS_PER_STEP + j, comb_ref[...], 0.0), axis=1, keepdims=True)
        act = (hg * (1.0 / (1.0 + jnp.exp(-hg)))) * hu * c
        yj = jnp.dot(act.astype(BF16), wd_ref[j], preferred_element_type=F32)
        y = yj if y is None else y + yj
    o_ref[...] += y

    if final_norm:
        @pl.when(step == MOE_EXPERTS // EXPERTS_PER_STEP - 1)
        def _final():
            o_ref[...] = _rms_norm(o_ref[...], fg_ref[...])


def _mix_and_moe(x2d, y_ssm_tm, y_pool, y_att, w_out, norm_g, w_router, b_router, w_gu, w_down, final_g,
                 final_norm):
    rows = x2d.shape[0]
    tiles_per_seq = y_ssm_tm.shape[0] // MOE_ROWS
    const = lambda i, e: (0, 0)
    row_blk = lambda width: pl.BlockSpec((MOE_ROWS, width), lambda i, e: (i, 0))
    return pl.pallas_call(
        partial(_moe_kernel, final_norm=final_norm),
        grid=(rows // MOE_ROWS, MOE_EXPERTS // EXPERTS_PER_STEP),
        in_specs=[
            row_blk(D_MODEL),
            pl.BlockSpec((MOE_ROWS, SSM_WIDTH), lambda i, e: (i % tiles_per_seq, i // tiles_per_seq)),
            row_blk(POOL_WIDTH),
            row_blk(ATT_WIDTH),
            pl.BlockSpec((D_MODEL, D_MODEL), const),
            pl.BlockSpec((1, D_MODEL), const),
            pl.BlockSpec((ROUTER_ROWS, D_MODEL), const),
            pl.BlockSpec((ROUTER_ROWS, 1), const),
            pl.BlockSpec((EXPERTS_PER_STEP, D_MODEL, 2 * D_EXPERT), lambda i, e: (e, 0, 0)),
            pl.BlockSpec((EXPERTS_PER_STEP, D_EXPERT, D_MODEL), lambda i, e: (e, 0, 0)),
            pl.BlockSpec((1, D_MODEL), const),
        ],
        out_specs=pl.BlockSpec((MOE_ROWS, D_MODEL), lambda i, e: (i, 0)),
        out_shape=jax.ShapeDtypeStruct((rows, D_MODEL), F32),
        scratch_shapes=[pltpu.VMEM((MOE_ROWS, D_MODEL), BF16), pltpu.VMEM((MOE_ROWS, ROUTER_LANES), F32)],
        compiler_params=pltpu.CompilerParams(dimension_semantics=("parallel", "arbitrary"),
                                             vmem_limit_bytes=VMEM_LIMIT),
        name="mix_and_moe",
    )(x2d, y_ssm_tm, y_pool, y_att, w_out, norm_g.reshape(1, D_MODEL), w_router, b_router, w_gu, w_down,
      final_g.reshape(1, D_MODEL))


def _router_params(group_w, group_b, router_w, router_b):
    w = jnp.concatenate([group_w.astype(F32)] + [router_w[g].astype(F32) for g in range(MOE_GROUPS)], axis=1)
    b = jnp.concatenate([group_b.astype(F32), router_b.astype(F32).reshape(-1)])
    pad = ROUTER_ROWS - w.shape[1]
    return jnp.pad(w.T, ((0, pad), (0, 0))).astype(BF16), jnp.pad(b, (0, pad)).reshape(ROUTER_ROWS, 1)


def kernel(x, rel_bias, norm1_g, w_in, ssm_a_re, ssm_a_im, ssm_log_dt, ssm_b_re, ssm_b_im, ssm_c_re, ssm_c_im,
           ssm_d, ssm_glu_w, ssm_glu_b, pool_w, pool_b, pool_scale, w_out, norm2_g, moe_group_w, moe_group_b,
           moe_router_w, moe_router_b, moe_w_gate, moe_w_up, moe_w_down, final_norm_g):
    bsz, seq, dm = x.shape
    depth = w_in.shape[0]
    assert dm == D_MODEL and depth >= 1
    assert bsz % SUBLANES == 0, "the scan keeps one time step of all batches in whole sublane tiles"
    assert all(seq % t == 0 for t in (PROJ_ROWS, MOE_ROWS, SSM_STEPS, MOBA_BLOCK))
    n_main = SSM_WIDTH + POOL_WIDTH + 2 * ATT_WIDTH
    x2d = x.astype(F32).reshape(bsz * seq, dm)
    tabs = _bias_tables(rel_bias)
    for l in range(depth):
        w_main = w_in[l, :, :n_main].astype(BF16)
        w_vt = w_in[l, :, n_main:].T.astype(BF16)
        u, y_pool, q, k, vt = _in_projection(x2d, norm1_g[l], w_main, w_vt,
                                             _pool_params(pool_w[l], pool_b[l], pool_scale[l]), bsz, seq)

        params = _ssm_params(ssm_a_re[l], ssm_a_im[l], ssm_log_dt[l], ssm_b_re[l], ssm_b_im[l],
                             ssm_c_re[l], ssm_c_im[l])
        y_ssm = _ssm_mixer(u, params, ssm_d[l], ssm_glu_w[l], ssm_glu_b[l], bsz, seq)

        y_att = _moba_attention(q.reshape(bsz, seq, ATT_WIDTH), k.reshape(bsz, seq, ATT_WIDTH), vt, tabs, rel_bias)

        w_router, b_router = _router_params(moe_group_w[l], moe_group_b[l], moe_router_w[l], moe_router_b[l])
        w_gu = jnp.concatenate([moe_w_gate[l], moe_w_up[l]], axis=-1).astype(BF16)
        x2d = _mix_and_moe(x2d, y_ssm, y_pool, y_att.reshape(bsz * seq, ATT_WIDTH),
                           w_out[l].astype(BF16), norm2_g[l], w_router, b_router, w_gu,
                           moe_w_down[l].astype(BF16), final_norm_g, final_norm=(l == depth - 1))
    return x2d.reshape(bsz, seq, dm).astype(x.dtype)
```

```python
import math
from functools import partial

import jax
import jax.numpy as jnp
from jax import lax
from jax.experimental import pallas as pl
from jax.experimental.pallas import tpu as pltpu

F32 = jnp.float32
BF16 = jnp.bfloat16

LANES = 128
SUBLANES = 8
D_MODEL = 1024
SSM_WIDTH = 256
POOL_WIDTH = 256
ATT_WIDTH = 512
SSM_GROUP = 16
SSM_GROUPS = SSM_WIDTH // SSM_GROUP
SSM_STATE = 64
SSM_STATES = SSM_GROUPS * SSM_STATE
POOL_WINDOWS = (2, 4, 8, 16)
POOL_GROUP = POOL_WIDTH // len(POOL_WINDOWS)
POOL_HALO = 16
HEAD_DIM = 64
HEADS = ATT_WIDTH // HEAD_DIM
HEAD_PAIR = 2 * HEAD_DIM
ONES_ROWS = 16
MOBA_BLOCK = 256
MOBA_TOPK = 3
FAR_GROUP = 4
REL_BUCKETS = 32
REL_MAX_EXACT = REL_BUCKETS // 2
REL_MAX_DIST = 128
MOE_GROUPS = 4
MOE_PER_GROUP = 4
MOE_EXPERTS = MOE_GROUPS * MOE_PER_GROUP
D_EXPERT = D_MODEL // 4
ROUTER_ROWS = 32
ROUTER_LANES = 128
RMS_EPS = 1e-6
NEG_INF = -1e30
TAKEN = -3e38
LOG2E = math.log2(math.e)

PROJ_ROWS = 1024
MOE_ROWS = 1024
EXPERTS_PER_STEP = 4
SSM_STEPS = 128
VMEM_LIMIT = 48 * 1024 * 1024

_NT = (((1,), (1,)), ((), ()))

_BUCKET_START = list(range(REL_MAX_EXACT)) + [
    math.ceil(REL_MAX_EXACT * (REL_MAX_DIST / REL_MAX_EXACT) ** (k / (REL_BUCKETS - REL_MAX_EXACT)))
    for k in range(REL_BUCKETS - REL_MAX_EXACT)
]


def _rms_norm(x, g):
    return x * lax.rsqrt(jnp.mean(x * x, axis=-1, keepdims=True) + RMS_EPS) * g


def _inproj_kernel(x_ref, g_ref, w_ref, wvt_ref, pw_ref, pb_ref, ps_ref,
                   u_ref, yp_ref, q_ref, k_ref, vt_ref, halo_ref, *, tiles_per_seq):
    h = _rms_norm(x_ref[...], g_ref[...]).astype(BF16)
    pr = jnp.dot(h, w_ref[...], preferred_element_type=F32)
    s0, s1, s2 = SSM_WIDTH, SSM_WIDTH + POOL_WIDTH, SSM_WIDTH + POOL_WIDTH + ATT_WIDTH
    u_ref[...] = pr[:, :s0]
    t_tile = pl.program_id(0) % tiles_per_seq
    p = pr[:, s0:s1]
    halo = jnp.where(t_tile == 0, 0.0, halo_ref[...])
    halo_ref[...] = p[PROJ_ROWS - POOL_HALO:, :]
    yp_ref[...] = _pool_mix(p, halo, t_tile * PROJ_ROWS, pw_ref[...], pb_ref[...], ps_ref[...]).astype(BF16)
    q_ref[...] = (pr[:, s1:s2] * (HEAD_DIM ** -0.5 * LOG2E)).astype(BF16)
    k_ref[...] = pr[:, s2:].astype(BF16)
    vt_ref[0] = lax.dot_general(wvt_ref[...], h, _NT, preferred_element_type=F32).astype(BF16)


def _pool_mix(p, halo, t0, w_bd, b, scale):
    ext = jnp.concatenate([halo, p], axis=0)
    s2 = ext + pltpu.roll(ext, 1, 0)
    s4 = s2 + pltpu.roll(s2, 2, 0)
    s8 = s4 + pltpu.roll(s4, 4, 0)
    s16 = s8 + pltpu.roll(s8, 8, 0)
    sums = (s2, s4, s8, s16)
    lane = lax.broadcasted_iota(jnp.int32, p.shape, 1)
    t1 = (lax.broadcasted_iota(jnp.int32, p.shape, 0) + t0 + 1).astype(F32)
    mean = None
    for gi, win in enumerate(POOL_WINDOWS):
        m = sums[gi][POOL_HALO:] / jnp.minimum(t1, float(win))
        mean = m if mean is None else jnp.where(lane >= gi * POOL_GROUP, m, mean)
    y = jnp.dot((mean - p).astype(BF16), w_bd, preferred_element_type=F32)
    return (y + b) * scale


def _pool_params(w, b, scale):
    eye = jnp.eye(len(POOL_WINDOWS), dtype=F32)
    w_bd = jnp.einsum('gcd,gk->gckd', w.astype(F32), eye).reshape(POOL_WIDTH, POOL_WIDTH).astype(BF16)
    return w_bd, b.astype(F32).reshape(1, POOL_WIDTH), scale.astype(F32).reshape(1, POOL_WIDTH)


def _in_projection(x2d, g, w_in, w_vt, layer, pool_params, bsz, seq):
    rows = x2d.shape[0]
    tiles_per_seq = seq // PROJ_ROWS
    n_main = SSM_WIDTH + POOL_WIDTH + 2 * ATT_WIDTH
    const = lambda i: (0, 0)
    this_layer = lambda i: (layer, 0, 0)
    row_blk = lambda width: pl.BlockSpec((PROJ_ROWS, width), lambda i: (i, 0))
    return pl.pallas_call(
        partial(_inproj_kernel, tiles_per_seq=tiles_per_seq),
        grid=(rows // PROJ_ROWS,),
        in_specs=[row_blk(D_MODEL), pl.BlockSpec((1, D_MODEL), const),
                  pl.BlockSpec((None, D_MODEL, n_main), this_layer),
                  pl.BlockSpec((None, ATT_WIDTH, D_MODEL), this_layer),
                  pl.BlockSpec((POOL_WIDTH, POOL_WIDTH), const), pl.BlockSpec((1, POOL_WIDTH), const),
                  pl.BlockSpec((1, POOL_WIDTH), const)],
        out_specs=[pl.BlockSpec((PROJ_ROWS, SSM_WIDTH), lambda i: (i % tiles_per_seq, i // tiles_per_seq)),
                   row_blk(POOL_WIDTH), row_blk(ATT_WIDTH), row_blk(ATT_WIDTH),
                   pl.BlockSpec((1, ATT_WIDTH, PROJ_ROWS),
                                lambda i: (i // tiles_per_seq, 0, i % tiles_per_seq))],
        out_shape=[jax.ShapeDtypeStruct((seq, bsz * SSM_WIDTH), F32),
                   jax.ShapeDtypeStruct((rows, POOL_WIDTH), BF16),
                   jax.ShapeDtypeStruct((rows, ATT_WIDTH), BF16),
                   jax.ShapeDtypeStruct((rows, ATT_WIDTH), BF16),
                   jax.ShapeDtypeStruct((bsz, ATT_WIDTH, seq), BF16)],
        scratch_shapes=[pltpu.VMEM((POOL_HALO, POOL_WIDTH), F32)],
        compiler_params=pltpu.CompilerParams(dimension_semantics=("arbitrary",),
                                             vmem_limit_bytes=VMEM_LIMIT),
        name="in_projection",
    )(x2d, g.reshape(1, D_MODEL), w_in, w_vt, *pool_params)


def _ssm_kernel(u_ref, bm_ref, cm_ref, ar_ref, ai_ref, d_ref, gw_ref, gb_ref, o_ref, bu_ref, st_ref, tb_ref):
    bsz = st_ref.shape[0]
    ns = SSM_STATES

    @pl.when(pl.program_id(0) == 0)
    def _init():
        st_ref[...] = jnp.zeros_like(st_ref)

    halves = SSM_WIDTH // LANES
    for b in range(bsz):
        for h in range(halves):
            c0 = b * SSM_WIDTH + h * LANES
            tb_ref[h, pl.ds(b, SSM_STEPS, stride=bsz), :] = u_ref[:, c0:c0 + LANES]
    u = jnp.concatenate([tb_ref[h] for h in range(halves)], axis=1)
    bu_ref[...] = jnp.dot(u.astype(BF16), bm_ref[...], preferred_element_type=F32)
    ar = jnp.broadcast_to(ar_ref[...], (bsz, ns))
    ai = jnp.broadcast_to(ai_ref[...], (bsz, ns))

    def step(t, carry):
        sr, si = carry
        r0 = pl.multiple_of(t * bsz, bsz)
        nr = ar * sr - ai * si + bu_ref[pl.ds(r0, bsz), 0:ns]
        ni = ar * si + ai * sr + bu_ref[pl.ds(r0, bsz), ns:2 * ns]
        bu_ref[pl.ds(r0, bsz), 0:ns] = nr
        bu_ref[pl.ds(r0, bsz), ns:2 * ns] = ni
        return nr, ni

    sr, si = lax.fori_loop(0, SSM_STEPS, step, (st_ref[:, 0:ns], st_ref[:, ns:2 * ns]), unroll=4)
    st_ref[:, 0:ns] = sr
    st_ref[:, ns:2 * ns] = si

    y = jnp.dot(bu_ref[...].astype(BF16), cm_ref[...], preferred_element_type=F32)
    y = y + d_ref[...] * u
    y = y * (0.5 * (1.0 + jnp.tanh(math.sqrt(2.0 / math.pi) * (y + 0.044715 * (y * y * y)))))
    z = jnp.dot(y.astype(BF16), gw_ref[...], preferred_element_type=F32) + gb_ref[...]
    out = y * (1.0 / (1.0 + jnp.exp(-z)))
    for h in range(halves):
        tb_ref[h] = out[:, h * LANES:(h + 1) * LANES]
    for b in range(bsz):
        for h in range(halves):
            c0 = b * SSM_WIDTH + h * LANES
            o_ref[:, c0:c0 + LANES] = tb_ref[h, pl.ds(b, SSM_STEPS, stride=bsz), :].astype(o_ref.dtype)


def _ssm_params(a_re, a_im, log_dt, b_re, b_im, c_re, c_im):
    lam = lax.complex(a_re.astype(F32), a_im.astype(F32))
    dt = jnp.exp(log_dt.astype(F32))[:, None]
    lam_bar = jnp.exp(lam * dt)
    b = lax.complex(b_re.astype(F32), b_im.astype(F32))
    b_bar = ((lam_bar - 1.0) / lam)[..., None] * b
    eye = jnp.eye(SSM_GROUPS, dtype=F32)

    def bdiag_in(m):
        return jnp.einsum('gph,gk->ghkp', m, eye).reshape(SSM_WIDTH, SSM_STATES)

    def bdiag_out(m):
        return jnp.einsum('ghp,gk->gpkh', m, eye).reshape(SSM_STATES, SSM_WIDTH)

    bm = jnp.concatenate([bdiag_in(jnp.real(b_bar)), bdiag_in(jnp.imag(b_bar))], axis=1)
    cm = jnp.concatenate([bdiag_out(c_re.astype(F32)), -bdiag_out(c_im.astype(F32))], axis=0)
    return (bm.astype(BF16), cm.astype(BF16),
            jnp.real(lam_bar).reshape(1, SSM_STATES), jnp.imag(lam_bar).reshape(1, SSM_STATES))


def _ssm_mixer(u_tm, params, d, glu_w, glu_b, bsz, seq):
    bm, cm, ar, ai = params
    rows = SSM_STEPS * bsz
    const = lambda c: (0, 0)
    return pl.pallas_call(
        _ssm_kernel,
        grid=(seq // SSM_STEPS,),
        in_specs=[
            pl.BlockSpec((SSM_STEPS, bsz * SSM_WIDTH), lambda c: (c, 0)),
            pl.BlockSpec((SSM_WIDTH, 2 * SSM_STATES), const),
            pl.BlockSpec((2 * SSM_STATES, SSM_WIDTH), const),
            pl.BlockSpec((1, SSM_STATES), const),
            pl.BlockSpec((1, SSM_STATES), const),
            pl.BlockSpec((1, SSM_WIDTH), const),
            pl.BlockSpec((SSM_WIDTH, SSM_WIDTH), const),
            pl.BlockSpec((1, SSM_WIDTH), const),
        ],
        out_specs=pl.BlockSpec((SSM_STEPS, bsz * SSM_WIDTH), lambda c: (c, 0)),
        out_shape=jax.ShapeDtypeStruct((seq, bsz * SSM_WIDTH), BF16),
        scratch_shapes=[pltpu.VMEM((rows, 2 * SSM_STATES), F32), pltpu.VMEM((bsz, 2 * SSM_STATES), F32),
                        pltpu.VMEM((SSM_WIDTH // LANES, rows, LANES), F32)],
        compiler_params=pltpu.CompilerParams(dimension_semantics=("arbitrary",),
                                             vmem_limit_bytes=VMEM_LIMIT),
        name="ssm_mixer",
    )(u_tm, bm, cm, ar, ai, d.reshape(1, SSM_WIDTH), glu_w.astype(BF16), glu_b.reshape(1, SSM_WIDTH))


def _bias_table_kernel(bias_ref, tab_ref):
    h = pl.program_id(0)
    kk = lax.broadcasted_iota(jnp.int32, (MOBA_BLOCK, MOBA_BLOCK), 0)
    qq = lax.broadcasted_iota(jnp.int32, (MOBA_BLOCK, MOBA_BLOCK), 1)
    for which in range(2):
        rel = qq - kk + which * MOBA_BLOCK
        val = jnp.full((MOBA_BLOCK, MOBA_BLOCK), bias_ref[0, h], F32)
        for b in range(1, REL_BUCKETS):
            val = jnp.where(rel >= _BUCKET_START[b], bias_ref[b, h], val)
        tab_ref[0, which] = jnp.where(rel >= 0, val * LOG2E, NEG_INF)


def _bias_tables(rel_bias):
    return pl.pallas_call(
        _bias_table_kernel,
        grid=(HEADS,),
        in_specs=[pl.BlockSpec(memory_space=pltpu.SMEM)],
        out_specs=pl.BlockSpec((1, 2, MOBA_BLOCK, MOBA_BLOCK), lambda h: (h, 0, 0, 0)),
        out_shape=jax.ShapeDtypeStruct((HEADS, 2, MOBA_BLOCK, MOBA_BLOCK), F32),
        name="bias_tables",
    )(rel_bias.astype(F32))


def _attn_kernel(bias_ref, q_ref, k_ref, vt_ref, tab_ref, o_ref,
                 qh_ref, va_ref, neg_ref, acc_ref, m_ref, s_ref, ahead_ref):
    seq = k_ref.shape[1]
    nb = seq // MOBA_BLOCK

    def prep():
        lane = lax.broadcasted_iota(jnp.int32, (1, HEAD_PAIR), 1)
        head_lanes = [lane < HEAD_DIM, lane >= HEAD_DIM]
        kf = k_ref[0].astype(F32).reshape(nb, MOBA_BLOCK, HEAD_PAIR)
        km = jnp.sum(kf, axis=1) * (1.0 / MOBA_BLOCK)
        q_all = q_ref[0].astype(F32)
        ones = jnp.ones((ONES_ROWS, seq), BF16)
        va_ref[0, 0:HEAD_DIM, :] = vt_ref[0, 0:HEAD_DIM, :]
        va_ref[0, HEAD_DIM:, :] = ones
        va_ref[1, 0:ONES_ROWS, :] = ones
        va_ref[1, ONES_ROWS:, :] = vt_ref[0, HEAD_DIM:, :]
        blk_id = lax.broadcasted_iota(jnp.int32, (nb, seq), 0)
        past = blk_id < lax.broadcasted_iota(jnp.int32, (nb, seq), 1) // MOBA_BLOCK
        for hh in range(2):
            qh_ref[hh] = jnp.where(head_lanes[hh], q_all, 0.0).astype(BF16)
            km_h = jnp.where(head_lanes[hh], km, 0.0).astype(BF16)
            g = lax.dot_general(km_h, qh_ref[hh], _NT, preferred_element_type=F32)
            g = jnp.where(past, g, NEG_INF)
            sel = None
            for _ in range(min(MOBA_TOPK, nb)):
                top = jnp.max(g, axis=0, keepdims=True)
                first = jnp.min(jnp.where(g == top, blk_id, nb), axis=0, keepdims=True)
                hit = blk_id == first
                sel = hit if sel is None else (sel | hit)
                g = jnp.where(hit, TAKEN, g)
            neg = jnp.where(sel & past, 0.0, NEG_INF)
            for qb in range(nb):
                neg_ref[hh, qb] = neg[:, qb * MOBA_BLOCK:(qb + 1) * MOBA_BLOCK]

    prep()

    def q_block(qi, carry):
        _attn_q_block(qi, pl.program_id(1), nb, bias_ref, k_ref, tab_ref, o_ref,
                      qh_ref, va_ref, neg_ref, acc_ref, m_ref, s_ref, ahead_ref)
        return carry

    lax.fori_loop(0, nb, q_block, 0)


def _attn_q_block(qi, pr, nb, bias_ref, k_ref, tab_ref, o_ref,
                  qh_ref, va_ref, neg_ref, acc_ref, m_ref, s_ref, ahead_ref):
    qh = [qh_ref[hh, pl.ds(pl.multiple_of(qi * MOBA_BLOCK, MOBA_BLOCK), MOBA_BLOCK), :] for hh in range(2)]

    def key_rows(j):
        return pl.ds(pl.multiple_of(j * MOBA_BLOCK, MOBA_BLOCK), MOBA_BLOCK)

    def score(hh, j):
        return lax.dot_general(k_ref[0, key_rows(j), :], qh[hh], _NT, preferred_element_type=F32)

    def attend(items, m_prev, parked=None, ahead=None):
        units = [(hh, i) for i in range(len(items)) for hh in range(2)]
        scores = {}
        if parked is None:
            for hh, i in units:
                scores[hh, i] = score(hh, items[i][0])
        ahead_units = [] if ahead is None else [(hh, i) for i in range(len(ahead[0])) for hh in range(2)]
        per_unit = -(-len(ahead_units) // len(units))
        mbs, pvs, staged = {}, {}, 0
        for n, (hh, i) in enumerate(units):
            for ahh, ai in ahead_units[n * per_unit:(n + 1) * per_unit]:
                ahead_ref[ahead[1], 2 * ai + ahh] = score(ahh, ahead[0][ai])
            add = items[i][1][hh]
            s = scores[hh, i] if parked is None else ahead_ref[parked, 2 * i + hh]
            if add.shape[0] == 1:
                smax = jnp.max(s, axis=0, keepdims=True)
                mbs[hh, i] = smax + add
                p = jnp.exp2(s - smax)
            else:
                s_ref[staged] = s
                s = s_ref[staged] + add
                staged += 1
                mbs[hh, i] = jnp.max(s, axis=0, keepdims=True)
                p = jnp.exp2(s - mbs[hh, i])
            pvs[hh, i] = jnp.dot(va_ref[hh, :, key_rows(items[i][0])], p.astype(BF16),
                                 preferred_element_type=F32)
        out = []
        for hh in range(2):
            m_new = None if m_prev is None else m_prev[hh]
            for i in range(len(items)):
                m_new = mbs[hh, i] if m_new is None else jnp.maximum(m_new, mbs[hh, i])
            acc = None if m_prev is None else acc_ref[hh] * jnp.exp2(m_prev[hh] - m_new)
            for i in range(len(items)):
                term = pvs[hh, i] * jnp.exp2(mbs[hh, i] - m_new)
                acc = term if acc is None else acc + term
            acc_ref[hh] = acc
            out.append(m_new)
        return tuple(out)

    def store_max(ms):
        m_ref[0] = ms[0]
        m_ref[1] = ms[1]

    def finalize(qb):
        oa = acc_ref[0]
        ob = acc_ref[1]
        out_t = jnp.concatenate([oa[0:HEAD_DIM] / oa[HEAD_DIM:HEAD_DIM + 1],
                                 ob[ONES_ROWS:] / ob[0:1]], axis=0)
        o_ref[0, key_rows(qb), :] = out_t.T.astype(o_ref.dtype)

    def own_item():
        return (qi, [tab_ref[hh, 0] for hh in range(2)])

    def prev_item():
        return (qi - 1, [tab_ref[hh, 1] + neg_ref[hh, qi, pl.ds(qi - 1, 1), :] for hh in range(2)])

    def far_item(j):
        return (j, [neg_ref[hh, qi, pl.ds(j, 1), :] + bias_ref[REL_BUCKETS - 1, 2 * pr + hh] * LOG2E
                    for hh in range(2)])

    n_far = jnp.maximum(qi - 1, 0)
    n_rem = n_far % FAR_GROUP
    n_groups = n_far // FAR_GROUP

    def group_blocks(g):
        return [n_rem + g * FAR_GROUP + b for b in range(FAR_GROUP)]

    @pl.when(qi == 0)
    def _first_block():
        store_max(attend([own_item()], None))

    for r in range(FAR_GROUP):
        for follows in (False, True):
            @pl.when((qi > 0) & (n_rem == r) & ((n_groups > 0) == follows))
            def _near_blocks(r=r, follows=follows):
                finalize(qi - 1)
                items = [own_item(), prev_item()] + [far_item(j) for j in range(r)]
                store_max(attend(items, None, ahead=(group_blocks(0), 0) if follows else None))

    max_groups = (nb - 2) // FAR_GROUP
    for g in range(max_groups):
        for follows in (False, True)[:1 if g + 1 == max_groups else 2]:
            @pl.when((g < n_groups) & ((g + 1 < n_groups) == follows))
            def _far_group(g=g, follows=follows):
                items = [far_item(j) for j in group_blocks(g)]
                ahead = (group_blocks(g + 1), (g + 1) % 2) if follows else None
                store_max(attend(items, (m_ref[0], m_ref[1]), parked=g % 2, ahead=ahead))

    @pl.when(qi == nb - 1)
    def _last_block():
        finalize(qi)


def _moba_attention(q, k, vt, tabs, rel_bias):
    bsz, seq, _ = q.shape
    assert seq % MOBA_BLOCK == 0
    nb = seq // MOBA_BLOCK
    return pl.pallas_call(
        _attn_kernel,
        grid=(bsz, HEADS // 2),
        in_specs=[
            pl.BlockSpec(memory_space=pltpu.SMEM),
            pl.BlockSpec((1, seq, HEAD_PAIR), lambda b, p: (b, 0, p)),
            pl.BlockSpec((1, seq, HEAD_PAIR), lambda b, p: (b, 0, p)),
            pl.BlockSpec((1, HEAD_PAIR, seq), lambda b, p: (b, p, 0)),
            pl.BlockSpec((2, 2, MOBA_BLOCK, MOBA_BLOCK), lambda b, p: (p, 0, 0, 0)),
        ],
        out_specs=pl.BlockSpec((1, seq, HEAD_PAIR), lambda b, p: (b, 0, p)),
        out_shape=jax.ShapeDtypeStruct((bsz, seq, ATT_WIDTH), BF16),
        scratch_shapes=[
            pltpu.VMEM((2, seq, HEAD_PAIR), BF16),
            pltpu.VMEM((2, HEAD_DIM + ONES_ROWS, seq), BF16),
            pltpu.VMEM((2, nb, nb, MOBA_BLOCK), F32),
            pltpu.VMEM((2, HEAD_DIM + ONES_ROWS, MOBA_BLOCK), F32),
            pltpu.VMEM((2, 1, MOBA_BLOCK), F32),
            pltpu.VMEM((4, MOBA_BLOCK, MOBA_BLOCK), F32),
            pltpu.VMEM((2, 2 * FAR_GROUP, MOBA_BLOCK, MOBA_BLOCK), F32),
        ],
        compiler_params=pltpu.CompilerParams(
            dimension_semantics=("parallel", "parallel"), vmem_limit_bytes=VMEM_LIMIT),
        name="moba_attention",
    )(rel_bias.astype(F32), q, k, vt, tabs)


def _routing_weights(logits_t):
    col = lambda i: logits_t[i:i + 1, :]
    gl = [col(g) for g in range(MOE_GROUPS)]
    gmax = _max_of(gl)
    sel, taken = [], None
    for g in range(MOE_GROUPS):
        hit = gl[g] == gmax
        if taken is not None:
            hit = hit & jnp.logical_not(taken)
        taken = hit if taken is None else (taken | hit)
        sel.append(hit)
    denom = None
    for g in range(MOE_GROUPS):
        e = jnp.exp(gl[g] - gmax)
        denom = e if denom is None else denom + e
    g_wt = 1.0 / denom
    el = []
    for e in range(MOE_PER_GROUP):
        v = col(MOE_GROUPS + (MOE_GROUPS - 1) * MOE_PER_GROUP + e)
        for g in range(MOE_GROUPS - 2, -1, -1):
            v = jnp.where(sel[g], col(MOE_GROUPS + g * MOE_PER_GROUP + e), v)
        el.append(v)
    rank = []
    for e in range(MOE_PER_GROUP):
        r = jnp.zeros_like(el[e], dtype=jnp.int32)
        for m in range(MOE_PER_GROUP):
            if m == e:
                continue
            beats = (el[m] >= el[e]) if m < e else (el[m] > el[e])
            r = r + beats.astype(jnp.int32)
        rank.append(r)
    v1 = _max_of(el)
    v2 = None
    for e in range(MOE_PER_GROUP):
        c = jnp.where(rank[e] == 1, el[e], 0.0)
        v2 = c if v2 is None else v2 + c
    e2 = jnp.exp(v2 - v1)
    w1 = 1.0 / (1.0 + e2)
    w2 = e2 / (1.0 + e2)
    tokens = logits_t.shape[1]
    row_id = lax.broadcasted_iota(jnp.int32, (MOE_EXPERTS, tokens), 0)
    comb_t = jnp.zeros((MOE_EXPERTS, tokens), F32)
    for g in range(MOE_GROUPS):
        for e in range(MOE_PER_GROUP):
            within = jnp.where(rank[e] == 0, w1, jnp.where(rank[e] == 1, w2, 0.0))
            c = jnp.where(sel[g], g_wt * within, 0.0)
            comb_t = jnp.where(row_id == g * MOE_PER_GROUP + e, c, comb_t)
    pad = jnp.zeros((ROUTER_LANES - MOE_EXPERTS, tokens), F32)
    return jnp.concatenate([comb_t, pad], axis=0).T


def _max_of(cols):
    out = cols[0]
    for c in cols[1:]:
        out = jnp.maximum(out, c)
    return out


def _moe_kernel(x_ref, ys_ref, yp_ref, ya_ref, wo_ref, g_ref, wr_ref, br_ref, wg_ref, wu_ref, wd_ref, fg_ref,
                o_ref, h_ref, comb_ref, *, final_norm):
    step = pl.program_id(1)

    @pl.when(step == 0)
    def _mix_and_route():
        s0, s1 = SSM_WIDTH, SSM_WIDTH + POOL_WIDTH
        mix = jnp.dot(ys_ref[...], wo_ref[0:s0, :], preferred_element_type=F32)
        mix = mix + jnp.dot(yp_ref[...], wo_ref[s0:s1, :], preferred_element_type=F32)
        mix = mix + jnp.dot(ya_ref[...], wo_ref[s1:, :], preferred_element_type=F32)
        x = x_ref[...] + mix
        h = _rms_norm(x, g_ref[...]).astype(BF16)
        h_ref[...] = h
        logits_t = lax.dot_general(wr_ref[...], h, _NT, preferred_element_type=F32)
        comb_ref[...] = _routing_weights(logits_t + br_ref[...])
        o_ref[...] = x

    h = h_ref[...]
    lane = lax.broadcasted_iota(jnp.int32, comb_ref.shape, 1)
    y = None
    for j in range(EXPERTS_PER_STEP):
        hg = jnp.dot(h, wg_ref[j], preferred_element_type=F32)
        hu = jnp.dot(h, wu_ref[j], preferred_element_type=F32)
        c = jnp.sum(jnp.where(lane == step * EXPERTS_PER_STEP + j, comb_ref[...], 0.0), axis=1, keepdims=True)
        act = (hg * (1.0 / (1.0 + jnp.exp(-hg)))) * hu * c
        yj = jnp.dot(act.astype(BF16), wd_ref[j], preferred_element_type=F32)
        y = yj if y is None else y + yj
    o_ref[...] += y

    if final_norm:
        @pl.when(step == MOE_EXPERTS // EXPERTS_PER_STEP - 1)
        def _final():
            o_ref[...] = _rms_norm(o_ref[...], fg_ref[...])


def _mix_and_moe(x2d, y_ssm_tm, y_pool, y_att, w_out, layer, norm_g, w_router, b_router, w_gate, w_up, w_down,
                 final_g, final_norm):
    rows = x2d.shape[0]
    tiles_per_seq = y_ssm_tm.shape[0] // MOE_ROWS
    const = lambda i, e: (0, 0)
    row_blk = lambda width: pl.BlockSpec((MOE_ROWS, width), lambda i, e: (i, 0))
    experts = lambda i, e: (layer, e, 0, 0)
    return pl.pallas_call(
        partial(_moe_kernel, final_norm=final_norm),
        grid=(rows // MOE_ROWS, MOE_EXPERTS // EXPERTS_PER_STEP),
        in_specs=[
            row_blk(D_MODEL),
            pl.BlockSpec((MOE_ROWS, SSM_WIDTH), lambda i, e: (i % tiles_per_seq, i // tiles_per_seq)),
            row_blk(POOL_WIDTH),
            row_blk(ATT_WIDTH),
            pl.BlockSpec((None, D_MODEL, D_MODEL), lambda i, e: (layer, 0, 0)),
            pl.BlockSpec((1, D_MODEL), const),
            pl.BlockSpec((ROUTER_ROWS, D_MODEL), const),
            pl.BlockSpec((ROUTER_ROWS, 1), const),
            pl.BlockSpec((None, EXPERTS_PER_STEP, D_MODEL, D_EXPERT), experts),
            pl.BlockSpec((None, EXPERTS_PER_STEP, D_MODEL, D_EXPERT), experts),
            pl.BlockSpec((None, EXPERTS_PER_STEP, D_EXPERT, D_MODEL), experts),
            pl.BlockSpec((1, D_MODEL), const),
        ],
        out_specs=pl.BlockSpec((MOE_ROWS, D_MODEL), lambda i, e: (i, 0)),
        out_shape=jax.ShapeDtypeStruct((rows, D_MODEL), F32),
        scratch_shapes=[pltpu.VMEM((MOE_ROWS, D_MODEL), BF16), pltpu.VMEM((MOE_ROWS, ROUTER_LANES), F32)],
        compiler_params=pltpu.CompilerParams(dimension_semantics=("parallel", "arbitrary"),
                                             vmem_limit_bytes=VMEM_LIMIT),
        name="mix_and_moe",
    )(x2d, y_ssm_tm, y_pool, y_att, w_out, norm_g.reshape(1, D_MODEL), w_router, b_router, w_gate, w_up, w_down,
      final_g.reshape(1, D_MODEL))


def _router_params(group_w, group_b, router_w, router_b):
    w = jnp.concatenate([group_w.astype(F32)] + [router_w[g].astype(F32) for g in range(MOE_GROUPS)], axis=1)
    b = jnp.concatenate([group_b.astype(F32), router_b.astype(F32).reshape(-1)])
    pad = ROUTER_ROWS - w.shape[1]
    return jnp.pad(w.T, ((0, pad), (0, 0))).astype(BF16), jnp.pad(b, (0, pad)).reshape(ROUTER_ROWS, 1)


def kernel(x, rel_bias, norm1_g, w_in, ssm_a_re, ssm_a_im, ssm_log_dt, ssm_b_re, ssm_b_im, ssm_c_re, ssm_c_im,
           ssm_d, ssm_glu_w, ssm_glu_b, pool_w, pool_b, pool_scale, w_out, norm2_g, moe_group_w, moe_group_b,
           moe_router_w, moe_router_b, moe_w_gate, moe_w_up, moe_w_down, final_norm_g):
    bsz, seq, dm = x.shape
    depth = w_in.shape[0]
    assert dm == D_MODEL and depth >= 1
    assert bsz % SUBLANES == 0, "the scan keeps one time step of all batches in whole sublane tiles"
    assert all(seq % t == 0 for t in (PROJ_ROWS, MOE_ROWS, SSM_STEPS, MOBA_BLOCK))
    n_main = SSM_WIDTH + POOL_WIDTH + 2 * ATT_WIDTH
    x2d = x.astype(F32).reshape(bsz * seq, dm)
    tabs = _bias_tables(rel_bias)
    w_in_bf = w_in.astype(BF16)
    w_vt_bf = jnp.swapaxes(w_in[:, :, n_main:], 1, 2).astype(BF16)
    w_out_bf = w_out.astype(BF16)
    w_gate_bf, w_up_bf, w_down_bf = moe_w_gate.astype(BF16), moe_w_up.astype(BF16), moe_w_down.astype(BF16)
    ssm_params = jax.vmap(_ssm_params)(ssm_a_re, ssm_a_im, ssm_log_dt, ssm_b_re, ssm_b_im, ssm_c_re, ssm_c_im)
    for l in range(depth):
        u, y_pool, q, k, vt = _in_projection(x2d, norm1_g[l], w_in_bf, w_vt_bf, l,
                                             _pool_params(pool_w[l], pool_b[l], pool_scale[l]), bsz, seq)

        y_ssm = _ssm_mixer(u, [p[l] for p in ssm_params], ssm_d[l], ssm_glu_w[l], ssm_glu_b[l], bsz, seq)

        y_att = _moba_attention(q.reshape(bsz, seq, ATT_WIDTH), k.reshape(bsz, seq, ATT_WIDTH), vt, tabs, rel_bias)

        w_router, b_router = _router_params(moe_group_w[l], moe_group_b[l], moe_router_w[l], moe_router_b[l])
        x2d = _mix_and_moe(x2d, y_ssm, y_pool, y_att.reshape(bsz * seq, ATT_WIDTH), w_out_bf, l,
                           norm2_g[l], w_router, b_router, w_gate_bf, w_up_bf, w_down_bf,
                           final_norm_g, final_norm=(l == depth - 1))
    return x2d.reshape(bsz, seq, dm).astype(x.dtype)
```

```python
import math
from functools import partial

import jax
import jax.numpy as jnp
from jax import lax
from jax.experimental import pallas as pl
from jax.experimental.pallas import tpu as pltpu

F32 = jnp.float32
BF16 = jnp.bfloat16

LANES = 128
SUBLANES = 8
D_MODEL = 1024
SSM_WIDTH = 256
POOL_WIDTH = 256
ATT_WIDTH = 512
SSM_GROUP = 16
SSM_GROUPS = SSM_WIDTH // SSM_GROUP
SSM_STATE = 64
SSM_STATES = SSM_GROUPS * SSM_STATE
POOL_WINDOWS = (2, 4, 8, 16)
POOL_GROUP = POOL_WIDTH // len(POOL_WINDOWS)
POOL_HALO = 16
HEAD_DIM = 64
HEADS = ATT_WIDTH // HEAD_DIM
HEAD_PAIR = 2 * HEAD_DIM
ONES_ROWS = 16
MOBA_BLOCK = 256
MOBA_TOPK = 3
FAR_GROUP = 4
REL_BUCKETS = 32
REL_MAX_EXACT = REL_BUCKETS // 2
REL_MAX_DIST = 128
MOE_GROUPS = 4
MOE_PER_GROUP = 4
MOE_EXPERTS = MOE_GROUPS * MOE_PER_GROUP
D_EXPERT = D_MODEL // 4
ROUTER_ROWS = 32
ROUTER_LANES = 128
RMS_EPS = 1e-6
NEG_INF = -1e30
TAKEN = -3e38
LOG2E = math.log2(math.e)

PROJ_ROWS = 1024
MOE_ROWS = 1024
EXPERTS_PER_STEP = 4
SSM_STEPS = 128
VMEM_LIMIT = 48 * 1024 * 1024

_NT = (((1,), (1,)), ((), ()))

_BUCKET_START = list(range(REL_MAX_EXACT)) + [
    math.ceil(REL_MAX_EXACT * (REL_MAX_DIST / REL_MAX_EXACT) ** (k / (REL_BUCKETS - REL_MAX_EXACT)))
    for k in range(REL_BUCKETS - REL_MAX_EXACT)
]


def _rms_norm(x, g):
    return x * lax.rsqrt(jnp.mean(x * x, axis=-1, keepdims=True) + RMS_EPS) * g


def _inproj_kernel(x_ref, g_ref, w_ref, wvt_ref, pw_ref, pb_ref, ps_ref,
                   u_ref, yp_ref, q_ref, k_ref, vt_ref, halo_ref, *, tiles_per_seq):
    h = _rms_norm(x_ref[...], g_ref[...]).astype(BF16)
    pr = jnp.dot(h, w_ref[...], preferred_element_type=F32)
    s0, s1, s2 = SSM_WIDTH, SSM_WIDTH + POOL_WIDTH, SSM_WIDTH + POOL_WIDTH + ATT_WIDTH
    u_ref[...] = pr[:, :s0]
    t_tile = pl.program_id(0) % tiles_per_seq
    p = pr[:, s0:s1]
    halo = jnp.where(t_tile == 0, 0.0, halo_ref[...])
    halo_ref[...] = p[PROJ_ROWS - POOL_HALO:, :]
    yp_ref[...] = _pool_mix(p, halo, t_tile * PROJ_ROWS, pw_ref[...], pb_ref[...], ps_ref[...]).astype(BF16)
    q_ref[...] = (pr[:, s1:s2] * (HEAD_DIM ** -0.5 * LOG2E)).astype(BF16)
    k_ref[...] = pr[:, s2:].astype(BF16)
    vt_ref[0] = lax.dot_general(wvt_ref[...], h, _NT, preferred_element_type=F32).astype(BF16)


def _pool_mix(p, halo, t0, w_bd, b, scale):
    ext = jnp.concatenate([halo, p], axis=0)
    s2 = ext + pltpu.roll(ext, 1, 0)
    s4 = s2 + pltpu.roll(s2, 2, 0)
    s8 = s4 + pltpu.roll(s4, 4, 0)
    s16 = s8 + pltpu.roll(s8, 8, 0)
    sums = (s2, s4, s8, s16)
    lane = lax.broadcasted_iota(jnp.int32, p.shape, 1)
    t1 = (lax.broadcasted_iota(jnp.int32, p.shape, 0) + t0 + 1).astype(F32)
    mean = None
    for gi, win in enumerate(POOL_WINDOWS):
        m = sums[gi][POOL_HALO:] / jnp.minimum(t1, float(win))
        mean = m if mean is None else jnp.where(lane >= gi * POOL_GROUP, m, mean)
    y = jnp.dot((mean - p).astype(BF16), w_bd, preferred_element_type=F32)
    return (y + b) * scale


def _pool_params(w, b, scale):
    eye = jnp.eye(len(POOL_WINDOWS), dtype=F32)
    w_bd = jnp.einsum('gcd,gk->gckd', w.astype(F32), eye).reshape(POOL_WIDTH, POOL_WIDTH).astype(BF16)
    return w_bd, b.astype(F32).reshape(1, POOL_WIDTH), scale.astype(F32).reshape(1, POOL_WIDTH)


def _in_projection(x2d, g, w_in, w_vt, layer, pool_params, bsz, seq):
    rows = x2d.shape[0]
    tiles_per_seq = seq // PROJ_ROWS
    n_main = SSM_WIDTH + POOL_WIDTH + 2 * ATT_WIDTH
    const = lambda i: (0, 0)
    this_layer = lambda i: (layer, 0, 0)
    row_blk = lambda width: pl.BlockSpec((PROJ_ROWS, width), lambda i: (i, 0))
    return pl.pallas_call(
        partial(_inproj_kernel, tiles_per_seq=tiles_per_seq),
        grid=(rows // PROJ_ROWS,),
        in_specs=[row_blk(D_MODEL), pl.BlockSpec((1, D_MODEL), const),
                  pl.BlockSpec((None, D_MODEL, n_main), this_layer),
                  pl.BlockSpec((None, ATT_WIDTH, D_MODEL), this_layer),
                  pl.BlockSpec((POOL_WIDTH, POOL_WIDTH), const), pl.BlockSpec((1, POOL_WIDTH), const),
                  pl.BlockSpec((1, POOL_WIDTH), const)],
        out_specs=[pl.BlockSpec((PROJ_ROWS, SSM_WIDTH), lambda i: (i % tiles_per_seq, i // tiles_per_seq)),
                   row_blk(POOL_WIDTH), row_blk(ATT_WIDTH), row_blk(ATT_WIDTH),
                   pl.BlockSpec((1, ATT_WIDTH, PROJ_ROWS),
                                lambda i: (i // tiles_per_seq, 0, i % tiles_per_seq))],
        out_shape=[jax.ShapeDtypeStruct((seq, bsz * SSM_WIDTH), F32),
                   jax.ShapeDtypeStruct((rows, POOL_WIDTH), BF16),
                   jax.ShapeDtypeStruct((rows, ATT_WIDTH), BF16),
                   jax.ShapeDtypeStruct((rows, ATT_WIDTH), BF16),
                   jax.ShapeDtypeStruct((bsz, ATT_WIDTH, seq), BF16)],
        scratch_shapes=[pltpu.VMEM((POOL_HALO, POOL_WIDTH), F32)],
        compiler_params=pltpu.CompilerParams(dimension_semantics=("arbitrary",),
                                             vmem_limit_bytes=VMEM_LIMIT),
        name="in_projection",
    )(x2d, g.reshape(1, D_MODEL), w_in, w_vt, *pool_params)


def _ssm_kernel(u_ref, bm_ref, cm_ref, ar_ref, ai_ref, d_ref, gw_ref, gb_ref, o_ref, bu_ref, st_ref, tb_ref):
    bsz = st_ref.shape[0]
    ns = SSM_STATES

    @pl.when(pl.program_id(0) == 0)
    def _init():
        st_ref[...] = jnp.zeros_like(st_ref)

    halves = SSM_WIDTH // LANES
    for b in range(bsz):
        for h in range(halves):
            c0 = b * SSM_WIDTH + h * LANES
            tb_ref[h, pl.ds(b, SSM_STEPS, stride=bsz), :] = u_ref[:, c0:c0 + LANES]
    ar = jnp.broadcast_to(ar_ref[...], (bsz, ns))
    ai = jnp.broadcast_to(ai_ref[...], (bsz, ns))
    half_rows = SSM_STEPS // 2 * bsz

    def rows_of(part):
        return slice(part * half_rows, (part + 1) * half_rows)

    def u_of(part):
        return jnp.concatenate([tb_ref[h, rows_of(part), :] for h in range(halves)], axis=1)

    def project_in(part):
        bu_ref[rows_of(part), :] = jnp.dot(u_of(part).astype(BF16), bm_ref[...], preferred_element_type=F32)

    def scan(part, carry):
        sr, si = carry
        for t in range(SSM_STEPS // 2):
            rows = slice(part * half_rows + t * bsz, part * half_rows + (t + 1) * bsz)
            sr, si = (ar * sr - ai * si + bu_ref[rows, 0:ns], ar * si + ai * sr + bu_ref[rows, ns:2 * ns])
            bu_ref[rows, 0:ns] = sr
            bu_ref[rows, ns:2 * ns] = si
        return sr, si

    def project_out(part):
        slab = half_rows // 2
        y = jnp.concatenate(
            [jnp.dot(bu_ref[part * half_rows + i * slab:part * half_rows + (i + 1) * slab, :].astype(BF16),
                     cm_ref[...], preferred_element_type=F32) for i in range(2)], axis=0)
        y = y + d_ref[...] * u_of(part)
        y = y * (0.5 * (1.0 + jnp.tanh(math.sqrt(2.0 / math.pi) * (y + 0.044715 * (y * y * y)))))
        z = jnp.dot(y.astype(BF16), gw_ref[...], preferred_element_type=F32) + gb_ref[...]
        return y * (1.0 / (1.0 + jnp.exp(-z)))

    project_in(0)
    project_in(1)
    state = scan(0, (st_ref[:, 0:ns], st_ref[:, ns:2 * ns]))
    out0 = project_out(0)
    sr, si = scan(1, state)
    st_ref[:, 0:ns] = sr
    st_ref[:, ns:2 * ns] = si
    out = jnp.concatenate([out0, project_out(1)], axis=0)
    for h in range(halves):
        tb_ref[h] = out[:, h * LANES:(h + 1) * LANES]
    for b in range(bsz):
        for h in range(halves):
            c0 = b * SSM_WIDTH + h * LANES
            o_ref[:, c0:c0 + LANES] = tb_ref[h, pl.ds(b, SSM_STEPS, stride=bsz), :].astype(o_ref.dtype)


def _ssm_params(a_re, a_im, log_dt, b_re, b_im, c_re, c_im):
    lam = lax.complex(a_re.astype(F32), a_im.astype(F32))
    dt = jnp.exp(log_dt.astype(F32))[:, None]
    lam_bar = jnp.exp(lam * dt)
    b = lax.complex(b_re.astype(F32), b_im.astype(F32))
    b_bar = ((lam_bar - 1.0) / lam)[..., None] * b
    eye = jnp.eye(SSM_GROUPS, dtype=F32)

    def bdiag_in(m):
        return jnp.einsum('gph,gk->ghkp', m, eye).reshape(SSM_WIDTH, SSM_STATES)

    def bdiag_out(m):
        return jnp.einsum('ghp,gk->gpkh', m, eye).reshape(SSM_STATES, SSM_WIDTH)

    bm = jnp.concatenate([bdiag_in(jnp.real(b_bar)), bdiag_in(jnp.imag(b_bar))], axis=1)
    cm = jnp.concatenate([bdiag_out(c_re.astype(F32)), -bdiag_out(c_im.astype(F32))], axis=0)
    return (bm.astype(BF16), cm.astype(BF16),
            jnp.real(lam_bar).reshape(1, SSM_STATES), jnp.imag(lam_bar).reshape(1, SSM_STATES))


def _ssm_mixer(u_tm, params, d, glu_w, glu_b, bsz, seq):
    bm, cm, ar, ai = params
    rows = SSM_STEPS * bsz
    const = lambda c: (0, 0)
    return pl.pallas_call(
        _ssm_kernel,
        grid=(seq // SSM_STEPS,),
        in_specs=[
            pl.BlockSpec((SSM_STEPS, bsz * SSM_WIDTH), lambda c: (c, 0)),
            pl.BlockSpec((SSM_WIDTH, 2 * SSM_STATES), const),
            pl.BlockSpec((2 * SSM_STATES, SSM_WIDTH), const),
            pl.BlockSpec((1, SSM_STATES), const),
            pl.BlockSpec((1, SSM_STATES), const),
            pl.BlockSpec((1, SSM_WIDTH), const),
            pl.BlockSpec((SSM_WIDTH, SSM_WIDTH), const),
            pl.BlockSpec((1, SSM_WIDTH), const),
        ],
        out_specs=pl.BlockSpec((SSM_STEPS, bsz * SSM_WIDTH), lambda c: (c, 0)),
        out_shape=jax.ShapeDtypeStruct((seq, bsz * SSM_WIDTH), BF16),
        scratch_shapes=[pltpu.VMEM((rows, 2 * SSM_STATES), F32), pltpu.VMEM((bsz, 2 * SSM_STATES), F32),
                        pltpu.VMEM((SSM_WIDTH // LANES, rows, LANES), F32)],
        compiler_params=pltpu.CompilerParams(dimension_semantics=("arbitrary",),
                                             vmem_limit_bytes=VMEM_LIMIT),
        name="ssm_mixer",
    )(u_tm, bm, cm, ar, ai, d.reshape(1, SSM_WIDTH), glu_w.astype(BF16), glu_b.reshape(1, SSM_WIDTH))


def _bias_table_kernel(bias_ref, tab_ref):
    h = pl.program_id(0)
    kk = lax.broadcasted_iota(jnp.int32, (MOBA_BLOCK, MOBA_BLOCK), 0)
    qq = lax.broadcasted_iota(jnp.int32, (MOBA_BLOCK, MOBA_BLOCK), 1)
    for which in range(2):
        rel = qq - kk + which * MOBA_BLOCK
        val = jnp.full((MOBA_BLOCK, MOBA_BLOCK), bias_ref[0, h], F32)
        for b in range(1, REL_BUCKETS):
            val = jnp.where(rel >= _BUCKET_START[b], bias_ref[b, h], val)
        tab_ref[0, which] = jnp.where(rel >= 0, val * LOG2E, NEG_INF)


def _bias_tables(rel_bias):
    return pl.pallas_call(
        _bias_table_kernel,
        grid=(HEADS,),
        in_specs=[pl.BlockSpec(memory_space=pltpu.SMEM)],
        out_specs=pl.BlockSpec((1, 2, MOBA_BLOCK, MOBA_BLOCK), lambda h: (h, 0, 0, 0)),
        out_shape=jax.ShapeDtypeStruct((HEADS, 2, MOBA_BLOCK, MOBA_BLOCK), F32),
        name="bias_tables",
    )(rel_bias.astype(F32))


def _attn_kernel(bias_ref, q_ref, k_ref, vt_ref, tab_ref, o_ref,
                 qh_ref, va_ref, neg_ref, acc_ref, m_ref, s_ref, ahead_ref):
    seq = k_ref.shape[1]
    nb = seq // MOBA_BLOCK

    def prep():
        lane = lax.broadcasted_iota(jnp.int32, (1, HEAD_PAIR), 1)
        head_lanes = [lane < HEAD_DIM, lane >= HEAD_DIM]
        kf = k_ref[0].astype(F32).reshape(nb, MOBA_BLOCK, HEAD_PAIR)
        km = jnp.sum(kf, axis=1) * (1.0 / MOBA_BLOCK)
        q_all = q_ref[0].astype(F32)
        ones = jnp.ones((ONES_ROWS, seq), BF16)
        va_ref[0, 0:HEAD_DIM, :] = vt_ref[0, 0:HEAD_DIM, :]
        va_ref[0, HEAD_DIM:, :] = ones
        va_ref[1, 0:ONES_ROWS, :] = ones
        va_ref[1, ONES_ROWS:, :] = vt_ref[0, HEAD_DIM:, :]
        blk_id = lax.broadcasted_iota(jnp.int32, (nb, seq), 0)
        past = blk_id < lax.broadcasted_iota(jnp.int32, (nb, seq), 1) // MOBA_BLOCK
        for hh in range(2):
            qh_ref[hh] = jnp.where(head_lanes[hh], q_all, 0.0).astype(BF16)
            km_h = jnp.where(head_lanes[hh], km, 0.0).astype(BF16)
            g = lax.dot_general(km_h, qh_ref[hh], _NT, preferred_element_type=F32)
            g = jnp.where(past, g, NEG_INF)
            sel = None
            for _ in range(min(MOBA_TOPK, nb)):
                top = jnp.max(g, axis=0, keepdims=True)
                first = jnp.min(jnp.where(g == top, blk_id, nb), axis=0, keepdims=True)
                hit = blk_id == first
                sel = hit if sel is None else (sel | hit)
                g = jnp.where(hit, TAKEN, g)
            neg = jnp.where(sel & past, 0.0, NEG_INF)
            for qb in range(nb):
                neg_ref[hh, qb] = neg[:, qb * MOBA_BLOCK:(qb + 1) * MOBA_BLOCK]

    prep()

    def q_block(qi, carry):
        _attn_q_block(qi, pl.program_id(1), nb, bias_ref, k_ref, tab_ref, o_ref,
                      qh_ref, va_ref, neg_ref, acc_ref, m_ref, s_ref, ahead_ref)
        return carry

    lax.fori_loop(0, nb, q_block, 0)


def _attn_q_block(qi, pr, nb, bias_ref, k_ref, tab_ref, o_ref,
                  qh_ref, va_ref, neg_ref, acc_ref, m_ref, s_ref, ahead_ref):
    qh = [qh_ref[hh, pl.ds(pl.multiple_of(qi * MOBA_BLOCK, MOBA_BLOCK), MOBA_BLOCK), :] for hh in range(2)]

    def key_rows(j):
        return pl.ds(pl.multiple_of(j * MOBA_BLOCK, MOBA_BLOCK), MOBA_BLOCK)

    def score(hh, j):
        return lax.dot_general(k_ref[0, key_rows(j), :], qh[hh], _NT, preferred_element_type=F32)

    def attend(items, m_prev, parked=None, ahead=None):
        units = [(hh, i) for i in range(len(items)) for hh in range(2)]
        scores = {}
        if parked is None:
            for hh, i in units:
                scores[hh, i] = score(hh, items[i][0])
        ahead_units = [] if ahead is None else [(hh, i) for i in range(len(ahead[0])) for hh in range(2)]
        per_unit = -(-len(ahead_units) // len(units))
        mbs, pvs, staged = {}, {}, 0
        for n, (hh, i) in enumerate(units):
            for ahh, ai in ahead_units[n * per_unit:(n + 1) * per_unit]:
                ahead_ref[ahead[1], 2 * ai + ahh] = score(ahh, ahead[0][ai])
            add = items[i][1][hh]
            s = scores[hh, i] if parked is None else ahead_ref[parked, 2 * i + hh]
            if add.shape[0] == 1:
                smax = jnp.max(s, axis=0, keepdims=True)
                mbs[hh, i] = smax + add
                p = jnp.exp2(s - smax)
            else:
                s_ref[staged] = s
                s = s_ref[staged] + add
                staged += 1
                mbs[hh, i] = jnp.max(s, axis=0, keepdims=True)
                p = jnp.exp2(s - mbs[hh, i])
            pvs[hh, i] = jnp.dot(va_ref[hh, :, key_rows(items[i][0])], p.astype(BF16),
                                 preferred_element_type=F32)
        out = []
        for hh in range(2):
            m_new = None if m_prev is None else m_prev[hh]
            for i in range(len(items)):
                m_new = mbs[hh, i] if m_new is None else jnp.maximum(m_new, mbs[hh, i])
            acc = None if m_prev is None else acc_ref[hh] * jnp.exp2(m_prev[hh] - m_new)
            for i in range(len(items)):
                term = pvs[hh, i] * jnp.exp2(mbs[hh, i] - m_new)
                acc = term if acc is None else acc + term
            acc_ref[hh] = acc
            out.append(m_new)
        return tuple(out)

    def store_max(ms):
        m_ref[0] = ms[0]
        m_ref[1] = ms[1]

    def finalize(qb):
        oa = acc_ref[0]
        ob = acc_ref[1]
        out_t = jnp.concatenate([oa[0:HEAD_DIM] / oa[HEAD_DIM:HEAD_DIM + 1],
                                 ob[ONES_ROWS:] / ob[0:1]], axis=0)
        o_ref[0, key_rows(qb), :] = out_t.T.astype(o_ref.dtype)

    def own_item():
        return (qi, [tab_ref[hh, 0] for hh in range(2)])

    def prev_item():
        return (qi - 1, [tab_ref[hh, 1] + neg_ref[hh, qi, pl.ds(qi - 1, 1), :] for hh in range(2)])

    def far_item(j):
        return (j, [neg_ref[hh, qi, pl.ds(j, 1), :] + bias_ref[REL_BUCKETS - 1, 2 * pr + hh] * LOG2E
                    for hh in range(2)])

    n_far = jnp.maximum(qi - 1, 0)
    n_rem = n_far % FAR_GROUP
    n_groups = n_far // FAR_GROUP

    def group_blocks(g):
        return [n_rem + g * FAR_GROUP + b for b in range(FAR_GROUP)]

    @pl.when(qi == 0)
    def _first_block():
        store_max(attend([own_item()], None))

    for r in range(FAR_GROUP):
        for follows in (False, True):
            @pl.when((qi > 0) & (n_rem == r) & ((n_groups > 0) == follows))
            def _near_blocks(r=r, follows=follows):
                finalize(qi - 1)
                items = [own_item(), prev_item()] + [far_item(j) for j in range(r)]
                store_max(attend(items, None, ahead=(group_blocks(0), 0) if follows else None))

    max_groups = (nb - 2) // FAR_GROUP
    for g in range(max_groups):
        for follows in (False, True)[:1 if g + 1 == max_groups else 2]:
            @pl.when((g < n_groups) & ((g + 1 < n_groups) == follows))
            def _far_group(g=g, follows=follows):
                items = [far_item(j) for j in group_blocks(g)]
                ahead = (group_blocks(g + 1), (g + 1) % 2) if follows else None
                store_max(attend(items, (m_ref[0], m_ref[1]), parked=g % 2, ahead=ahead))

    @pl.when(qi == nb - 1)
    def _last_block():
        finalize(qi)


def _moba_attention(q, k, vt, tabs, rel_bias):
    bsz, seq, _ = q.shape
    assert seq % MOBA_BLOCK == 0
    nb = seq // MOBA_BLOCK
    return pl.pallas_call(
        _attn_kernel,
        grid=(bsz, HEADS // 2),
        in_specs=[
            pl.BlockSpec(memory_space=pltpu.SMEM),
            pl.BlockSpec((1, seq, HEAD_PAIR), lambda b, p: (b, 0, p)),
            pl.BlockSpec((1, seq, HEAD_PAIR), lambda b, p: (b, 0, p)),
            pl.BlockSpec((1, HEAD_PAIR, seq), lambda b, p: (b, p, 0)),
            pl.BlockSpec((2, 2, MOBA_BLOCK, MOBA_BLOCK), lambda b, p: (p, 0, 0, 0)),
        ],
        out_specs=pl.BlockSpec((1, seq, HEAD_PAIR), lambda b, p: (b, 0, p)),
        out_shape=jax.ShapeDtypeStruct((bsz, seq, ATT_WIDTH), BF16),
        scratch_shapes=[
            pltpu.VMEM((2, seq, HEAD_PAIR), BF16),
            pltpu.VMEM((2, HEAD_DIM + ONES_ROWS, seq), BF16),
            pltpu.VMEM((2, nb, nb, MOBA_BLOCK), F32),
            pltpu.VMEM((2, HEAD_DIM + ONES_ROWS, MOBA_BLOCK), F32),
            pltpu.VMEM((2, 1, MOBA_BLOCK), F32),
            pltpu.VMEM((4, MOBA_BLOCK, MOBA_BLOCK), F32),
            pltpu.VMEM((2, 2 * FAR_GROUP, MOBA_BLOCK, MOBA_BLOCK), F32),
        ],
        compiler_params=pltpu.CompilerParams(
            dimension_semantics=("parallel", "parallel"), vmem_limit_bytes=VMEM_LIMIT),
        name="moba_attention",
    )(rel_bias.astype(F32), q, k, vt, tabs)


def _routing_weights(logits_t):
    col = lambda i: logits_t[i:i + 1, :]
    gl = [col(g) for g in range(MOE_GROUPS)]
    gmax = _max_of(gl)
    sel, taken = [], None
    for g in range(MOE_GROUPS):
        hit = gl[g] == gmax
        if taken is not None:
            hit = hit & jnp.logical_not(taken)
        taken = hit if taken is None else (taken | hit)
        sel.append(hit)
    denom = None
    for g in range(MOE_GROUPS):
        e = jnp.exp(gl[g] - gmax)
        denom = e if denom is None else denom + e
    g_wt = 1.0 / denom
    el = []
    for e in range(MOE_PER_GROUP):
        v = col(MOE_GROUPS + (MOE_GROUPS - 1) * MOE_PER_GROUP + e)
        for g in range(MOE_GROUPS - 2, -1, -1):
            v = jnp.where(sel[g], col(MOE_GROUPS + g * MOE_PER_GROUP + e), v)
        el.append(v)
    rank = []
    for e in range(MOE_PER_GROUP):
        r = jnp.zeros_like(el[e], dtype=jnp.int32)
        for m in range(MOE_PER_GROUP):
            if m == e:
                continue
            beats = (el[m] >= el[e]) if m < e else (el[m] > el[e])
            r = r + beats.astype(jnp.int32)
        rank.append(r)
    v1 = _max_of(el)
    v2 = None
    for e in range(MOE_PER_GROUP):
        c = jnp.where(rank[e] == 1, el[e], 0.0)
        v2 = c if v2 is None else v2 + c
    e2 = jnp.exp(v2 - v1)
    w1 = 1.0 / (1.0 + e2)
    w2 = e2 / (1.0 + e2)
    tokens = logits_t.shape[1]
    row_id = lax.broadcasted_iota(jnp.int32, (MOE_EXPERTS, tokens), 0)
    comb_t = jnp.zeros((MOE_EXPERTS, tokens), F32)
    for g in range(MOE_GROUPS):
        for e in range(MOE_PER_GROUP):
            within = jnp.where(rank[e] == 0, w1, jnp.where(rank[e] == 1, w2, 0.0))
            c = jnp.where(sel[g], g_wt * within, 0.0)
            comb_t = jnp.where(row_id == g * MOE_PER_GROUP + e, c, comb_t)
    pad = jnp.zeros((ROUTER_LANES - MOE_EXPERTS, tokens), F32)
    return jnp.concatenate([comb_t, pad], axis=0).T


def _max_of(cols):
    out = cols[0]
    for c in cols[1:]:
        out = jnp.maximum(out, c)
    return out


def _moe_kernel(x_ref, ys_ref, yp_ref, ya_ref, wo_ref, g_ref, wr_ref, br_ref, wg_ref, wu_ref, wd_ref, fg_ref,
                o_ref, h_ref, comb_ref, *, final_norm):
    step = pl.program_id(1)

    @pl.when(step == 0)
    def _mix_and_route():
        s0, s1 = SSM_WIDTH, SSM_WIDTH + POOL_WIDTH
        mix = jnp.dot(ys_ref[...], wo_ref[0:s0, :], preferred_element_type=F32)
        mix = mix + jnp.dot(yp_ref[...], wo_ref[s0:s1, :], preferred_element_type=F32)
        mix = mix + jnp.dot(ya_ref[...], wo_ref[s1:, :], preferred_element_type=F32)
        x = x_ref[...] + mix
        h = _rms_norm(x, g_ref[...]).astype(BF16)
        h_ref[...] = h
        logits_t = lax.dot_general(wr_ref[...], h, _NT, preferred_element_type=F32)
        comb_ref[...] = _routing_weights(logits_t + br_ref[...])
        o_ref[...] = x

    h = h_ref[...]
    lane = lax.broadcasted_iota(jnp.int32, comb_ref.shape, 1)
    y = None
    for j in range(EXPERTS_PER_STEP):
        hg = jnp.dot(h, wg_ref[j], preferred_element_type=F32)
        hu = jnp.dot(h, wu_ref[j], preferred_element_type=F32)
        c = jnp.sum(jnp.where(lane == step * EXPERTS_PER_STEP + j, comb_ref[...], 0.0), axis=1, keepdims=True)
        act = (hg * (1.0 / (1.0 + jnp.exp(-hg)))) * hu * c
        yj = jnp.dot(act.astype(BF16), wd_ref[j], preferred_element_type=F32)
        y = yj if y is None else y + yj
    o_ref[...] += y

    if final_norm:
        @pl.when(step == MOE_EXPERTS // EXPERTS_PER_STEP - 1)
        def _final():
            o_ref[...] = _rms_norm(o_ref[...], fg_ref[...])


def _mix_and_moe(x2d, y_ssm_tm, y_pool, y_att, w_out, layer, norm_g, w_router, b_router, w_gate, w_up, w_down,
                 final_g, final_norm):
    rows = x2d.shape[0]
    tiles_per_seq = y_ssm_tm.shape[0] // MOE_ROWS
    const = lambda i, e: (0, 0)
    row_blk = lambda width: pl.BlockSpec((MOE_ROWS, width), lambda i, e: (i, 0))
    experts = lambda i, e: (layer, e, 0, 0)
    return pl.pallas_call(
        partial(_moe_kernel, final_norm=final_norm),
        grid=(rows // MOE_ROWS, MOE_EXPERTS // EXPERTS_PER_STEP),
        in_specs=[
            row_blk(D_MODEL),
            pl.BlockSpec((MOE_ROWS, SSM_WIDTH), lambda i, e: (i % tiles_per_seq, i // tiles_per_seq)),
            row_blk(POOL_WIDTH),
            row_blk(ATT_WIDTH),
            pl.BlockSpec((None, D_MODEL, D_MODEL), lambda i, e: (layer, 0, 0)),
            pl.BlockSpec((1, D_MODEL), const),
            pl.BlockSpec((ROUTER_ROWS, D_MODEL), const),
            pl.BlockSpec((ROUTER_ROWS, 1), const),
            pl.BlockSpec((None, EXPERTS_PER_STEP, D_MODEL, D_EXPERT), experts),
            pl.BlockSpec((None, EXPERTS_PER_STEP, D_MODEL, D_EXPERT), experts),
            pl.BlockSpec((None, EXPERTS_PER_STEP, D_EXPERT, D_MODEL), experts),
            pl.BlockSpec((1, D_MODEL), const),
        ],
        out_specs=pl.BlockSpec((MOE_ROWS, D_MODEL), lambda i, e: (i, 0)),
        out_shape=jax.ShapeDtypeStruct((rows, D_MODEL), F32),
        scratch_shapes=[pltpu.VMEM((MOE_ROWS, D_MODEL), BF16), pltpu.VMEM((MOE_ROWS, ROUTER_LANES), F32)],
        compiler_params=pltpu.CompilerParams(dimension_semantics=("parallel", "arbitrary"),
                                             vmem_limit_bytes=VMEM_LIMIT),
        name="mix_and_moe",
    )(x2d, y_ssm_tm, y_pool, y_att, w_out, norm_g.reshape(1, D_MODEL), w_router, b_router, w_gate, w_up, w_down,
      final_g.reshape(1, D_MODEL))


def _router_params(group_w, group_b, router_w, router_b):
    w = jnp.concatenate([group_w.astype(F32)] + [router_w[g].astype(F32) for g in range(MOE_GROUPS)], axis=1)
    b = jnp.concatenate([group_b.astype(F32), router_b.astype(F32).reshape(-1)])
    pad = ROUTER_ROWS - w.shape[1]
    return jnp.pad(w.T, ((0, pad), (0, 0))).astype(BF16), jnp.pad(b, (0, pad)).reshape(ROUTER_ROWS, 1)


def kernel(x, rel_bias, norm1_g, w_in, ssm_a_re, ssm_a_im, ssm_log_dt, ssm_b_re, ssm_b_im, ssm_c_re, ssm_c_im,
           ssm_d, ssm_glu_w, ssm_glu_b, pool_w, pool_b, pool_scale, w_out, norm2_g, moe_group_w, moe_group_b,
           moe_router_w, moe_router_b, moe_w_gate, moe_w_up, moe_w_down, final_norm_g):
    bsz, seq, dm = x.shape
    depth = w_in.shape[0]
    assert dm == D_MODEL and depth >= 1
    assert bsz % SUBLANES == 0, "the scan keeps one time step of all batches in whole sublane tiles"
    assert all(seq % t == 0 for t in (PROJ_ROWS, MOE_ROWS, SSM_STEPS, MOBA_BLOCK))
    n_main = SSM_WIDTH + POOL_WIDTH + 2 * ATT_WIDTH
    x2d = x.astype(F32).reshape(bsz * seq, dm)
    tabs = _bias_tables(rel_bias)
    w_in_bf = w_in.astype(BF16)
    w_vt_bf = jnp.swapaxes(w_in[:, :, n_main:], 1, 2).astype(BF16)
    w_out_bf = w_out.astype(BF16)
    w_gate_bf, w_up_bf, w_down_bf = moe_w_gate.astype(BF16), moe_w_up.astype(BF16), moe_w_down.astype(BF16)
    ssm_params = jax.vmap(_ssm_params)(ssm_a_re, ssm_a_im, ssm_log_dt, ssm_b_re, ssm_b_im, ssm_c_re, ssm_c_im)
    for l in range(depth):
        u, y_pool, q, k, vt = _in_projection(x2d, norm1_g[l], w_in_bf, w_vt_bf, l,
                                             _pool_params(pool_w[l], pool_b[l], pool_scale[l]), bsz, seq)

        y_ssm = _ssm_mixer(u, [p[l] for p in ssm_params], ssm_d[l], ssm_glu_w[l], ssm_glu_b[l], bsz, seq)

        y_att = _moba_attention(q.reshape(bsz, seq, ATT_WIDTH), k.reshape(bsz, seq, ATT_WIDTH), vt, tabs, rel_bias)

        w_router, b_router = _router_params(moe_group_w[l], moe_group_b[l], moe_router_w[l], moe_router_b[l])
        x2d = _mix_and_moe(x2d, y_ssm, y_pool, y_att.reshape(bsz * seq, ATT_WIDTH), w_out_bf, l,
                           norm2_g[l], w_router, b_router, w_gate_bf, w_up_bf, w_down_bf,
                           final_norm_g, final_norm=(l == depth - 1))
    return x2d.reshape(bsz, seq, dm).astype(x.dtype)
```

```python
import math
from functools import partial

import jax
import jax.numpy as jnp
from jax import lax
from jax.experimental import pallas as pl
from jax.experimental.pallas import tpu as pltpu

F32 = jnp.float32
BF16 = jnp.bfloat16

LANES = 128
SUBLANES = 8
D_MODEL = 1024
SSM_WIDTH = 256
POOL_WIDTH = 256
ATT_WIDTH = 512
SSM_GROUP = 16
SSM_GROUPS = SSM_WIDTH // SSM_GROUP
SSM_STATE = 64
SSM_STATES = SSM_GROUPS * SSM_STATE
POOL_WINDOWS = (2, 4, 8, 16)
POOL_GROUP = POOL_WIDTH // len(POOL_WINDOWS)
POOL_HALO = 16
HEAD_DIM = 64
HEADS = ATT_WIDTH // HEAD_DIM
HEAD_PAIR = 2 * HEAD_DIM
ONES_ROWS = 16
MOBA_BLOCK = 256
MOBA_TOPK = 3
FAR_GROUP = 4
REL_BUCKETS = 32
REL_MAX_EXACT = REL_BUCKETS // 2
REL_MAX_DIST = 128
MOE_GROUPS = 4
MOE_PER_GROUP = 4
MOE_EXPERTS = MOE_GROUPS * MOE_PER_GROUP
D_EXPERT = D_MODEL // 4
ROUTER_ROWS = 32
ROUTER_LANES = 128
RMS_EPS = 1e-6
NEG_INF = -1e30
TAKEN = -3e38
LOG2E = math.log2(math.e)

PROJ_ROWS = 1024
MOE_ROWS = 1024
EXPERTS_PER_STEP = 4
SSM_STEPS = 128
SSM_PARTS = 2
VMEM_LIMIT = 48 * 1024 * 1024

_NT = (((1,), (1,)), ((), ()))

_BUCKET_START = list(range(REL_MAX_EXACT)) + [
    math.ceil(REL_MAX_EXACT * (REL_MAX_DIST / REL_MAX_EXACT) ** (k / (REL_BUCKETS - REL_MAX_EXACT)))
    for k in range(REL_BUCKETS - REL_MAX_EXACT)
]


def _rms_norm(x, g):
    return x * lax.rsqrt(jnp.mean(x * x, axis=-1, keepdims=True) + RMS_EPS) * g


def _inproj_kernel(x_ref, g_ref, w_ref, wvt_ref, pw_ref, pb_ref, ps_ref,
                   u_ref, yp_ref, q_ref, k_ref, vt_ref, halo_ref, *, tiles_per_seq):
    h = _rms_norm(x_ref[...], g_ref[...]).astype(BF16)
    pr = jnp.dot(h, w_ref[...], preferred_element_type=F32)
    s0, s1, s2 = SSM_WIDTH, SSM_WIDTH + POOL_WIDTH, SSM_WIDTH + POOL_WIDTH + ATT_WIDTH
    u_ref[...] = pr[:, :s0]
    t_tile = pl.program_id(0) % tiles_per_seq
    p = pr[:, s0:s1]
    halo = jnp.where(t_tile == 0, 0.0, halo_ref[...])
    halo_ref[...] = p[PROJ_ROWS - POOL_HALO:, :]
    yp_ref[...] = _pool_mix(p, halo, t_tile * PROJ_ROWS, pw_ref[...], pb_ref[...], ps_ref[...]).astype(BF16)
    q_ref[...] = (pr[:, s1:s2] * (HEAD_DIM ** -0.5 * LOG2E)).astype(BF16)
    k_ref[...] = pr[:, s2:].astype(BF16)
    vt_ref[0] = lax.dot_general(wvt_ref[...], h, _NT, preferred_element_type=F32).astype(BF16)


def _pool_mix(p, halo, t0, w_bd, b, scale):
    ext = jnp.concatenate([halo, p], axis=0)
    s2 = ext + pltpu.roll(ext, 1, 0)
    s4 = s2 + pltpu.roll(s2, 2, 0)
    s8 = s4 + pltpu.roll(s4, 4, 0)
    s16 = s8 + pltpu.roll(s8, 8, 0)
    sums = (s2, s4, s8, s16)
    lane = lax.broadcasted_iota(jnp.int32, p.shape, 1)
    t1 = (lax.broadcasted_iota(jnp.int32, p.shape, 0) + t0 + 1).astype(F32)
    mean = None
    for gi, win in enumerate(POOL_WINDOWS):
        m = sums[gi][POOL_HALO:] / jnp.minimum(t1, float(win))
        mean = m if mean is None else jnp.where(lane >= gi * POOL_GROUP, m, mean)
    y = jnp.dot((mean - p).astype(BF16), w_bd, preferred_element_type=F32)
    return (y + b) * scale


def _pool_params(w, b, scale):
    eye = jnp.eye(len(POOL_WINDOWS), dtype=F32)
    w_bd = jnp.einsum('gcd,gk->gckd', w.astype(F32), eye).reshape(POOL_WIDTH, POOL_WIDTH).astype(BF16)
    return w_bd, b.astype(F32).reshape(1, POOL_WIDTH), scale.astype(F32).reshape(1, POOL_WIDTH)


def _in_projection(x2d, g, w_in, w_vt, layer, pool_params, bsz, seq):
    rows = x2d.shape[0]
    tiles_per_seq = seq // PROJ_ROWS
    n_main = SSM_WIDTH + POOL_WIDTH + 2 * ATT_WIDTH
    const = lambda i: (0, 0)
    this_layer = lambda i: (layer, 0, 0)
    row_blk = lambda width: pl.BlockSpec((PROJ_ROWS, width), lambda i: (i, 0))
    return pl.pallas_call(
        partial(_inproj_kernel, tiles_per_seq=tiles_per_seq),
        grid=(rows // PROJ_ROWS,),
        in_specs=[row_blk(D_MODEL), pl.BlockSpec((1, D_MODEL), const),
                  pl.BlockSpec((None, D_MODEL, n_main), this_layer),
                  pl.BlockSpec((None, ATT_WIDTH, D_MODEL), this_layer),
                  pl.BlockSpec((POOL_WIDTH, POOL_WIDTH), const), pl.BlockSpec((1, POOL_WIDTH), const),
                  pl.BlockSpec((1, POOL_WIDTH), const)],
        out_specs=[pl.BlockSpec((PROJ_ROWS, SSM_WIDTH), lambda i: (i % tiles_per_seq, i // tiles_per_seq)),
                   row_blk(POOL_WIDTH), row_blk(ATT_WIDTH), row_blk(ATT_WIDTH),
                   pl.BlockSpec((1, ATT_WIDTH, PROJ_ROWS),
                                lambda i: (i // tiles_per_seq, 0, i % tiles_per_seq))],
        out_shape=[jax.ShapeDtypeStruct((seq, bsz * SSM_WIDTH), F32),
                   jax.ShapeDtypeStruct((rows, POOL_WIDTH), BF16),
                   jax.ShapeDtypeStruct((rows, ATT_WIDTH), BF16),
                   jax.ShapeDtypeStruct((rows, ATT_WIDTH), BF16),
                   jax.ShapeDtypeStruct((bsz, ATT_WIDTH, seq), BF16)],
        scratch_shapes=[pltpu.VMEM((POOL_HALO, POOL_WIDTH), F32)],
        compiler_params=pltpu.CompilerParams(dimension_semantics=("arbitrary",),
                                             vmem_limit_bytes=VMEM_LIMIT),
        name="in_projection",
    )(x2d, g.reshape(1, D_MODEL), w_in, w_vt, *pool_params)


def _ssm_kernel(u_ref, bm_ref, cm_ref, ar_ref, ai_ref, d_ref, gw_ref, gb_ref, o_ref, bu_ref, st_ref, tb_ref):
    bsz = st_ref.shape[0]
    ns = SSM_STATES

    @pl.when(pl.program_id(0) == 0)
    def _init():
        st_ref[...] = jnp.zeros_like(st_ref)

    halves = SSM_WIDTH // LANES
    for b in range(bsz):
        for h in range(halves):
            c0 = b * SSM_WIDTH + h * LANES
            tb_ref[h, pl.ds(b, SSM_STEPS, stride=bsz), :] = u_ref[:, c0:c0 + LANES]
    ar = jnp.broadcast_to(ar_ref[...], (bsz, ns))
    ai = jnp.broadcast_to(ai_ref[...], (bsz, ns))
    part_steps = SSM_STEPS // SSM_PARTS
    part_rows = part_steps * bsz

    def rows_of(part):
        return slice(part * part_rows, (part + 1) * part_rows)

    def u_of(part):
        return jnp.concatenate([tb_ref[h, rows_of(part), :] for h in range(halves)], axis=1)

    def project_in(part):
        bu_ref[rows_of(part), :] = jnp.dot(u_of(part).astype(BF16), bm_ref[...], preferred_element_type=F32)

    def scan(part, carry):
        sr, si = carry
        for t in range(part_steps):
            rows = slice(part * part_rows + t * bsz, part * part_rows + (t + 1) * bsz)
            sr, si = (ar * sr - ai * si + bu_ref[rows, 0:ns], ar * si + ai * sr + bu_ref[rows, ns:2 * ns])
            bu_ref[rows, 0:ns] = sr
            bu_ref[rows, ns:2 * ns] = si
        return sr, si

    def project_out(part):
        slab = part_rows // 2
        y = jnp.concatenate(
            [jnp.dot(bu_ref[part * part_rows + i * slab:part * part_rows + (i + 1) * slab, :].astype(BF16),
                     cm_ref[...], preferred_element_type=F32) for i in range(2)], axis=0)
        y = y + d_ref[...] * u_of(part)
        y = y * (0.5 * (1.0 + jnp.tanh(math.sqrt(2.0 / math.pi) * (y + 0.044715 * (y * y * y)))))
        z = jnp.dot(y.astype(BF16), gw_ref[...], preferred_element_type=F32) + gb_ref[...]
        return y * (1.0 / (1.0 + jnp.exp(-z)))

    project_in(0)
    state, outs = (st_ref[:, 0:ns], st_ref[:, ns:2 * ns]), []
    for part in range(SSM_PARTS):
        if part + 1 < SSM_PARTS:
            project_in(part + 1)
        state = scan(part, state)
        outs.append(project_out(part))
    st_ref[:, 0:ns] = state[0]
    st_ref[:, ns:2 * ns] = state[1]
    out = jnp.concatenate(outs, axis=0)
    for h in range(halves):
        tb_ref[h] = out[:, h * LANES:(h + 1) * LANES]
    for b in range(bsz):
        for h in range(halves):
            c0 = b * SSM_WIDTH + h * LANES
            o_ref[:, c0:c0 + LANES] = tb_ref[h, pl.ds(b, SSM_STEPS, stride=bsz), :].astype(o_ref.dtype)


def _ssm_params(a_re, a_im, log_dt, b_re, b_im, c_re, c_im):
    lam = lax.complex(a_re.astype(F32), a_im.astype(F32))
    dt = jnp.exp(log_dt.astype(F32))[:, None]
    lam_bar = jnp.exp(lam * dt)
    b = lax.complex(b_re.astype(F32), b_im.astype(F32))
    b_bar = ((lam_bar - 1.0) / lam)[..., None] * b
    eye = jnp.eye(SSM_GROUPS, dtype=F32)

    def bdiag_in(m):
        return jnp.einsum('gph,gk->ghkp', m, eye).reshape(SSM_WIDTH, SSM_STATES)

    def bdiag_out(m):
        return jnp.einsum('ghp,gk->gpkh', m, eye).reshape(SSM_STATES, SSM_WIDTH)

    bm = jnp.concatenate([bdiag_in(jnp.real(b_bar)), bdiag_in(jnp.imag(b_bar))], axis=1)
    cm = jnp.concatenate([bdiag_out(c_re.astype(F32)), -bdiag_out(c_im.astype(F32))], axis=0)
    return (bm.astype(BF16), cm.astype(BF16),
            jnp.real(lam_bar).reshape(1, SSM_STATES), jnp.imag(lam_bar).reshape(1, SSM_STATES))


def _ssm_mixer(u_tm, params, d, glu_w, glu_b, bsz, seq):
    bm, cm, ar, ai = params
    rows = SSM_STEPS * bsz
    const = lambda c: (0, 0)
    return pl.pallas_call(
        _ssm_kernel,
        grid=(seq // SSM_STEPS,),
        in_specs=[
            pl.BlockSpec((SSM_STEPS, bsz * SSM_WIDTH), lambda c: (c, 0)),
            pl.BlockSpec((SSM_WIDTH, 2 * SSM_STATES), const),
            pl.BlockSpec((2 * SSM_STATES, SSM_WIDTH), const),
            pl.BlockSpec((1, SSM_STATES), const),
            pl.BlockSpec((1, SSM_STATES), const),
            pl.BlockSpec((1, SSM_WIDTH), const),
            pl.BlockSpec((SSM_WIDTH, SSM_WIDTH), const),
            pl.BlockSpec((1, SSM_WIDTH), const),
        ],
        out_specs=pl.BlockSpec((SSM_STEPS, bsz * SSM_WIDTH), lambda c: (c, 0)),
        out_shape=jax.ShapeDtypeStruct((seq, bsz * SSM_WIDTH), BF16),
        scratch_shapes=[pltpu.VMEM((rows, 2 * SSM_STATES), F32), pltpu.VMEM((bsz, 2 * SSM_STATES), F32),
                        pltpu.VMEM((SSM_WIDTH // LANES, rows, LANES), F32)],
        compiler_params=pltpu.CompilerParams(dimension_semantics=("arbitrary",),
                                             vmem_limit_bytes=VMEM_LIMIT),
        name="ssm_mixer",
    )(u_tm, bm, cm, ar, ai, d.reshape(1, SSM_WIDTH), glu_w.astype(BF16), glu_b.reshape(1, SSM_WIDTH))


def _bias_table_kernel(bias_ref, tab_ref):
    h = pl.program_id(0)
    kk = lax.broadcasted_iota(jnp.int32, (MOBA_BLOCK, MOBA_BLOCK), 0)
    qq = lax.broadcasted_iota(jnp.int32, (MOBA_BLOCK, MOBA_BLOCK), 1)
    for which in range(2):
        rel = qq - kk + which * MOBA_BLOCK
        val = jnp.full((MOBA_BLOCK, MOBA_BLOCK), bias_ref[0, h], F32)
        for b in range(1, REL_BUCKETS):
            val = jnp.where(rel >= _BUCKET_START[b], bias_ref[b, h], val)
        tab_ref[0, which] = jnp.where(rel >= 0, val * LOG2E, NEG_INF)


def _bias_tables(rel_bias):
    return pl.pallas_call(
        _bias_table_kernel,
        grid=(HEADS,),
        in_specs=[pl.BlockSpec(memory_space=pltpu.SMEM)],
        out_specs=pl.BlockSpec((1, 2, MOBA_BLOCK, MOBA_BLOCK), lambda h: (h, 0, 0, 0)),
        out_shape=jax.ShapeDtypeStruct((HEADS, 2, MOBA_BLOCK, MOBA_BLOCK), F32),
        name="bias_tables",
    )(rel_bias.astype(F32))


def _attn_kernel(bias_ref, q_ref, k_ref, vt_ref, tab_ref, o_ref,
                 qh_ref, va_ref, neg_ref, acc_ref, s_ref):
    seq = k_ref.shape[1]
    nb = seq // MOBA_BLOCK

    def prep():
        lane = lax.broadcasted_iota(jnp.int32, (1, HEAD_PAIR), 1)
        head_lanes = [lane < HEAD_DIM, lane >= HEAD_DIM]
        kf = k_ref[0].astype(F32).reshape(nb, MOBA_BLOCK, HEAD_PAIR)
        km = jnp.sum(kf, axis=1) * (1.0 / MOBA_BLOCK)
        q_all = q_ref[0].astype(F32)
        ones = jnp.ones((ONES_ROWS, seq), BF16)
        va_ref[0, 0:HEAD_DIM, :] = vt_ref[0, 0:HEAD_DIM, :]
        va_ref[0, HEAD_DIM:, :] = ones
        va_ref[1, 0:ONES_ROWS, :] = ones
        va_ref[1, ONES_ROWS:, :] = vt_ref[0, HEAD_DIM:, :]
        blk_id = lax.broadcasted_iota(jnp.int32, (nb, seq), 0)
        past = blk_id < lax.broadcasted_iota(jnp.int32, (nb, seq), 1) // MOBA_BLOCK
        for hh in range(2):
            qh_ref[hh] = jnp.where(head_lanes[hh], q_all, 0.0).astype(BF16)
            km_h = jnp.where(head_lanes[hh], km, 0.0).astype(BF16)
            g = lax.dot_general(km_h, qh_ref[hh], _NT, preferred_element_type=F32)
            g = jnp.where(past, g, NEG_INF)
            sel = None
            for _ in range(min(MOBA_TOPK, nb)):
                top = jnp.max(g, axis=0, keepdims=True)
                first = jnp.min(jnp.where(g == top, blk_id, nb), axis=0, keepdims=True)
                hit = blk_id == first
                sel = hit if sel is None else (sel | hit)
                g = jnp.where(hit, TAKEN, g)
            neg = jnp.where(sel & past, 0.0, NEG_INF)
            for qb in range(nb):
                neg_ref[hh, qb] = neg[:, qb * MOBA_BLOCK:(qb + 1) * MOBA_BLOCK]

    prep()

    def q_block(qi, carry):
        for qv in range(nb):
            pl.when(qi == qv)(partial(_attn_q_block_static, qv, pl.program_id(1), nb, bias_ref, k_ref, tab_ref,
                                      o_ref, qh_ref, va_ref, neg_ref, acc_ref, s_ref))
        return carry

    lax.fori_loop(0, nb, q_block, 0)


def _attn_q_block_static(qv, pr, nb, bias_ref, k_ref, tab_ref, o_ref, qh_ref, va_ref, neg_ref, acc_ref, s_ref):
    rows = lambda j: slice(j * MOBA_BLOCK, (j + 1) * MOBA_BLOCK)
    qh = [qh_ref[hh, rows(qv), :] for hh in range(2)]

    def score(hh, j):
        return lax.dot_general(k_ref[0, rows(j), :], qh[hh], _NT, preferred_element_type=F32)

    def finalize(qb):
        oa = acc_ref[0]
        ob = acc_ref[1]
        out_t = jnp.concatenate([oa[0:HEAD_DIM] / oa[HEAD_DIM:HEAD_DIM + 1],
                                 ob[ONES_ROWS:] / ob[0:1]], axis=0)
        o_ref[0, rows(qb), :] = out_t.T.astype(o_ref.dtype)

    def add_of(hh, j):
        if j == qv:
            return tab_ref[hh, 0]
        mask_row = neg_ref[hh, qv, j:j + 1, :]
        if j == qv - 1:
            return tab_ref[hh, 1] + mask_row
        return mask_row + bias_ref[REL_BUCKETS - 1, 2 * pr + hh] * LOG2E

    if qv > 0:
        finalize(qv - 1)
    blocks = [qv] + ([qv - 1] if qv > 0 else []) + list(range(qv - 1))
    groups = [blocks[i:i + FAR_GROUP] for i in range(0, len(blocks), FAR_GROUP)]
    scores = {(hh, j): score(hh, j) for j in groups[0] for hh in range(2)}
    m, acc, staged = [None, None], [None, None], 0
    for gi, group in enumerate(groups):
        units = [(hh, j) for j in group for hh in range(2)]
        ahead = [(hh, j) for j in (groups[gi + 1] if gi + 1 < len(groups) else []) for hh in range(2)]
        per_unit = -(-len(ahead) // len(units))
        nxt, mbs, pvs = {}, {}, {}
        for n, (hh, j) in enumerate(units):
            for ahh, aj in ahead[n * per_unit:(n + 1) * per_unit]:
                nxt[ahh, aj] = score(ahh, aj)
            s, add = scores[hh, j], add_of(hh, j)
            if add.shape[0] == 1:
                smax = jnp.max(s, axis=0, keepdims=True)
                mbs[hh, j] = smax + add
                p = jnp.exp2(s - smax)
            else:
                s_ref[staged] = s
                s = s_ref[staged] + add
                staged += 1
                mbs[hh, j] = jnp.max(s, axis=0, keepdims=True)
                p = jnp.exp2(s - mbs[hh, j])
            pvs[hh, j] = jnp.dot(va_ref[hh, :, rows(j)], p.astype(BF16), preferred_element_type=F32)
        for hh in range(2):
            m_new = m[hh]
            for j in group:
                m_new = mbs[hh, j] if m_new is None else jnp.maximum(m_new, mbs[hh, j])
            total = None if m[hh] is None else acc[hh] * jnp.exp2(m[hh] - m_new)
            for j in group:
                term = pvs[hh, j] * jnp.exp2(mbs[hh, j] - m_new)
                total = term if total is None else total + term
            m[hh], acc[hh] = m_new, total
        scores = nxt
    acc_ref[0] = acc[0]
    acc_ref[1] = acc[1]
    if qv == nb - 1:
        finalize(qv)


def _moba_attention(q, k, vt, tabs, rel_bias):
    bsz, seq, _ = q.shape
    assert seq % MOBA_BLOCK == 0
    nb = seq // MOBA_BLOCK
    return pl.pallas_call(
        _attn_kernel,
        grid=(bsz, HEADS // 2),
        in_specs=[
            pl.BlockSpec(memory_space=pltpu.SMEM),
            pl.BlockSpec((1, seq, HEAD_PAIR), lambda b, p: (b, 0, p)),
            pl.BlockSpec((1, seq, HEAD_PAIR), lambda b, p: (b, 0, p)),
            pl.BlockSpec((1, HEAD_PAIR, seq), lambda b, p: (b, p, 0)),
            pl.BlockSpec((2, 2, MOBA_BLOCK, MOBA_BLOCK), lambda b, p: (p, 0, 0, 0)),
        ],
        out_specs=pl.BlockSpec((1, seq, HEAD_PAIR), lambda b, p: (b, 0, p)),
        out_shape=jax.ShapeDtypeStruct((bsz, seq, ATT_WIDTH), BF16),
        scratch_shapes=[
            pltpu.VMEM((2, seq, HEAD_PAIR), BF16),
            pltpu.VMEM((2, HEAD_DIM + ONES_ROWS, seq), BF16),
            pltpu.VMEM((2, nb, nb, MOBA_BLOCK), F32),
            pltpu.VMEM((2, HEAD_DIM + ONES_ROWS, MOBA_BLOCK), F32),
            pltpu.VMEM((4, MOBA_BLOCK, MOBA_BLOCK), F32),
        ],
        compiler_params=pltpu.CompilerParams(
            dimension_semantics=("parallel", "parallel"), vmem_limit_bytes=VMEM_LIMIT),
        name="moba_attention",
    )(rel_bias.astype(F32), q, k, vt, tabs)


def _routing_weights(logits_t):
    col = lambda i: logits_t[i:i + 1, :]
    gl = [col(g) for g in range(MOE_GROUPS)]
    gmax = _max_of(gl)
    sel, taken = [], None
    for g in range(MOE_GROUPS):
        hit = gl[g] == gmax
        if taken is not None:
            hit = hit & jnp.logical_not(taken)
        taken = hit if taken is None else (taken | hit)
        sel.append(hit)
    denom = None
    for g in range(MOE_GROUPS):
        e = jnp.exp(gl[g] - gmax)
        denom = e if denom is None else denom + e
    g_wt = 1.0 / denom
    el = []
    for e in range(MOE_PER_GROUP):
        v = col(MOE_GROUPS + (MOE_GROUPS - 1) * MOE_PER_GROUP + e)
        for g in range(MOE_GROUPS - 2, -1, -1):
            v = jnp.where(sel[g], col(MOE_GROUPS + g * MOE_PER_GROUP + e), v)
        el.append(v)
    rank = []
    for e in range(MOE_PER_GROUP):
        r = jnp.zeros_like(el[e], dtype=jnp.int32)
        for m in range(MOE_PER_GROUP):
            if m == e:
                continue
            beats = (el[m] >= el[e]) if m < e else (el[m] > el[e])
            r = r + beats.astype(jnp.int32)
        rank.append(r)
    v1 = _max_of(el)
    v2 = None
    for e in range(MOE_PER_GROUP):
        c = jnp.where(rank[e] == 1, el[e], 0.0)
        v2 = c if v2 is None else v2 + c
    e2 = jnp.exp(v2 - v1)
    w1 = 1.0 / (1.0 + e2)
    w2 = e2 / (1.0 + e2)
    tokens = logits_t.shape[1]
    row_id = lax.broadcasted_iota(jnp.int32, (MOE_EXPERTS, tokens), 0)
    comb_t = jnp.zeros((MOE_EXPERTS, tokens), F32)
    for g in range(MOE_GROUPS):
        for e in range(MOE_PER_GROUP):
            within = jnp.where(rank[e] == 0, w1, jnp.where(rank[e] == 1, w2, 0.0))
            c = jnp.where(sel[g], g_wt * within, 0.0)
            comb_t = jnp.where(row_id == g * MOE_PER_GROUP + e, c, comb_t)
    pad = jnp.zeros((ROUTER_LANES - MOE_EXPERTS, tokens), F32)
    return jnp.concatenate([comb_t, pad], axis=0).T


def _max_of(cols):
    out = cols[0]
    for c in cols[1:]:
        out = jnp.maximum(out, c)
    return out


def _moe_kernel(x_ref, ys_ref, yp_ref, ya_ref, wo_ref, g_ref, wr_ref, br_ref, wg_ref, wu_ref, wd_ref, fg_ref,
                o_ref, h_ref, comb_ref, *, final_norm):
    step = pl.program_id(1)

    @pl.when(step == 0)
    def _mix_and_route():
        s0, s1 = SSM_WIDTH, SSM_WIDTH + POOL_WIDTH
        mix = jnp.dot(ys_ref[...], wo_ref[0:s0, :], preferred_element_type=F32)
        mix = mix + jnp.dot(yp_ref[...], wo_ref[s0:s1, :], preferred_element_type=F32)
        mix = mix + jnp.dot(ya_ref[...], wo_ref[s1:, :], preferred_element_type=F32)
        x = x_ref[...] + mix
        h = _rms_norm(x, g_ref[...]).astype(BF16)
        h_ref[...] = h
        logits_t = lax.dot_general(wr_ref[...], h, _NT, preferred_element_type=F32)
        comb_ref[...] = _routing_weights(logits_t + br_ref[...])
        o_ref[...] = x

    h = h_ref[...]
    lane = lax.broadcasted_iota(jnp.int32, comb_ref.shape, 1)
    y = None
    for j in range(EXPERTS_PER_STEP):
        hg = jnp.dot(h, wg_ref[j], preferred_element_type=F32)
        hu = jnp.dot(h, wu_ref[j], preferred_element_type=F32)
        c = jnp.sum(jnp.where(lane == step * EXPERTS_PER_STEP + j, comb_ref[...], 0.0), axis=1, keepdims=True)
        act = (hg * (1.0 / (1.0 + jnp.exp(-hg)))) * hu * c
        yj = jnp.dot(act.astype(BF16), wd_ref[j], preferred_element_type=F32)
        y = yj if y is None else y + yj
    o_ref[...] += y

    if final_norm:
        @pl.when(step == MOE_EXPERTS // EXPERTS_PER_STEP - 1)
        def _final():
            o_ref[...] = _rms_norm(o_ref[...], fg_ref[...])


def _mix_and_moe(x2d, y_ssm_tm, y_pool, y_att, w_out, layer, norm_g, w_router, b_router, w_gate, w_up, w_down,
                 final_g, final_norm):
    rows = x2d.shape[0]
    tiles_per_seq = y_ssm_tm.shape[0] // MOE_ROWS
    const = lambda i, e: (0, 0)
    row_blk = lambda width: pl.BlockSpec((MOE_ROWS, width), lambda i, e: (i, 0))
    experts = lambda i, e: (layer, e, 0, 0)
    return pl.pallas_call(
        partial(_moe_kernel, final_norm=final_norm),
        grid=(rows // MOE_ROWS, MOE_EXPERTS // EXPERTS_PER_STEP),
        in_specs=[
            row_blk(D_MODEL),
            pl.BlockSpec((MOE_ROWS, SSM_WIDTH), lambda i, e: (i % tiles_per_seq, i // tiles_per_seq)),
            row_blk(POOL_WIDTH),
            row_blk(ATT_WIDTH),
            pl.BlockSpec((None, D_MODEL, D_MODEL), lambda i, e: (layer, 0, 0)),
            pl.BlockSpec((1, D_MODEL), const),
            pl.BlockSpec((ROUTER_ROWS, D_MODEL), const),
            pl.BlockSpec((ROUTER_ROWS, 1), const),
            pl.BlockSpec((None, EXPERTS_PER_STEP, D_MODEL, D_EXPERT), experts),
            pl.BlockSpec((None, EXPERTS_PER_STEP, D_MODEL, D_EXPERT), experts),
            pl.BlockSpec((None, EXPERTS_PER_STEP, D_EXPERT, D_MODEL), experts),
            pl.BlockSpec((1, D_MODEL), const),
        ],
        out_specs=pl.BlockSpec((MOE_ROWS, D_MODEL), lambda i, e: (i, 0)),
        out_shape=jax.ShapeDtypeStruct((rows, D_MODEL), F32),
        scratch_shapes=[pltpu.VMEM((MOE_ROWS, D_MODEL), BF16), pltpu.VMEM((MOE_ROWS, ROUTER_LANES), F32)],
        compiler_params=pltpu.CompilerParams(dimension_semantics=("parallel", "arbitrary"),
                                             vmem_limit_bytes=VMEM_LIMIT),
        name="mix_and_moe",
    )(x2d, y_ssm_tm, y_pool, y_att, w_out, norm_g.reshape(1, D_MODEL), w_router, b_router, w_gate, w_up, w_down,
      final_g.reshape(1, D_MODEL))


def _router_params(group_w, group_b, router_w, router_b):
    w = jnp.concatenate([group_w.astype(F32)] + [router_w[g].astype(F32) for g in range(MOE_GROUPS)], axis=1)
    b = jnp.concatenate([group_b.astype(F32), router_b.astype(F32).reshape(-1)])
    pad = ROUTER_ROWS - w.shape[1]
    return jnp.pad(w.T, ((0, pad), (0, 0))).astype(BF16), jnp.pad(b, (0, pad)).reshape(ROUTER_ROWS, 1)


def kernel(x, rel_bias, norm1_g, w_in, ssm_a_re, ssm_a_im, ssm_log_dt, ssm_b_re, ssm_b_im, ssm_c_re, ssm_c_im,
           ssm_d, ssm_glu_w, ssm_glu_b, pool_w, pool_b, pool_scale, w_out, norm2_g, moe_group_w, moe_group_b,
           moe_router_w, moe_router_b, moe_w_gate, moe_w_up, moe_w_down, final_norm_g):
    bsz, seq, dm = x.shape
    depth = w_in.shape[0]
    assert dm == D_MODEL and depth >= 1
    assert bsz % SUBLANES == 0, "the scan keeps one time step of all batches in whole sublane tiles"
    assert all(seq % t == 0 for t in (PROJ_ROWS, MOE_ROWS, SSM_STEPS, MOBA_BLOCK))
    n_main = SSM_WIDTH + POOL_WIDTH + 2 * ATT_WIDTH
    x2d = x.astype(F32).reshape(bsz * seq, dm)
    tabs = _bias_tables(rel_bias)
    w_in_bf = w_in.astype(BF16)
    w_vt_bf = jnp.swapaxes(w_in[:, :, n_main:], 1, 2).astype(BF16)
    w_out_bf = w_out.astype(BF16)
    w_gate_bf, w_up_bf, w_down_bf = moe_w_gate.astype(BF16), moe_w_up.astype(BF16), moe_w_down.astype(BF16)
    ssm_params = jax.vmap(_ssm_params)(ssm_a_re, ssm_a_im, ssm_log_dt, ssm_b_re, ssm_b_im, ssm_c_re, ssm_c_im)
    for l in range(depth):
        u, y_pool, q, k, vt = _in_projection(x2d, norm1_g[l], w_in_bf, w_vt_bf, l,
                                             _pool_params(pool_w[l], pool_b[l], pool_scale[l]), bsz, seq)

        y_ssm = _ssm_mixer(u, [p[l] for p in ssm_params], ssm_d[l], ssm_glu_w[l], ssm_glu_b[l], bsz, seq)

        y_att = _moba_attention(q.reshape(bsz, seq, ATT_WIDTH), k.reshape(bsz, seq, ATT_WIDTH), vt, tabs, rel_bias)

        w_router, b_router = _router_params(moe_group_w[l], moe_group_b[l], moe_router_w[l], moe_router_b[l])
        x2d = _mix_and_moe(x2d, y_ssm, y_pool, y_att.reshape(bsz * seq, ATT_WIDTH), w_out_bf, l,
                           norm2_g[l], w_router, b_router, w_gate_bf, w_up_bf, w_down_bf,
                           final_norm_g, final_norm=(l == depth - 1))
    return x2d.reshape(bsz, seq, dm).astype(x.dtype)
```

```python
import math
from functools import partial

import jax
import jax.numpy as jnp
from jax import lax
from jax.experimental import pallas as pl
from jax.experimental.pallas import tpu as pltpu

F32 = jnp.float32
BF16 = jnp.bfloat16

LANES = 128
SUBLANES = 8
D_MODEL = 1024
SSM_WIDTH = 256
POOL_WIDTH = 256
ATT_WIDTH = 512
SSM_GROUP = 16
SSM_GROUPS = SSM_WIDTH // SSM_GROUP
SSM_STATE = 64
SSM_STATES = SSM_GROUPS * SSM_STATE
POOL_WINDOWS = (2, 4, 8, 16)
POOL_GROUP = POOL_WIDTH // len(POOL_WINDOWS)
POOL_HALO = 16
HEAD_DIM = 64
HEADS = ATT_WIDTH // HEAD_DIM
HEAD_PAIR = 2 * HEAD_DIM
ONES_ROWS = 16
MOBA_BLOCK = 256
MOBA_TOPK = 3
FAR_GROUP = 4
REL_BUCKETS = 32
REL_MAX_EXACT = REL_BUCKETS // 2
REL_MAX_DIST = 128
MOE_GROUPS = 4
MOE_PER_GROUP = 4
MOE_EXPERTS = MOE_GROUPS * MOE_PER_GROUP
D_EXPERT = D_MODEL // 4
ROUTER_ROWS = 32
ROUTER_LANES = 128
RMS_EPS = 1e-6
NEG_INF = -1e30
TAKEN = -3e38
LOG2E = math.log2(math.e)

PROJ_ROWS = 1024
MOE_ROWS = 1024
EXPERTS_PER_STEP = 4
SSM_STEPS = 128
SSM_PARTS = 2
VMEM_LIMIT = 48 * 1024 * 1024

_NT = (((1,), (1,)), ((), ()))

_BUCKET_START = list(range(REL_MAX_EXACT)) + [
    math.ceil(REL_MAX_EXACT * (REL_MAX_DIST / REL_MAX_EXACT) ** (k / (REL_BUCKETS - REL_MAX_EXACT)))
    for k in range(REL_BUCKETS - REL_MAX_EXACT)
]


def _rms_norm(x, g):
    return x * lax.rsqrt(jnp.mean(x * x, axis=-1, keepdims=True) + RMS_EPS) * g


def _inproj_kernel(x_ref, g_ref, w_ref, wvt_ref, pw_ref, pb_ref, ps_ref,
                   u_ref, yp_ref, q_ref, k_ref, vt_ref, halo_ref, *, tiles_per_seq):
    h = _rms_norm(x_ref[...], g_ref[...]).astype(BF16)
    pr = jnp.dot(h, w_ref[...], preferred_element_type=F32)
    s0, s1, s2 = SSM_WIDTH, SSM_WIDTH + POOL_WIDTH, SSM_WIDTH + POOL_WIDTH + ATT_WIDTH
    u_ref[...] = pr[:, :s0]
    t_tile = pl.program_id(0) % tiles_per_seq
    p = pr[:, s0:s1]
    halo = jnp.where(t_tile == 0, 0.0, halo_ref[...])
    halo_ref[...] = p[PROJ_ROWS - POOL_HALO:, :]
    yp_ref[...] = _pool_mix(p, halo, t_tile * PROJ_ROWS, pw_ref[...], pb_ref[...], ps_ref[...]).astype(BF16)
    q_ref[...] = (pr[:, s1:s2] * (HEAD_DIM ** -0.5 * LOG2E)).astype(BF16)
    k_ref[...] = pr[:, s2:].astype(BF16)
    vt_ref[0] = lax.dot_general(wvt_ref[...], h, _NT, preferred_element_type=F32).astype(BF16)


def _pool_mix(p, halo, t0, w_bd, b, scale):
    ext = jnp.concatenate([halo, p], axis=0)
    s2 = ext + pltpu.roll(ext, 1, 0)
    s4 = s2 + pltpu.roll(s2, 2, 0)
    s8 = s4 + pltpu.roll(s4, 4, 0)
    s16 = s8 + pltpu.roll(s8, 8, 0)
    sums = (s2, s4, s8, s16)
    lane = lax.broadcasted_iota(jnp.int32, p.shape, 1)
    t1 = (lax.broadcasted_iota(jnp.int32, p.shape, 0) + t0 + 1).astype(F32)
    mean = None
    for gi, win in enumerate(POOL_WINDOWS):
        m = sums[gi][POOL_HALO:] / jnp.minimum(t1, float(win))
        mean = m if mean is None else jnp.where(lane >= gi * POOL_GROUP, m, mean)
    y = jnp.dot((mean - p).astype(BF16), w_bd, preferred_element_type=F32)
    return (y + b) * scale


def _pool_params(w, b, scale):
    eye = jnp.eye(len(POOL_WINDOWS), dtype=F32)
    w_bd = jnp.einsum('gcd,gk->gckd', w.astype(F32), eye).reshape(POOL_WIDTH, POOL_WIDTH).astype(BF16)
    return w_bd, b.astype(F32).reshape(1, POOL_WIDTH), scale.astype(F32).reshape(1, POOL_WIDTH)


def _in_projection(x2d, g, w_in, w_vt, layer, pool_params, bsz, seq):
    rows = x2d.shape[0]
    tiles_per_seq = seq // PROJ_ROWS
    n_main = SSM_WIDTH + POOL_WIDTH + 2 * ATT_WIDTH
    const = lambda i: (0, 0)
    this_layer = lambda i: (layer, 0, 0)
    row_blk = lambda width: pl.BlockSpec((PROJ_ROWS, width), lambda i: (i, 0))
    return pl.pallas_call(
        partial(_inproj_kernel, tiles_per_seq=tiles_per_seq),
        grid=(rows // PROJ_ROWS,),
        in_specs=[row_blk(D_MODEL), pl.BlockSpec((1, D_MODEL), const),
                  pl.BlockSpec((None, D_MODEL, n_main), this_layer),
                  pl.BlockSpec((None, ATT_WIDTH, D_MODEL), this_layer),
                  pl.BlockSpec((POOL_WIDTH, POOL_WIDTH), const), pl.BlockSpec((1, POOL_WIDTH), const),
                  pl.BlockSpec((1, POOL_WIDTH), const)],
        out_specs=[pl.BlockSpec((PROJ_ROWS, SSM_WIDTH), lambda i: (i % tiles_per_seq, i // tiles_per_seq)),
                   row_blk(POOL_WIDTH), row_blk(ATT_WIDTH), row_blk(ATT_WIDTH),
                   pl.BlockSpec((1, ATT_WIDTH, PROJ_ROWS),
                                lambda i: (i // tiles_per_seq, 0, i % tiles_per_seq))],
        out_shape=[jax.ShapeDtypeStruct((seq, bsz * SSM_WIDTH), F32),
                   jax.ShapeDtypeStruct((rows, POOL_WIDTH), BF16),
                   jax.ShapeDtypeStruct((rows, ATT_WIDTH), BF16),
                   jax.ShapeDtypeStruct((rows, ATT_WIDTH), BF16),
                   jax.ShapeDtypeStruct((bsz, ATT_WIDTH, seq), BF16)],
        scratch_shapes=[pltpu.VMEM((POOL_HALO, POOL_WIDTH), F32)],
        compiler_params=pltpu.CompilerParams(dimension_semantics=("arbitrary",),
                                             vmem_limit_bytes=VMEM_LIMIT),
        name="in_projection",
    )(x2d, g.reshape(1, D_MODEL), w_in, w_vt, *pool_params)


def _ssm_kernel(u_ref, bm_ref, cm_ref, ar_ref, ai_ref, d_ref, gw_ref, gb_ref, o_ref, bu_ref, st_ref, tb_ref):
    bsz = st_ref.shape[0]
    ns = SSM_STATES

    @pl.when(pl.program_id(0) == 0)
    def _init():
        st_ref[...] = jnp.zeros_like(st_ref)

    halves = SSM_WIDTH // LANES
    for b in range(bsz):
        for h in range(halves):
            c0 = b * SSM_WIDTH + h * LANES
            tb_ref[h, pl.ds(b, SSM_STEPS, stride=bsz), :] = u_ref[:, c0:c0 + LANES]
    ar = jnp.broadcast_to(ar_ref[...], (bsz, ns))
    ai = jnp.broadcast_to(ai_ref[...], (bsz, ns))
    part_steps = SSM_STEPS // SSM_PARTS
    part_rows = part_steps * bsz

    def rows_of(part):
        return slice(part * part_rows, (part + 1) * part_rows)

    def u_of(part):
        return jnp.concatenate([tb_ref[h, rows_of(part), :] for h in range(halves)], axis=1)

    def project_in(part):
        bu_ref[rows_of(part), :] = jnp.dot(u_of(part).astype(BF16), bm_ref[...], preferred_element_type=F32)

    def scan(part, carry):
        sr, si = carry
        for t in range(part_steps):
            rows = slice(part * part_rows + t * bsz, part * part_rows + (t + 1) * bsz)
            sr, si = (ar * sr - ai * si + bu_ref[rows, 0:ns], ar * si + ai * sr + bu_ref[rows, ns:2 * ns])
            bu_ref[rows, 0:ns] = sr
            bu_ref[rows, ns:2 * ns] = si
        return sr, si

    def project_out(part):
        slab = part_rows // 2
        y = jnp.concatenate(
            [jnp.dot(bu_ref[part * part_rows + i * slab:part * part_rows + (i + 1) * slab, :].astype(BF16),
                     cm_ref[...], preferred_element_type=F32) for i in range(2)], axis=0)
        y = y + d_ref[...] * u_of(part)
        y = y * (0.5 * (1.0 + jnp.tanh(math.sqrt(2.0 / math.pi) * (y + 0.044715 * (y * y * y)))))
        z = jnp.dot(y.astype(BF16), gw_ref[...], preferred_element_type=F32) + gb_ref[...]
        return y * (1.0 / (1.0 + jnp.exp(-z)))

    project_in(0)
    state, outs = (st_ref[:, 0:ns], st_ref[:, ns:2 * ns]), []
    for part in range(SSM_PARTS):
        if part + 1 < SSM_PARTS:
            project_in(part + 1)
        state = scan(part, state)
        outs.append(project_out(part))
    st_ref[:, 0:ns] = state[0]
    st_ref[:, ns:2 * ns] = state[1]
    out = jnp.concatenate(outs, axis=0)
    for h in range(halves):
        tb_ref[h] = out[:, h * LANES:(h + 1) * LANES]
    for b in range(bsz):
        for h in range(halves):
            c0 = b * SSM_WIDTH + h * LANES
            o_ref[:, c0:c0 + LANES] = tb_ref[h, pl.ds(b, SSM_STEPS, stride=bsz), :].astype(o_ref.dtype)


def _ssm_params(a_re, a_im, log_dt, b_re, b_im, c_re, c_im):
    lam = lax.complex(a_re.astype(F32), a_im.astype(F32))
    dt = jnp.exp(log_dt.astype(F32))[:, None]
    lam_bar = jnp.exp(lam * dt)
    b = lax.complex(b_re.astype(F32), b_im.astype(F32))
    b_bar = ((lam_bar - 1.0) / lam)[..., None] * b
    eye = jnp.eye(SSM_GROUPS, dtype=F32)

    def bdiag_in(m):
        return jnp.einsum('gph,gk->ghkp', m, eye).reshape(SSM_WIDTH, SSM_STATES)

    def bdiag_out(m):
        return jnp.einsum('ghp,gk->gpkh', m, eye).reshape(SSM_STATES, SSM_WIDTH)

    bm = jnp.concatenate([bdiag_in(jnp.real(b_bar)), bdiag_in(jnp.imag(b_bar))], axis=1)
    cm = jnp.concatenate([bdiag_out(c_re.astype(F32)), -bdiag_out(c_im.astype(F32))], axis=0)
    return (bm.astype(BF16), cm.astype(BF16),
            jnp.real(lam_bar).reshape(1, SSM_STATES), jnp.imag(lam_bar).reshape(1, SSM_STATES))


def _ssm_mixer(u_tm, params, d, glu_w, glu_b, bsz, seq):
    bm, cm, ar, ai = params
    rows = SSM_STEPS * bsz
    const = lambda c: (0, 0)
    return pl.pallas_call(
        _ssm_kernel,
        grid=(seq // SSM_STEPS,),
        in_specs=[
            pl.BlockSpec((SSM_STEPS, bsz * SSM_WIDTH), lambda c: (c, 0)),
            pl.BlockSpec((SSM_WIDTH, 2 * SSM_STATES), const),
            pl.BlockSpec((2 * SSM_STATES, SSM_WIDTH), const),
            pl.BlockSpec((1, SSM_STATES), const),
            pl.BlockSpec((1, SSM_STATES), const),
            pl.BlockSpec((1, SSM_WIDTH), const),
            pl.BlockSpec((SSM_WIDTH, SSM_WIDTH), const),
            pl.BlockSpec((1, SSM_WIDTH), const),
        ],
        out_specs=pl.BlockSpec((SSM_STEPS, bsz * SSM_WIDTH), lambda c: (c, 0)),
        out_shape=jax.ShapeDtypeStruct((seq, bsz * SSM_WIDTH), BF16),
        scratch_shapes=[pltpu.VMEM((rows, 2 * SSM_STATES), F32), pltpu.VMEM((bsz, 2 * SSM_STATES), F32),
                        pltpu.VMEM((SSM_WIDTH // LANES, rows, LANES), F32)],
        compiler_params=pltpu.CompilerParams(dimension_semantics=("arbitrary",),
                                             vmem_limit_bytes=VMEM_LIMIT),
        name="ssm_mixer",
    )(u_tm, bm, cm, ar, ai, d.reshape(1, SSM_WIDTH), glu_w.astype(BF16), glu_b.reshape(1, SSM_WIDTH))


def _bias_table_kernel(bias_ref, tab_ref):
    h = pl.program_id(0)
    kk = lax.broadcasted_iota(jnp.int32, (MOBA_BLOCK, MOBA_BLOCK), 0)
    qq = lax.broadcasted_iota(jnp.int32, (MOBA_BLOCK, MOBA_BLOCK), 1)
    for which in range(2):
        rel = qq - kk + which * MOBA_BLOCK
        val = jnp.full((MOBA_BLOCK, MOBA_BLOCK), bias_ref[0, h], F32)
        for b in range(1, REL_BUCKETS):
            val = jnp.where(rel >= _BUCKET_START[b], bias_ref[b, h], val)
        tab_ref[0, which] = jnp.where(rel >= 0, val * LOG2E, NEG_INF)


def _bias_tables(rel_bias):
    return pl.pallas_call(
        _bias_table_kernel,
        grid=(HEADS,),
        in_specs=[pl.BlockSpec(memory_space=pltpu.SMEM)],
        out_specs=pl.BlockSpec((1, 2, MOBA_BLOCK, MOBA_BLOCK), lambda h: (h, 0, 0, 0)),
        out_shape=jax.ShapeDtypeStruct((HEADS, 2, MOBA_BLOCK, MOBA_BLOCK), F32),
        name="bias_tables",
    )(rel_bias.astype(F32))


def _attn_kernel(bias_ref, q_ref, k_ref, vt_ref, tab_ref, o_ref,
                 qh_ref, va_ref, neg_ref, acc_ref, s_ref):
    seq = k_ref.shape[1]
    nb = seq // MOBA_BLOCK

    def prep():
        lane = lax.broadcasted_iota(jnp.int32, (1, HEAD_PAIR), 1)
        head_lanes = [lane < HEAD_DIM, lane >= HEAD_DIM]
        kf = k_ref[0].astype(F32).reshape(nb, MOBA_BLOCK, HEAD_PAIR)
        km = jnp.sum(kf, axis=1) * (1.0 / MOBA_BLOCK)
        q_all = q_ref[0].astype(F32)
        ones = jnp.ones((ONES_ROWS, seq), BF16)
        va_ref[0, 0:HEAD_DIM, :] = vt_ref[0, 0:HEAD_DIM, :]
        va_ref[0, HEAD_DIM:, :] = ones
        va_ref[1, 0:ONES_ROWS, :] = ones
        va_ref[1, ONES_ROWS:, :] = vt_ref[0, HEAD_DIM:, :]
        blk_id = lax.broadcasted_iota(jnp.int32, (nb, seq), 0)
        past = blk_id < lax.broadcasted_iota(jnp.int32, (nb, seq), 1) // MOBA_BLOCK
        for hh in range(2):
            qh_ref[hh] = jnp.where(head_lanes[hh], q_all, 0.0).astype(BF16)
            km_h = jnp.where(head_lanes[hh], km, 0.0).astype(BF16)
            g = lax.dot_general(km_h, qh_ref[hh], _NT, preferred_element_type=F32)
            g = jnp.where(past, g, NEG_INF)
            sel = None
            for _ in range(min(MOBA_TOPK, nb)):
                top = jnp.max(g, axis=0, keepdims=True)
                first = jnp.min(jnp.where(g == top, blk_id, nb), axis=0, keepdims=True)
                hit = blk_id == first
                sel = hit if sel is None else (sel | hit)
                g = jnp.where(hit, TAKEN, g)
            neg = jnp.where(sel & past, 0.0, NEG_INF)
            for qb in range(nb):
                neg_ref[hh, qb] = neg[:, qb * MOBA_BLOCK:(qb + 1) * MOBA_BLOCK]

    prep()

    for qv in range(nb):
        _attn_q_block_static(qv, pl.program_id(1), nb, bias_ref, k_ref, tab_ref, o_ref,
                             qh_ref, va_ref, neg_ref, acc_ref, s_ref)


def _attn_q_block_static(qv, pr, nb, bias_ref, k_ref, tab_ref, o_ref, qh_ref, va_ref, neg_ref, acc_ref, s_ref):
    rows = lambda j: slice(j * MOBA_BLOCK, (j + 1) * MOBA_BLOCK)
    qh = [qh_ref[hh, rows(qv), :] for hh in range(2)]

    def score(hh, j):
        return lax.dot_general(k_ref[0, rows(j), :], qh[hh], _NT, preferred_element_type=F32)

    def finalize(qb):
        oa = acc_ref[0]
        ob = acc_ref[1]
        out_t = jnp.concatenate([oa[0:HEAD_DIM] / oa[HEAD_DIM:HEAD_DIM + 1],
                                 ob[ONES_ROWS:] / ob[0:1]], axis=0)
        o_ref[0, rows(qb), :] = out_t.T.astype(o_ref.dtype)

    def add_of(hh, j):
        if j == qv:
            return tab_ref[hh, 0]
        mask_row = neg_ref[hh, qv, j:j + 1, :]
        if j == qv - 1:
            return tab_ref[hh, 1] + mask_row
        return mask_row + bias_ref[REL_BUCKETS - 1, 2 * pr + hh] * LOG2E

    if qv > 0:
        finalize(qv - 1)
    blocks = [qv] + ([qv - 1] if qv > 0 else []) + list(range(qv - 1))
    groups = [blocks[i:i + FAR_GROUP] for i in range(0, len(blocks), FAR_GROUP)]
    scores = {(hh, j): score(hh, j) for j in groups[0] for hh in range(2)}
    m, acc, staged = [None, None], [None, None], 0
    for gi, group in enumerate(groups):
        units = [(hh, j) for j in group for hh in range(2)]
        ahead = [(hh, j) for j in (groups[gi + 1] if gi + 1 < len(groups) else []) for hh in range(2)]
        per_unit = -(-len(ahead) // len(units))
        nxt, mbs, pvs = {}, {}, {}
        for n, (hh, j) in enumerate(units):
            for ahh, aj in ahead[n * per_unit:(n + 1) * per_unit]:
                nxt[ahh, aj] = score(ahh, aj)
            s, add = scores[hh, j], add_of(hh, j)
            if add.shape[0] == 1:
                smax = jnp.max(s, axis=0, keepdims=True)
                mbs[hh, j] = smax + add
                p = jnp.exp2(s - smax)
            else:
                s_ref[staged] = s
                s = s_ref[staged] + add
                staged += 1
                mbs[hh, j] = jnp.max(s, axis=0, keepdims=True)
                p = jnp.exp2(s - mbs[hh, j])
            pvs[hh, j] = jnp.dot(va_ref[hh, :, rows(j)], p.astype(BF16), preferred_element_type=F32)
        for hh in range(2):
            m_new = m[hh]
            for j in group:
                m_new = mbs[hh, j] if m_new is None else jnp.maximum(m_new, mbs[hh, j])
            total = None if m[hh] is None else acc[hh] * jnp.exp2(m[hh] - m_new)
            for j in group:
                term = pvs[hh, j] * jnp.exp2(mbs[hh, j] - m_new)
                total = term if total is None else total + term
            m[hh], acc[hh] = m_new, total
        scores = nxt
    acc_ref[0] = acc[0]
    acc_ref[1] = acc[1]
    if qv == nb - 1:
        finalize(qv)


def _moba_attention(q, k, vt, tabs, rel_bias):
    bsz, seq, _ = q.shape
    assert seq % MOBA_BLOCK == 0
    nb = seq // MOBA_BLOCK
    return pl.pallas_call(
        _attn_kernel,
        grid=(bsz, HEADS // 2),
        in_specs=[
            pl.BlockSpec(memory_space=pltpu.SMEM),
            pl.BlockSpec((1, seq, HEAD_PAIR), lambda b, p: (b, 0, p)),
            pl.BlockSpec((1, seq, HEAD_PAIR), lambda b, p: (b, 0, p)),
            pl.BlockSpec((1, HEAD_PAIR, seq), lambda b, p: (b, p, 0)),
            pl.BlockSpec((2, 2, MOBA_BLOCK, MOBA_BLOCK), lambda b, p: (p, 0, 0, 0)),
        ],
        out_specs=pl.BlockSpec((1, seq, HEAD_PAIR), lambda b, p: (b, 0, p)),
        out_shape=jax.ShapeDtypeStruct((bsz, seq, ATT_WIDTH), BF16),
        scratch_shapes=[
            pltpu.VMEM((2, seq, HEAD_PAIR), BF16),
            pltpu.VMEM((2, HEAD_DIM + ONES_ROWS, seq), BF16),
            pltpu.VMEM((2, nb, nb, MOBA_BLOCK), F32),
            pltpu.VMEM((2, HEAD_DIM + ONES_ROWS, MOBA_BLOCK), F32),
            pltpu.VMEM((4, MOBA_BLOCK, MOBA_BLOCK), F32),
        ],
        compiler_params=pltpu.CompilerParams(
            dimension_semantics=("parallel", "parallel"), vmem_limit_bytes=VMEM_LIMIT),
        name="moba_attention",
    )(rel_bias.astype(F32), q, k, vt, tabs)


def _routing_weights(logits_t):
    col = lambda i: logits_t[i:i + 1, :]
    gl = [col(g) for g in range(MOE_GROUPS)]
    gmax = _max_of(gl)
    sel, taken = [], None
    for g in range(MOE_GROUPS):
        hit = gl[g] == gmax
        if taken is not None:
            hit = hit & jnp.logical_not(taken)
        taken = hit if taken is None else (taken | hit)
        sel.append(hit)
    denom = None
    for g in range(MOE_GROUPS):
        e = jnp.exp(gl[g] - gmax)
        denom = e if denom is None else denom + e
    g_wt = 1.0 / denom
    el = []
    for e in range(MOE_PER_GROUP):
        v = col(MOE_GROUPS + (MOE_GROUPS - 1) * MOE_PER_GROUP + e)
        for g in range(MOE_GROUPS - 2, -1, -1):
            v = jnp.where(sel[g], col(MOE_GROUPS + g * MOE_PER_GROUP + e), v)
        el.append(v)
    rank = []
    for e in range(MOE_PER_GROUP):
        r = jnp.zeros_like(el[e], dtype=jnp.int32)
        for m in range(MOE_PER_GROUP):
            if m == e:
                continue
            beats = (el[m] >= el[e]) if m < e else (el[m] > el[e])
            r = r + beats.astype(jnp.int32)
        rank.append(r)
    v1 = _max_of(el)
    v2 = None
    for e in range(MOE_PER_GROUP):
        c = jnp.where(rank[e] == 1, el[e], 0.0)
        v2 = c if v2 is None else v2 + c
    e2 = jnp.exp(v2 - v1)
    w1 = 1.0 / (1.0 + e2)
    w2 = e2 / (1.0 + e2)
    tokens = logits_t.shape[1]
    row_id = lax.broadcasted_iota(jnp.int32, (MOE_EXPERTS, tokens), 0)
    comb_t = jnp.zeros((MOE_EXPERTS, tokens), F32)
    for g in range(MOE_GROUPS):
        for e in range(MOE_PER_GROUP):
            within = jnp.where(rank[e] == 0, w1, jnp.where(rank[e] == 1, w2, 0.0))
            c = jnp.where(sel[g], g_wt * within, 0.0)
            comb_t = jnp.where(row_id == g * MOE_PER_GROUP + e, c, comb_t)
    pad = jnp.zeros((ROUTER_LANES - MOE_EXPERTS, tokens), F32)
    return jnp.concatenate([comb_t, pad], axis=0).T


def _max_of(cols):
    out = cols[0]
    for c in cols[1:]:
        out = jnp.maximum(out, c)
    return out


def _moe_kernel(x_ref, ys_ref, yp_ref, ya_ref, wo_ref, g_ref, wr_ref, br_ref, wg_ref, wu_ref, wd_ref, fg_ref,
                o_ref, h_ref, comb_ref, *, final_norm):
    step = pl.program_id(1)

    @pl.when(step == 0)
    def _mix_and_route():
        s0, s1 = SSM_WIDTH, SSM_WIDTH + POOL_WIDTH
        mix = jnp.dot(ys_ref[...], wo_ref[0:s0, :], preferred_element_type=F32)
        mix = mix + jnp.dot(yp_ref[...], wo_ref[s0:s1, :], preferred_element_type=F32)
        mix = mix + jnp.dot(ya_ref[...], wo_ref[s1:, :], preferred_element_type=F32)
        x = x_ref[...] + mix
        h = _rms_norm(x, g_ref[...]).astype(BF16)
        h_ref[...] = h
        logits_t = lax.dot_general(wr_ref[...], h, _NT, preferred_element_type=F32)
        comb_ref[...] = _routing_weights(logits_t + br_ref[...])
        o_ref[...] = x

    h = h_ref[...]
    lane = lax.broadcasted_iota(jnp.int32, comb_ref.shape, 1)
    y = None
    for j in range(EXPERTS_PER_STEP):
        hg = jnp.dot(h, wg_ref[j], preferred_element_type=F32)
        hu = jnp.dot(h, wu_ref[j], preferred_element_type=F32)
        c = jnp.sum(jnp.where(lane == step * EXPERTS_PER_STEP + j, comb_ref[...], 0.0), axis=1, keepdims=True)
        act = (hg * (1.0 / (1.0 + jnp.exp(-hg)))) * hu * c
        yj = jnp.dot(act.astype(BF16), wd_ref[j], preferred_element_type=F32)
        y = yj if y is None else y + yj
    o_ref[...] += y

    if final_norm:
        @pl.when(step == MOE_EXPERTS // EXPERTS_PER_STEP - 1)
        def _final():
            o_ref[...] = _rms_norm(o_ref[...], fg_ref[...])


def _mix_and_moe(x2d, y_ssm_tm, y_pool, y_att, w_out, layer, norm_g, w_router, b_router, w_gate, w_up, w_down,
                 final_g, final_norm):
    rows = x2d.shape[0]
    tiles_per_seq = y_ssm_tm.shape[0] // MOE_ROWS
    const = lambda i, e: (0, 0)
    row_blk = lambda width: pl.BlockSpec((MOE_ROWS, width), lambda i, e: (i, 0))
    experts = lambda i, e: (layer, e, 0, 0)
    return pl.pallas_call(
        partial(_moe_kernel, final_norm=final_norm),
        grid=(rows // MOE_ROWS, MOE_EXPERTS // EXPERTS_PER_STEP),
        in_specs=[
            row_blk(D_MODEL),
            pl.BlockSpec((MOE_ROWS, SSM_WIDTH), lambda i, e: (i % tiles_per_seq, i // tiles_per_seq)),
            row_blk(POOL_WIDTH),
            row_blk(ATT_WIDTH),
            pl.BlockSpec((None, D_MODEL, D_MODEL), lambda i, e: (layer, 0, 0)),
            pl.BlockSpec((1, D_MODEL), const),
            pl.BlockSpec((ROUTER_ROWS, D_MODEL), const),
            pl.BlockSpec((ROUTER_ROWS, 1), const),
            pl.BlockSpec((None, EXPERTS_PER_STEP, D_MODEL, D_EXPERT), experts),
            pl.BlockSpec((None, EXPERTS_PER_STEP, D_MODEL, D_EXPERT), experts),
            pl.BlockSpec((None, EXPERTS_PER_STEP, D_EXPERT, D_MODEL), experts),
            pl.BlockSpec((1, D_MODEL), const),
        ],
        out_specs=pl.BlockSpec((MOE_ROWS, D_MODEL), lambda i, e: (i, 0)),
        out_shape=jax.ShapeDtypeStruct((rows, D_MODEL), F32),
        scratch_shapes=[pltpu.VMEM((MOE_ROWS, D_MODEL), BF16), pltpu.VMEM((MOE_ROWS, ROUTER_LANES), F32)],
        compiler_params=pltpu.CompilerParams(dimension_semantics=("parallel", "arbitrary"),
                                             vmem_limit_bytes=VMEM_LIMIT),
        name="mix_and_moe",
    )(x2d, y_ssm_tm, y_pool, y_att, w_out, norm_g.reshape(1, D_MODEL), w_router, b_router, w_gate, w_up, w_down,
      final_g.reshape(1, D_MODEL))


def _router_params(group_w, group_b, router_w, router_b):
    w = jnp.concatenate([group_w.astype(F32)] + [router_w[g].astype(F32) for g in range(MOE_GROUPS)], axis=1)
    b = jnp.concatenate([group_b.astype(F32), router_b.astype(F32).reshape(-1)])
    pad = ROUTER_ROWS - w.shape[1]
    return jnp.pad(w.T, ((0, pad), (0, 0))).astype(BF16), jnp.pad(b, (0, pad)).reshape(ROUTER_ROWS, 1)


def kernel(x, rel_bias, norm1_g, w_in, ssm_a_re, ssm_a_im, ssm_log_dt, ssm_b_re, ssm_b_im, ssm_c_re, ssm_c_im,
           ssm_d, ssm_glu_w, ssm_glu_b, pool_w, pool_b, pool_scale, w_out, norm2_g, moe_group_w, moe_group_b,
           moe_router_w, moe_router_b, moe_w_gate, moe_w_up, moe_w_down, final_norm_g):
    bsz, seq, dm = x.shape
    depth = w_in.shape[0]
    assert dm == D_MODEL and depth >= 1
    assert bsz % SUBLANES == 0, "the scan keeps one time step of all batches in whole sublane tiles"
    assert all(seq % t == 0 for t in (PROJ_ROWS, MOE_ROWS, SSM_STEPS, MOBA_BLOCK))
    n_main = SSM_WIDTH + POOL_WIDTH + 2 * ATT_WIDTH
    x2d = x.astype(F32).reshape(bsz * seq, dm)
    tabs = _bias_tables(rel_bias)
    w_in_bf = w_in.astype(BF16)
    w_vt_bf = jnp.swapaxes(w_in[:, :, n_main:], 1, 2).astype(BF16)
    w_out_bf = w_out.astype(BF16)
    w_gate_bf, w_up_bf, w_down_bf = moe_w_gate.astype(BF16), moe_w_up.astype(BF16), moe_w_down.astype(BF16)
    ssm_params = jax.vmap(_ssm_params)(ssm_a_re, ssm_a_im, ssm_log_dt, ssm_b_re, ssm_b_im, ssm_c_re, ssm_c_im)
    for l in range(depth):
        u, y_pool, q, k, vt = _in_projection(x2d, norm1_g[l], w_in_bf, w_vt_bf, l,
                                             _pool_params(pool_w[l], pool_b[l], pool_scale[l]), bsz, seq)

        y_ssm = _ssm_mixer(u, [p[l] for p in ssm_params], ssm_d[l], ssm_glu_w[l], ssm_glu_b[l], bsz, seq)

        y_att = _moba_attention(q.reshape(bsz, seq, ATT_WIDTH), k.reshape(bsz, seq, ATT_WIDTH), vt, tabs, rel_bias)

        w_router, b_router = _router_params(moe_group_w[l], moe_group_b[l], moe_router_w[l], moe_router_b[l])
        x2d = _mix_and_moe(x2d, y_ssm, y_pool, y_att.reshape(bsz * seq, ATT_WIDTH), w_out_bf, l,
                           norm2_g[l], w_router, b_router, w_gate_bf, w_up_bf, w_down_bf,
                           final_norm_g, final_norm=(l == depth - 1))
    return x2d.reshape(bsz, seq, dm).astype(x.dtype)
```

```python
import math
from functools import partial

import jax
import jax.numpy as jnp
from jax import lax
from jax.experimental import pallas as pl
from jax.experimental.pallas import tpu as pltpu

F32 = jnp.float32
BF16 = jnp.bfloat16

LANES = 128
SUBLANES = 8
D_MODEL = 1024
SSM_WIDTH = 256
POOL_WIDTH = 256
ATT_WIDTH = 512
SSM_GROUP = 16
SSM_GROUPS = SSM_WIDTH // SSM_GROUP
SSM_STATE = 64
SSM_STATES = SSM_GROUPS * SSM_STATE
POOL_WINDOWS = (2, 4, 8, 16)
POOL_GROUP = POOL_WIDTH // len(POOL_WINDOWS)
POOL_HALO = 16
HEAD_DIM = 64
HEADS = ATT_WIDTH // HEAD_DIM
HEAD_PAIR = 2 * HEAD_DIM
ONES_ROWS = 16
MOBA_BLOCK = 256
MOBA_TOPK = 3
FAR_GROUP = 4
REL_BUCKETS = 32
REL_MAX_EXACT = REL_BUCKETS // 2
REL_MAX_DIST = 128
MOE_GROUPS = 4
MOE_PER_GROUP = 4
MOE_EXPERTS = MOE_GROUPS * MOE_PER_GROUP
D_EXPERT = D_MODEL // 4
ROUTER_ROWS = 32
ROUTER_LANES = 128
RMS_EPS = 1e-6
NEG_INF = -1e30
TAKEN = -3e38
LOG2E = math.log2(math.e)

PROJ_ROWS = 1024
MOE_ROWS = 1024
EXPERTS_PER_STEP = 4
SSM_STEPS = 128
SSM_PARTS = 2
VMEM_LIMIT = 48 * 1024 * 1024

_NT = (((1,), (1,)), ((), ()))

_BUCKET_START = list(range(REL_MAX_EXACT)) + [
    math.ceil(REL_MAX_EXACT * (REL_MAX_DIST / REL_MAX_EXACT) ** (k / (REL_BUCKETS - REL_MAX_EXACT)))
    for k in range(REL_BUCKETS - REL_MAX_EXACT)
]


def _rms_norm(x, g):
    return x * lax.rsqrt(jnp.mean(x * x, axis=-1, keepdims=True) + RMS_EPS) * g


def _inproj_kernel(x_ref, g_ref, w_ref, wvt_ref, pw_ref, pb_ref, ps_ref,
                   u_ref, yp_ref, q_ref, k_ref, vt_ref, halo_ref, *, tiles_per_seq):
    h = _rms_norm(x_ref[...], g_ref[...]).astype(BF16)
    pr = jnp.dot(h, w_ref[...], preferred_element_type=F32)
    s0, s1, s2 = SSM_WIDTH, SSM_WIDTH + POOL_WIDTH, SSM_WIDTH + POOL_WIDTH + ATT_WIDTH
    u_ref[...] = pr[:, :s0]
    t_tile = pl.program_id(0) % tiles_per_seq
    p = pr[:, s0:s1]
    halo = jnp.where(t_tile == 0, 0.0, halo_ref[...])
    halo_ref[...] = p[PROJ_ROWS - POOL_HALO:, :]
    yp_ref[...] = _pool_mix(p, halo, t_tile * PROJ_ROWS, pw_ref[...], pb_ref[...], ps_ref[...]).astype(BF16)
    q_ref[...] = (pr[:, s1:s2] * (HEAD_DIM ** -0.5 * LOG2E)).astype(BF16)
    k_ref[...] = pr[:, s2:].astype(BF16)
    vt_ref[0] = lax.dot_general(wvt_ref[...], h, _NT, preferred_element_type=F32).astype(BF16)


def _pool_mix(p, halo, t0, w_bd, b, scale):
    ext = jnp.concatenate([halo, p], axis=0)
    s2 = ext + pltpu.roll(ext, 1, 0)
    s4 = s2 + pltpu.roll(s2, 2, 0)
    s8 = s4 + pltpu.roll(s4, 4, 0)
    s16 = s8 + pltpu.roll(s8, 8, 0)
    sums = (s2, s4, s8, s16)
    lane = lax.broadcasted_iota(jnp.int32, p.shape, 1)
    t1 = (lax.broadcasted_iota(jnp.int32, p.shape, 0) + t0 + 1).astype(F32)
    mean = None
    for gi, win in enumerate(POOL_WINDOWS):
        m = sums[gi][POOL_HALO:] / jnp.minimum(t1, float(win))
        mean = m if mean is None else jnp.where(lane >= gi * POOL_GROUP, m, mean)
    y = jnp.dot((mean - p).astype(BF16), w_bd, preferred_element_type=F32)
    return (y + b) * scale


def _pool_params(w, b, scale):
    eye = jnp.eye(len(POOL_WINDOWS), dtype=F32)
    w_bd = jnp.einsum('gcd,gk->gckd', w.astype(F32), eye).reshape(POOL_WIDTH, POOL_WIDTH).astype(BF16)
    return w_bd, b.astype(F32).reshape(1, POOL_WIDTH), scale.astype(F32).reshape(1, POOL_WIDTH)


def _in_projection(x2d, g, w_in, w_vt, layer, pool_params, bsz, seq):
    rows = x2d.shape[0]
    tiles_per_seq = seq // PROJ_ROWS
    n_main = SSM_WIDTH + POOL_WIDTH + 2 * ATT_WIDTH
    const = lambda i: (0, 0)
    this_layer = lambda i: (layer, 0, 0)
    row_blk = lambda width: pl.BlockSpec((PROJ_ROWS, width), lambda i: (i, 0))
    return pl.pallas_call(
        partial(_inproj_kernel, tiles_per_seq=tiles_per_seq),
        grid=(rows // PROJ_ROWS,),
        in_specs=[row_blk(D_MODEL), pl.BlockSpec((1, D_MODEL), const),
                  pl.BlockSpec((None, D_MODEL, n_main), this_layer),
                  pl.BlockSpec((None, ATT_WIDTH, D_MODEL), this_layer),
                  pl.BlockSpec((POOL_WIDTH, POOL_WIDTH), const), pl.BlockSpec((1, POOL_WIDTH), const),
                  pl.BlockSpec((1, POOL_WIDTH), const)],
        out_specs=[pl.BlockSpec((PROJ_ROWS, SSM_WIDTH), lambda i: (i % tiles_per_seq, i // tiles_per_seq)),
                   row_blk(POOL_WIDTH), row_blk(ATT_WIDTH), row_blk(ATT_WIDTH),
                   pl.BlockSpec((1, ATT_WIDTH, PROJ_ROWS),
                                lambda i: (i // tiles_per_seq, 0, i % tiles_per_seq))],
        out_shape=[jax.ShapeDtypeStruct((seq, bsz * SSM_WIDTH), F32),
                   jax.ShapeDtypeStruct((rows, POOL_WIDTH), BF16),
                   jax.ShapeDtypeStruct((rows, ATT_WIDTH), BF16),
                   jax.ShapeDtypeStruct((rows, ATT_WIDTH), BF16),
                   jax.ShapeDtypeStruct((bsz, ATT_WIDTH, seq), BF16)],
        scratch_shapes=[pltpu.VMEM((POOL_HALO, POOL_WIDTH), F32)],
        compiler_params=pltpu.CompilerParams(dimension_semantics=("arbitrary",),
                                             vmem_limit_bytes=VMEM_LIMIT),
        name="in_projection",
    )(x2d, g.reshape(1, D_MODEL), w_in, w_vt, *pool_params)


def _ssm_kernel(u_ref, bm_ref, cm_ref, ar_ref, ai_ref, d_ref, gw_ref, gb_ref, o_ref, bu_ref, st_ref, tb_ref):
    bsz = st_ref.shape[0]
    ns = SSM_STATES

    @pl.when(pl.program_id(0) == 0)
    def _init():
        st_ref[...] = jnp.zeros_like(st_ref)

    halves = SSM_WIDTH // LANES
    for b in range(bsz):
        for h in range(halves):
            c0 = b * SSM_WIDTH + h * LANES
            tb_ref[h, pl.ds(b, SSM_STEPS, stride=bsz), :] = u_ref[:, c0:c0 + LANES]
    ar = jnp.broadcast_to(ar_ref[...], (bsz, ns))
    ai = jnp.broadcast_to(ai_ref[...], (bsz, ns))
    part_steps = SSM_STEPS // SSM_PARTS
    part_rows = part_steps * bsz

    def rows_of(part):
        return slice(part * part_rows, (part + 1) * part_rows)

    def u_of(part):
        return jnp.concatenate([tb_ref[h, rows_of(part), :] for h in range(halves)], axis=1)

    def project_in(part):
        bu_ref[rows_of(part), :] = jnp.dot(u_of(part).astype(BF16), bm_ref[...], preferred_element_type=F32)

    def scan(part, carry):
        sr, si = carry
        for t in range(part_steps):
            rows = slice(part * part_rows + t * bsz, part * part_rows + (t + 1) * bsz)
            sr, si = (ar * sr - ai * si + bu_ref[rows, 0:ns], ar * si + ai * sr + bu_ref[rows, ns:2 * ns])
            bu_ref[rows, 0:ns] = sr
            bu_ref[rows, ns:2 * ns] = si
        return sr, si

    def project_out(part):
        slab = part_rows // 2
        y = jnp.concatenate(
            [jnp.dot(bu_ref[part * part_rows + i * slab:part * part_rows + (i + 1) * slab, :].astype(BF16),
                     cm_ref[...], preferred_element_type=F32) for i in range(2)], axis=0)
        y = y + d_ref[...] * u_of(part)
        y = y * (0.5 * (1.0 + jnp.tanh(math.sqrt(2.0 / math.pi) * (y + 0.044715 * (y * y * y)))))
        z = jnp.dot(y.astype(BF16), gw_ref[...], preferred_element_type=F32) + gb_ref[...]
        return y * (1.0 / (1.0 + jnp.exp(-z)))

    project_in(0)
    state, outs = (st_ref[:, 0:ns], st_ref[:, ns:2 * ns]), []
    for part in range(SSM_PARTS):
        if part + 1 < SSM_PARTS:
            project_in(part + 1)
        state = scan(part, state)
        outs.append(project_out(part))
    st_ref[:, 0:ns] = state[0]
    st_ref[:, ns:2 * ns] = state[1]
    out = jnp.concatenate(outs, axis=0)
    for h in range(halves):
        tb_ref[h] = out[:, h * LANES:(h + 1) * LANES]
    for b in range(bsz):
        for h in range(halves):
            c0 = b * SSM_WIDTH + h * LANES
            o_ref[:, c0:c0 + LANES] = tb_ref[h, pl.ds(b, SSM_STEPS, stride=bsz), :].astype(o_ref.dtype)


def _ssm_params(a_re, a_im, log_dt, b_re, b_im, c_re, c_im):
    lam = lax.complex(a_re.astype(F32), a_im.astype(F32))
    dt = jnp.exp(log_dt.astype(F32))[:, None]
    lam_bar = jnp.exp(lam * dt)
    b = lax.complex(b_re.astype(F32), b_im.astype(F32))
    b_bar = ((lam_bar - 1.0) / lam)[..., None] * b
    eye = jnp.eye(SSM_GROUPS, dtype=F32)

    def bdiag_in(m):
        return jnp.einsum('gph,gk->ghkp', m, eye).reshape(SSM_WIDTH, SSM_STATES)

    def bdiag_out(m):
        return jnp.einsum('ghp,gk->gpkh', m, eye).reshape(SSM_STATES, SSM_WIDTH)

    bm = jnp.concatenate([bdiag_in(jnp.real(b_bar)), bdiag_in(jnp.imag(b_bar))], axis=1)
    cm = jnp.concatenate([bdiag_out(c_re.astype(F32)), -bdiag_out(c_im.astype(F32))], axis=0)
    return (bm.astype(BF16), cm.astype(BF16),
            jnp.real(lam_bar).reshape(1, SSM_STATES), jnp.imag(lam_bar).reshape(1, SSM_STATES))


def _ssm_mixer(u_tm, params, d, glu_w, glu_b, bsz, seq):
    bm, cm, ar, ai = params
    rows = SSM_STEPS * bsz
    const = lambda c: (0, 0)
    return pl.pallas_call(
        _ssm_kernel,
        grid=(seq // SSM_STEPS,),
        in_specs=[
            pl.BlockSpec((SSM_STEPS, bsz * SSM_WIDTH), lambda c: (c, 0)),
            pl.BlockSpec((SSM_WIDTH, 2 * SSM_STATES), const),
            pl.BlockSpec((2 * SSM_STATES, SSM_WIDTH), const),
            pl.BlockSpec((1, SSM_STATES), const),
            pl.BlockSpec((1, SSM_STATES), const),
            pl.BlockSpec((1, SSM_WIDTH), const),
            pl.BlockSpec((SSM_WIDTH, SSM_WIDTH), const),
            pl.BlockSpec((1, SSM_WIDTH), const),
        ],
        out_specs=pl.BlockSpec((SSM_STEPS, bsz * SSM_WIDTH), lambda c: (c, 0)),
        out_shape=jax.ShapeDtypeStruct((seq, bsz * SSM_WIDTH), BF16),
        scratch_shapes=[pltpu.VMEM((rows, 2 * SSM_STATES), F32), pltpu.VMEM((bsz, 2 * SSM_STATES), F32),
                        pltpu.VMEM((SSM_WIDTH // LANES, rows, LANES), F32)],
        compiler_params=pltpu.CompilerParams(dimension_semantics=("arbitrary",),
                                             vmem_limit_bytes=VMEM_LIMIT),
        name="ssm_mixer",
    )(u_tm, bm, cm, ar, ai, d.reshape(1, SSM_WIDTH), glu_w.astype(BF16), glu_b.reshape(1, SSM_WIDTH))


def _bias_table_kernel(bias_ref, tab_ref):
    h = pl.program_id(0)
    kk = lax.broadcasted_iota(jnp.int32, (MOBA_BLOCK, MOBA_BLOCK), 0)
    qq = lax.broadcasted_iota(jnp.int32, (MOBA_BLOCK, MOBA_BLOCK), 1)
    for which in range(2):
        rel = qq - kk + which * MOBA_BLOCK
        val = jnp.full((MOBA_BLOCK, MOBA_BLOCK), bias_ref[0, h], F32)
        for b in range(1, REL_BUCKETS):
            val = jnp.where(rel >= _BUCKET_START[b], bias_ref[b, h], val)
        tab_ref[0, which] = jnp.where(rel >= 0, val * LOG2E, NEG_INF)


def _bias_tables(rel_bias):
    return pl.pallas_call(
        _bias_table_kernel,
        grid=(HEADS,),
        in_specs=[pl.BlockSpec(memory_space=pltpu.SMEM)],
        out_specs=pl.BlockSpec((1, 2, MOBA_BLOCK, MOBA_BLOCK), lambda h: (h, 0, 0, 0)),
        out_shape=jax.ShapeDtypeStruct((HEADS, 2, MOBA_BLOCK, MOBA_BLOCK), F32),
        name="bias_tables",
    )(rel_bias.astype(F32))


def _attn_kernel(bias_ref, q_ref, k_ref, vt_ref, tab_ref, o_ref,
                 qh_ref, va_ref, neg_ref, acc_ref, s_ref):
    seq = k_ref.shape[1]
    nb = seq // MOBA_BLOCK

    def prep():
        lane = lax.broadcasted_iota(jnp.int32, (1, HEAD_PAIR), 1)
        head_lanes = [lane < HEAD_DIM, lane >= HEAD_DIM]
        kf = k_ref[0].astype(F32).reshape(nb, MOBA_BLOCK, HEAD_PAIR)
        km = jnp.sum(kf, axis=1) * (1.0 / MOBA_BLOCK)
        q_all = q_ref[0].astype(F32)
        ones = jnp.ones((ONES_ROWS, seq), BF16)
        va_ref[0, 0:HEAD_DIM, :] = vt_ref[0, 0:HEAD_DIM, :]
        va_ref[0, HEAD_DIM:, :] = ones
        va_ref[1, 0:ONES_ROWS, :] = ones
        va_ref[1, ONES_ROWS:, :] = vt_ref[0, HEAD_DIM:, :]
        blk_id = lax.broadcasted_iota(jnp.int32, (nb, seq), 0)
        past = blk_id < lax.broadcasted_iota(jnp.int32, (nb, seq), 1) // MOBA_BLOCK
        for hh in range(2):
            qh_ref[hh] = jnp.where(head_lanes[hh], q_all, 0.0).astype(BF16)
            km_h = jnp.where(head_lanes[hh], km, 0.0).astype(BF16)
            g = lax.dot_general(km_h, qh_ref[hh], _NT, preferred_element_type=F32)
            g = jnp.where(past, g, NEG_INF)
            sel = None
            for _ in range(min(MOBA_TOPK, nb)):
                top = jnp.max(g, axis=0, keepdims=True)
                first = jnp.min(jnp.where(g == top, blk_id, nb), axis=0, keepdims=True)
                hit = blk_id == first
                sel = hit if sel is None else (sel | hit)
                g = jnp.where(hit, TAKEN, g)
            neg = jnp.where(sel & past, 0.0, NEG_INF)
            for qb in range(nb):
                neg_ref[hh, qb] = neg[:, qb * MOBA_BLOCK:(qb + 1) * MOBA_BLOCK]

    prep()

    for qv in range(nb):
        _attn_q_block_static(qv, pl.program_id(1), nb, bias_ref, k_ref, tab_ref, o_ref,
                             qh_ref, va_ref, neg_ref, acc_ref, s_ref)


def _attn_q_block_static(qv, pr, nb, bias_ref, k_ref, tab_ref, o_ref, qh_ref, va_ref, neg_ref, acc_ref, s_ref):
    rows = lambda j: slice(j * MOBA_BLOCK, (j + 1) * MOBA_BLOCK)
    qh = [qh_ref[hh, rows(qv), :] for hh in range(2)]

    def score(hh, j):
        return lax.dot_general(k_ref[0, rows(j), :], qh[hh], _NT, preferred_element_type=F32)

    def finalize(qb):
        oa = acc_ref[0]
        ob = acc_ref[1]
        out_t = jnp.concatenate([oa[0:HEAD_DIM] / oa[HEAD_DIM:HEAD_DIM + 1],
                                 ob[ONES_ROWS:] / ob[0:1]], axis=0)
        o_ref[0, rows(qb), :] = out_t.T.astype(o_ref.dtype)

    def add_of(hh, j):
        if j == qv:
            return tab_ref[hh, 0]
        mask_row = neg_ref[hh, qv, j:j + 1, :]
        if j == qv - 1:
            return tab_ref[hh, 1] + mask_row
        return mask_row + bias_ref[REL_BUCKETS - 1, 2 * pr + hh] * LOG2E

    if qv > 0:
        finalize(qv - 1)
    blocks = [qv] + ([qv - 1] if qv > 0 else []) + list(range(qv - 1))
    groups = [blocks[i:i + FAR_GROUP] for i in range(0, len(blocks), FAR_GROUP)]
    scores = {(hh, j): score(hh, j) for j in groups[0] for hh in range(2)}
    m, acc, staged = [None, None], [None, None], 0
    for gi, group in enumerate(groups):
        units = [(hh, j) for j in group for hh in range(2)]
        ahead = [(hh, j) for j in (groups[gi + 1] if gi + 1 < len(groups) else []) for hh in range(2)]
        per_unit = -(-len(ahead) // len(units))
        nxt, mbs, pvs = {}, {}, {}
        for n, (hh, j) in enumerate(units):
            for ahh, aj in ahead[n * per_unit:(n + 1) * per_unit]:
                nxt[ahh, aj] = score(ahh, aj)
            s, add = scores[hh, j], add_of(hh, j)
            if add.shape[0] == 1:
                smax = jnp.max(s, axis=0, keepdims=True)
                mbs[hh, j] = smax + add
                p = jnp.exp2(s - smax)
            else:
                s_ref[staged] = s
                s = s_ref[staged] + add
                staged += 1
                mbs[hh, j] = jnp.max(s, axis=0, keepdims=True)
                p = jnp.exp2(s - mbs[hh, j])
            pvs[hh, j] = jnp.dot(va_ref[hh, :, rows(j)], p.astype(BF16), preferred_element_type=F32)
        for hh in range(2):
            m_new = m[hh]
            for j in group:
                m_new = mbs[hh, j] if m_new is None else jnp.maximum(m_new, mbs[hh, j])
            total = None if m[hh] is None else acc[hh] * jnp.exp2(m[hh] - m_new)
            for j in group:
                term = pvs[hh, j] * jnp.exp2(mbs[hh, j] - m_new)
                total = term if total is None else total + term
            m[hh], acc[hh] = m_new, total
        scores = nxt
    acc_ref[0] = acc[0]
    acc_ref[1] = acc[1]
    if qv == nb - 1:
        finalize(qv)


def _moba_attention(q, k, vt, tabs, rel_bias):
    bsz, seq, _ = q.shape
    assert seq % MOBA_BLOCK == 0
    nb = seq // MOBA_BLOCK
    return pl.pallas_call(
        _attn_kernel,
        grid=(bsz, HEADS // 2),
        in_specs=[
            pl.BlockSpec(memory_space=pltpu.SMEM),
            pl.BlockSpec((1, seq, HEAD_PAIR), lambda b, p: (b, 0, p)),
            pl.BlockSpec((1, seq, HEAD_PAIR), lambda b, p: (b, 0, p)),
            pl.BlockSpec((1, HEAD_PAIR, seq), lambda b, p: (b, p, 0)),
            pl.BlockSpec((2, 2, MOBA_BLOCK, MOBA_BLOCK), lambda b, p: (p, 0, 0, 0)),
        ],
        out_specs=pl.BlockSpec((1, seq, HEAD_PAIR), lambda b, p: (b, 0, p)),
        out_shape=jax.ShapeDtypeStruct((bsz, seq, ATT_WIDTH), BF16),
        scratch_shapes=[
            pltpu.VMEM((2, seq, HEAD_PAIR), BF16),
            pltpu.VMEM((2, HEAD_DIM + ONES_ROWS, seq), BF16),
            pltpu.VMEM((2, nb, nb, MOBA_BLOCK), F32),
            pltpu.VMEM((2, HEAD_DIM + ONES_ROWS, MOBA_BLOCK), F32),
            pltpu.VMEM((4, MOBA_BLOCK, MOBA_BLOCK), F32),
        ],
        compiler_params=pltpu.CompilerParams(
            dimension_semantics=("parallel", "parallel"), vmem_limit_bytes=VMEM_LIMIT),
        name="moba_attention",
    )(rel_bias.astype(F32), q, k, vt, tabs)


def _routing_weights(logits_t):
    col = lambda i: logits_t[i:i + 1, :]
    gl = [col(g) for g in range(MOE_GROUPS)]
    gmax = _max_of(gl)
    sel, taken = [], None
    for g in range(MOE_GROUPS):
        hit = gl[g] == gmax
        if taken is not None:
            hit = hit & jnp.logical_not(taken)
        taken = hit if taken is None else (taken | hit)
        sel.append(hit)
    denom = None
    for g in range(MOE_GROUPS):
        e = jnp.exp(gl[g] - gmax)
        denom = e if denom is None else denom + e
    g_wt = 1.0 / denom
    el = []
    for e in range(MOE_PER_GROUP):
        v = col(MOE_GROUPS + (MOE_GROUPS - 1) * MOE_PER_GROUP + e)
        for g in range(MOE_GROUPS - 2, -1, -1):
            v = jnp.where(sel[g], col(MOE_GROUPS + g * MOE_PER_GROUP + e), v)
        el.append(v)
    rank = []
    for e in range(MOE_PER_GROUP):
        r = jnp.zeros_like(el[e], dtype=jnp.int32)
        for m in range(MOE_PER_GROUP):
            if m == e:
                continue
            beats = (el[m] >= el[e]) if m < e else (el[m] > el[e])
            r = r + beats.astype(jnp.int32)
        rank.append(r)
    v1 = _max_of(el)
    v2 = None
    for e in range(MOE_PER_GROUP):
        c = jnp.where(rank[e] == 1, el[e], 0.0)
        v2 = c if v2 is None else v2 + c
    e2 = jnp.exp(v2 - v1)
    w1 = 1.0 / (1.0 + e2)
    w2 = e2 / (1.0 + e2)
    tokens = logits_t.shape[1]
    row_id = lax.broadcasted_iota(jnp.int32, (MOE_EXPERTS, tokens), 0)
    comb_t = jnp.zeros((MOE_EXPERTS, tokens), F32)
    for g in range(MOE_GROUPS):
        for e in range(MOE_PER_GROUP):
            within = jnp.where(rank[e] == 0, w1, jnp.where(rank[e] == 1, w2, 0.0))
            c = jnp.where(sel[g], g_wt * within, 0.0)
            comb_t = jnp.where(row_id == g * MOE_PER_GROUP + e, c, comb_t)
    pad = jnp.zeros((ROUTER_LANES - MOE_EXPERTS, tokens), F32)
    return jnp.concatenate([comb_t, pad], axis=0).T


def _max_of(cols):
    out = cols[0]
    for c in cols[1:]:
        out = jnp.maximum(out, c)
    return out


def _moe_kernel(x_ref, ys_ref, yp_ref, ya_ref, wo_ref, g_ref, wr_ref, br_ref, wg_ref, wu_ref, wd_ref, fg_ref,
                o_ref, h_ref, comb_ref, *, final_norm):
    step = pl.program_id(1)

    def experts(h, comb, base):
        lane = lax.broadcasted_iota(jnp.int32, comb.shape, 1)
        y = None
        for j in range(EXPERTS_PER_STEP):
            hg = jnp.dot(h, wg_ref[j], preferred_element_type=F32)
            hu = jnp.dot(h, wu_ref[j], preferred_element_type=F32)
            c = jnp.sum(jnp.where(lane == base + j, comb, 0.0), axis=1, keepdims=True)
            act = (hg * (1.0 / (1.0 + jnp.exp(-hg)))) * hu * c
            yj = jnp.dot(act.astype(BF16), wd_ref[j], preferred_element_type=F32)
            y = yj if y is None else y + yj
        return y

    @pl.when(step == 0)
    def _mix_route_and_first_experts():
        s0, s1 = SSM_WIDTH, SSM_WIDTH + POOL_WIDTH
        mix = jnp.dot(ys_ref[...], wo_ref[0:s0, :], preferred_element_type=F32)
        mix = mix + jnp.dot(yp_ref[...], wo_ref[s0:s1, :], preferred_element_type=F32)
        mix = mix + jnp.dot(ya_ref[...], wo_ref[s1:, :], preferred_element_type=F32)
        x = x_ref[...] + mix
        h = _rms_norm(x, g_ref[...]).astype(BF16)
        h_ref[...] = h
        logits_t = lax.dot_general(wr_ref[...], h, _NT, preferred_element_type=F32)
        comb = _routing_weights(logits_t + br_ref[...])
        comb_ref[...] = comb
        o_ref[...] = x + experts(h, comb, 0)

    @pl.when(step > 0)
    def _more_experts():
        o_ref[...] += experts(h_ref[...], comb_ref[...], step * EXPERTS_PER_STEP)

    if final_norm:
        @pl.when(step == MOE_EXPERTS // EXPERTS_PER_STEP - 1)
        def _final():
            o_ref[...] = _rms_norm(o_ref[...], fg_ref[...])


def _mix_and_moe(x2d, y_ssm_tm, y_pool, y_att, w_out, layer, norm_g, w_router, b_router, w_gate, w_up, w_down,
                 final_g, final_norm):
    rows = x2d.shape[0]
    tiles_per_seq = y_ssm_tm.shape[0] // MOE_ROWS
    const = lambda i, e: (0, 0)
    row_blk = lambda width: pl.BlockSpec((MOE_ROWS, width), lambda i, e: (i, 0))
    experts = lambda i, e: (layer, e, 0, 0)
    return pl.pallas_call(
        partial(_moe_kernel, final_norm=final_norm),
        grid=(rows // MOE_ROWS, MOE_EXPERTS // EXPERTS_PER_STEP),
        in_specs=[
            row_blk(D_MODEL),
            pl.BlockSpec((MOE_ROWS, SSM_WIDTH), lambda i, e: (i % tiles_per_seq, i // tiles_per_seq)),
            row_blk(POOL_WIDTH),
            row_blk(ATT_WIDTH),
            pl.BlockSpec((None, D_MODEL, D_MODEL), lambda i, e: (layer, 0, 0)),
            pl.BlockSpec((1, D_MODEL), const),
            pl.BlockSpec((ROUTER_ROWS, D_MODEL), const),
            pl.BlockSpec((ROUTER_ROWS, 1), const),
            pl.BlockSpec((None, EXPERTS_PER_STEP, D_MODEL, D_EXPERT), experts),
            pl.BlockSpec((None, EXPERTS_PER_STEP, D_MODEL, D_EXPERT), experts),
            pl.BlockSpec((None, EXPERTS_PER_STEP, D_EXPERT, D_MODEL), experts),
            pl.BlockSpec((1, D_MODEL), const),
        ],
        out_specs=pl.BlockSpec((MOE_ROWS, D_MODEL), lambda i, e: (i, 0)),
        out_shape=jax.ShapeDtypeStruct((rows, D_MODEL), F32),
        scratch_shapes=[pltpu.VMEM((MOE_ROWS, D_MODEL), BF16), pltpu.VMEM((MOE_ROWS, ROUTER_LANES), F32)],
        compiler_params=pltpu.CompilerParams(dimension_semantics=("parallel", "arbitrary"),
                                             vmem_limit_bytes=VMEM_LIMIT),
        name="mix_and_moe",
    )(x2d, y_ssm_tm, y_pool, y_att, w_out, norm_g.reshape(1, D_MODEL), w_router, b_router, w_gate, w_up, w_down,
      final_g.reshape(1, D_MODEL))


def _router_params(group_w, group_b, router_w, router_b):
    w = jnp.concatenate([group_w.astype(F32)] + [router_w[g].astype(F32) for g in range(MOE_GROUPS)], axis=1)
    b = jnp.concatenate([group_b.astype(F32), router_b.astype(F32).reshape(-1)])
    pad = ROUTER_ROWS - w.shape[1]
    return jnp.pad(w.T, ((0, pad), (0, 0))).astype(BF16), jnp.pad(b, (0, pad)).reshape(ROUTER_ROWS, 1)


def kernel(x, rel_bias, norm1_g, w_in, ssm_a_re, ssm_a_im, ssm_log_dt, ssm_b_re, ssm_b_im, ssm_c_re, ssm_c_im,
           ssm_d, ssm_glu_w, ssm_glu_b, pool_w, pool_b, pool_scale, w_out, norm2_g, moe_group_w, moe_group_b,
           moe_router_w, moe_router_b, moe_w_gate, moe_w_up, moe_w_down, final_norm_g):
    bsz, seq, dm = x.shape
    depth = w_in.shape[0]
    assert dm == D_MODEL and depth >= 1
    assert bsz % SUBLANES == 0, "the scan keeps one time step of all batches in whole sublane tiles"
    assert all(seq % t == 0 for t in (PROJ_ROWS, MOE_ROWS, SSM_STEPS, MOBA_BLOCK))
    n_main = SSM_WIDTH + POOL_WIDTH + 2 * ATT_WIDTH
    x2d = x.astype(F32).reshape(bsz * seq, dm)
    tabs = _bias_tables(rel_bias)
    w_in_bf = w_in.astype(BF16)
    w_vt_bf = jnp.swapaxes(w_in[:, :, n_main:], 1, 2).astype(BF16)
    w_out_bf = w_out.astype(BF16)
    w_gate_bf, w_up_bf, w_down_bf = moe_w_gate.astype(BF16), moe_w_up.astype(BF16), moe_w_down.astype(BF16)
    ssm_params = jax.vmap(_ssm_params)(ssm_a_re, ssm_a_im, ssm_log_dt, ssm_b_re, ssm_b_im, ssm_c_re, ssm_c_im)
    for l in range(depth):
        u, y_pool, q, k, vt = _in_projection(x2d, norm1_g[l], w_in_bf, w_vt_bf, l,
                                             _pool_params(pool_w[l], pool_b[l], pool_scale[l]), bsz, seq)

        y_ssm = _ssm_mixer(u, [p[l] for p in ssm_params], ssm_d[l], ssm_glu_w[l], ssm_glu_b[l], bsz, seq)

        y_att = _moba_attention(q.reshape(bsz, seq, ATT_WIDTH), k.reshape(bsz, seq, ATT_WIDTH), vt, tabs, rel_bias)

        w_router, b_router = _router_params(moe_group_w[l], moe_group_b[l], moe_router_w[l], moe_router_b[l])
        x2d = _mix_and_moe(x2d, y_ssm, y_pool, y_att.reshape(bsz * seq, ATT_WIDTH), w_out_bf, l,
                           norm2_g[l], w_router, b_router, w_gate_bf, w_up_bf, w_down_bf,
                           final_norm_g, final_norm=(l == depth - 1))
    return x2d.reshape(bsz, seq, dm).astype(x.dtype)
```
